```python
import math
import jax, jax.numpy as jnp
from jax import lax
import numpy as np

D_MODEL = 1024
BATCH = 8
SEQ = 2048
DEPTH = 2
DEC_BATCH = 32
DEC_SEQ = 4
PAST_LEN = 16384
PAGE_SIZE = 128

N_BRANCH = 4
BR_WIDTH = D_MODEL // 4
FOX_HEADS = 4
FOX_HD = BR_WIDTH // FOX_HEADS
LRU_WIDTH = BR_WIDTH
LRU_BLOCKS = 4
LRU_BD = LRU_WIDTH // LRU_BLOCKS
LRU_C = 8.0
CONV_W = 4
DIFF_HEADS = 4
DIFF_VD = BR_WIDTH // DIFF_HEADS
DIFF_HD = DIFF_VD // 2
GDN_HEADS = 4
GDN_HD = BR_WIDTH // GDN_HEADS
GDN_CHUNK = 64
MEM_TOKENS = 256
MEM_HEADS = 4
MEM_HD = 64
MEM_WIDTH = MEM_HEADS * MEM_HD
D_FF = 4 * D_MODEL
ROPE_THETA = 10000.0
Q_BLOCK = 128
NORM_EPS = 1e-6
IN_SIZES = (3 * BR_WIDTH, FOX_HEADS, LRU_WIDTH, LRU_WIDTH, 3 * BR_WIDTH, 3 * BR_WIDTH, GDN_HEADS, GDN_HEADS, BR_WIDTH, N_BRANCH * D_MODEL)
D_IN = 3 * BR_WIDTH + FOX_HEADS + 2 * LRU_WIDTH + 3 * BR_WIDTH + 3 * BR_WIDTH + 2 * GDN_HEADS + BR_WIDTH + N_BRANCH * D_MODEL

kernel_name = 'hybrid_fox_rglru_diff_gdn_decode_step'


def rmsnorm(x, g):
    xf = x.astype(jnp.float32)
    y = xf * lax.rsqrt(jnp.mean(xf * xf, axis=-1, keepdims=True) + NORM_EPS)
    return (y * g.astype(jnp.float32)).astype(x.dtype)


def l2norm(x):
    return x * lax.rsqrt(jnp.sum(x * x, axis=-1, keepdims=True) + NORM_EPS)


def rope(x, pos):
    d = x.shape[-1]
    inv = ROPE_THETA ** (-jnp.arange(0, d, 2, dtype=jnp.float32) / d)
    ang = pos.astype(jnp.float32)[:, None] * inv[None, :]
    ang = ang.reshape((1, ang.shape[0]) + (1,) * (x.ndim - 3) + (d // 2,))
    cos, sin = jnp.cos(ang), jnp.sin(ang)
    xf = x.astype(jnp.float32)
    x1, x2 = xf[..., : d // 2], xf[..., d // 2:]
    return jnp.concatenate([x1 * cos - x2 * sin, x2 * cos + x1 * sin], axis=-1).astype(x.dtype)


def causal_conv(x, buf, w, b=None):
    xp = jnp.concatenate([buf.astype(x.dtype), x], axis=1)
    T = x.shape[1]
    y = xp[:, 0:T] * w[0]
    for i in range(1, CONV_W):
        y = y + xp[:, i:i + T] * w[i]
    if b is not None:
        y = y + b
    return y, xp[:, -(CONV_W - 1):]


def gather_pages(cache, page_table):
    g = cache[page_table]
    return g.reshape((g.shape[0], -1) + g.shape[3:])


def block_attention(q, k, v, q_pos, c_q=None, c_k=None):
    B, Tq, H, M, d = q.shape
    Tk = k.shape[1]
    k_pos = jnp.arange(Tk)
    qb = Q_BLOCK if Tq % Q_BLOCK == 0 else Tq
    nb = Tq // qb
    scale = 1.0 / math.sqrt(d)
    kf = k.astype(jnp.float32)
    vf = v.astype(jnp.float32)
    ck = None if c_k is None else jnp.swapaxes(c_k.astype(jnp.float32), 1, 2)

    def one_block(blk):
        q_blk, pos_blk, cq_blk = blk
        s = jnp.einsum('bqhmd,bkhmd->bhmqk', q_blk.astype(jnp.float32), kf) * scale
        if ck is not None:
            cq = jnp.swapaxes(cq_blk, 1, 2)
            s = s + (cq[:, :, None, :, None] - ck[:, :, None, None, :])
        s = jnp.where(k_pos[None, :] <= pos_blk[:, None], s, -jnp.inf)
        p_ = jax.nn.softmax(s, axis=-1)
        return jnp.einsum('bhmqk,bkhe->bqhme', p_, vf)

    q_blocks = jnp.swapaxes(q.reshape(B, nb, qb, H, M, d), 0, 1)
    pos_blocks = q_pos.reshape(nb, qb)
    cq_blocks = None if c_q is None else jnp.swapaxes(c_q.astype(jnp.float32).reshape(B, nb, qb, H), 0, 1)
    out = lax.map(one_block, (q_blocks, pos_blocks, cq_blocks))
    return jnp.swapaxes(out, 0, 1).reshape(B, Tq, H, M, v.shape[-1])


def gated_delta_chunked(q, k, v, g, beta, S0):
    B, T, H, dk = k.shape
    dv = v.shape[-1]
    C = GDN_CHUNK if T % GDN_CHUNK == 0 else T
    N = T // C

    def chunk4(a):
        a = jnp.swapaxes(a, 1, 2)
        return jnp.moveaxis(a.reshape(B, H, N, C, a.shape[-1]), 2, 0)

    def chunk3(a):
        a = jnp.swapaxes(a, 1, 2)
        return jnp.moveaxis(a.reshape(B, H, N, C), 2, 0)

    tril = jnp.tril(jnp.ones((C, C), dtype=bool))
    strict = jnp.tril(jnp.ones((C, C), dtype=bool), -1)
    eye = jnp.eye(C, dtype=jnp.float32)

    def step(S, blk):
        qc, kc, vc, gc, bc = blk
        gcum = jnp.cumsum(gc, axis=-1)
        diff = gcum[..., :, None] - gcum[..., None, :]
        decay = jnp.where(tril, jnp.exp(jnp.where(tril, diff, 0.0)), 0.0)
        kk = jnp.einsum('bhid,bhjd->bhij', kc, kc)
        a_mat = jnp.where(strict, kk * decay * bc[..., :, None], 0.0) + eye
        rhs = jnp.concatenate([vc * bc[..., None], kc * (bc * jnp.exp(gcum))[..., None]], axis=-1)
        sol = lax.linalg.triangular_solve(a_mat, rhs, left_side=True, lower=True, unit_diagonal=True)
        u, w = sol[..., :dv], sol[..., dv:]
        v_new = u - jnp.einsum('bhck,bhkv->bhcv', w, S)
        qk = jnp.einsum('bhid,bhjd->bhij', qc, kc) * decay
        o = jnp.einsum('bhck,bhkv->bhcv', qc * jnp.exp(gcum)[..., None], S) + jnp.einsum('bhij,bhjv->bhiv', qk, v_new)
        g_last = gcum[..., -1:]
        S_new = S * jnp.exp(g_last)[..., None] + jnp.einsum('bhck,bhcv->bhkv', kc * jnp.exp(g_last - gcum)[..., None], v_new)
        return S_new, o

    S, o = lax.scan(step, S0.astype(jnp.float32), (chunk4(q), chunk4(k), chunk4(v), chunk3(g), chunk3(beta)))
    o = jnp.moveaxis(o, 0, 2).reshape(B, H, T, dv)
    return jnp.swapaxes(o, 1, 2), S


def mixer_sublayer(x, pos, p, lam_init, fox_past, diff_past, lru_h0, lru_buf, gdn_S0, gdn_buf):
    B, T, _ = x.shape
    f32 = jnp.float32
    h = rmsnorm(x, p['norm_mix'])
    z = h @ p['w_in']
    (fox_qkv, fox_f, lru_x, lru_g, diff_qkv, gdn_qkv, gdn_a, gdn_b, gdn_z, gates) = jnp.split(
        z, np.cumsum(IN_SIZES)[:-1].tolist(), axis=-1)

    r = fox_qkv.reshape(B, T, 3, FOX_HEADS, FOX_HD)
    fq = rmsnorm(r[:, :, 0], p['fox_q_norm'])
    fk = rmsnorm(r[:, :, 1], p['fox_k_norm'])
    fv = r[:, :, 2]
    flogf = jax.nn.log_sigmoid(fox_f.astype(f32) + p['fox_b_f'].astype(f32))
    if fox_past is None:
        ka, va, la = fk, fv, flogf
    else:
        ka = jnp.concatenate([fox_past[0].astype(fk.dtype), fk], axis=1)
        va = jnp.concatenate([fox_past[1].astype(fv.dtype), fv], axis=1)
        la = jnp.concatenate([fox_past[2].astype(f32), flogf], axis=1)
    c = jnp.cumsum(la, axis=1)
    fo = block_attention(fq[:, :, :, None], ka[:, :, :, None], va, pos, c[:, -T:], c)[:, :, :, 0]
    fox_out = fo.reshape(B, T, BR_WIDTH).astype(x.dtype)

    xc, lru_buf_new = causal_conv(lru_x, lru_buf, p['lru_conv_w'], p['lru_conv_b'])
    xf = xc.astype(f32)
    xblk = xf.reshape(B, T, LRU_BLOCKS, LRU_BD)
    rg = jax.nn.sigmoid(jnp.einsum('btni,nij->btnj', xblk, p['lru_w_a'].astype(f32)).reshape(B, T, LRU_WIDTH) + p['lru_b_a'])
    ig = jax.nn.sigmoid(jnp.einsum('btni,nij->btnj', xblk, p['lru_w_x'].astype(f32)).reshape(B, T, LRU_WIDTH) + p['lru_b_x'])
    log_a = -LRU_C * rg * jax.nn.softplus(-p['lru_lambda'].astype(f32))
    a = jnp.exp(log_a)
    bterm = jnp.sqrt(-jnp.expm1(2.0 * log_a)) * (ig * xf)
    bterm = bterm.at[:, 0].add(a[:, 0] * lru_h0.astype(f32))

    def lin_comb(lhs, rhs):
        return lhs[0] * rhs[0], rhs[0] * lhs[1] + rhs[1]

    _, hs = lax.associative_scan(lin_comb, (a, bterm), axis=1)
    lru_out = (hs * jax.nn.gelu(lru_g.astype(f32))).astype(x.dtype)
    lru_hT = hs[:, -1]

    r = diff_qkv.reshape(B, T, 3, DIFF_HEADS, DIFF_VD)
    dq = rope(rmsnorm(r[:, :, 0].reshape(B, T, DIFF_HEADS, 2, DIFF_HD), p['diff_q_norm']), pos)
    dk = rope(rmsnorm(r[:, :, 1].reshape(B, T, DIFF_HEADS, 2, DIFF_HD), p['diff_k_norm']), pos)
    dvv = r[:, :, 2]
    if diff_past is None:
        dka, dva = dk, dvv
    else:
        dka = jnp.concatenate([diff_past[0].astype(dk.dtype), dk], axis=1)
        dva = jnp.concatenate([diff_past[1].astype(dvv.dtype), dvv], axis=1)
    do = block_attention(dq, dka, dva, pos)
    lp = p['diff_lambda'].astype(f32)
    lam = jnp.exp(jnp.sum(lp[0] * lp[1])) - jnp.exp(jnp.sum(lp[2] * lp[3])) + lam_init
    do = do[:, :, :, 0] - lam * do[:, :, :, 1]
    do = rmsnorm(do, p['diff_sub_norm']) * (1.0 - lam_init)
    diff_out = do.reshape(B, T, BR_WIDTH).astype(x.dtype)

    gc, gdn_buf_new = causal_conv(gdn_qkv, gdn_buf, p['gdn_conv_w'])
    gc = jax.nn.silu(gc.astype(f32)).reshape(B, T, 3, GDN_HEADS, GDN_HD)
    gq = l2norm(gc[:, :, 0]) * (GDN_HD ** -0.5)
    gk = l2norm(gc[:, :, 1])
    gvv = gc[:, :, 2]
    gbeta = jax.nn.sigmoid(gdn_b.astype(f32))
    gg = -jnp.exp(p['gdn_A_log'].astype(f32)) * jax.nn.softplus(gdn_a.astype(f32) + p['gdn_dt_bias'].astype(f32))
    go, gS = gated_delta_chunked(gq, gk, gvv, gg, gbeta, gdn_S0)
    go = rmsnorm(go, p['gdn_out_norm']) * jax.nn.silu(gdn_z.astype(f32).reshape(B, T, GDN_HEADS, GDN_HD))
    gdn_out = go.reshape(B, T, BR_WIDTH).astype(x.dtype)

    branches = jnp.stack([fox_out, lru_out, diff_out, gdn_out], axis=2)
    proj = jnp.einsum('btnc,ncd->btnd', branches, p['w_branch'])
    gate = jax.nn.sigmoid(gates.reshape(B, T, N_BRANCH, D_MODEL).astype(f32))
    merged = jnp.sum(gate * proj.astype(f32), axis=2).astype(x.dtype)
    x = x + merged @ p['w_out']
    return x, (fk, fv, flogf, dk, dvv, lru_hT, lru_buf_new, gS, gdn_buf_new)


def memory_kv(mem, p):
    m = rmsnorm(mem, p['norm_mem_src'])
    kv = (m @ p['w_mem_kv']).reshape(mem.shape[0], mem.shape[1], 2, MEM_HEADS, MEM_HD)
    return rmsnorm(kv[:, :, 0], p['mem_k_norm']), kv[:, :, 1]


def memory_sublayer(x, mk, mv, p):
    B, T, _ = x.shape
    h = rmsnorm(x, p['norm_mem'])
    q = rmsnorm((h @ p['w_mem_q']).reshape(B, T, MEM_HEADS, MEM_HD), p['mem_q_norm'])
    s = jnp.einsum('bthd,bmhd->bhtm', q.astype(jnp.float32), mk.astype(jnp.float32)) / math.sqrt(MEM_HD)
    a = jax.nn.softmax(s, axis=-1)
    o = jnp.einsum('bhtm,bmhd->bthd', a, mv.astype(jnp.float32)).reshape(B, T, MEM_WIDTH).astype(x.dtype)
    return x + o @ p['w_mem_o']


def mlp_sublayer(x, p):
    h = rmsnorm(x, p['norm_mlp'])
    u = jnp.square(jax.nn.relu(h @ p['w_mlp_up']))
    return x + u @ p['w_mlp_down']


def setup_inputs(seed: int = 0) -> dict:
    key = jax.random.key(seed)
    ks = iter(jax.random.split(key, 80))

    def nrm(shape, scale=1.0):
        return scale * jax.random.normal(next(ks), shape, jnp.float32)

    def gain(shape):
        return 1.0 + nrm(shape, 0.02)

    n_pages = PAST_LEN // PAGE_SIZE
    n_used = DEC_BATCH * n_pages
    pool = (n_used * 5 + 3) // 4
    L = DEPTH
    inp = {}
    inp['x_prompt'] = nrm((BATCH, SEQ, D_MODEL))
    inp['x_sample'] = nrm((DEC_BATCH, DEC_SEQ, D_MODEL))
    inp['cache_fox_k'] = nrm((L, pool, PAGE_SIZE, FOX_HEADS, FOX_HD))
    inp['cache_fox_v'] = nrm((L, pool, PAGE_SIZE, FOX_HEADS, FOX_HD))
    inp['cache_fox_logf'] = jax.nn.log_sigmoid(2.0 + nrm((L, pool, PAGE_SIZE, FOX_HEADS), 0.5))
    inp['cache_diff_k'] = nrm((L, pool, PAGE_SIZE, DIFF_HEADS, 2, DIFF_HD))
    inp['cache_diff_v'] = nrm((L, pool, PAGE_SIZE, DIFF_HEADS, DIFF_VD))
    inp['cache_mem_k'] = nrm((L, DEC_BATCH, MEM_TOKENS, MEM_HEADS, MEM_HD))
    inp['cache_mem_v'] = nrm((L, DEC_BATCH, MEM_TOKENS, MEM_HEADS, MEM_HD))
    inp['state_lru_h'] = nrm((L, DEC_BATCH, LRU_WIDTH), 0.5)
    inp['state_lru_conv'] = nrm((L, DEC_BATCH, CONV_W - 1, LRU_WIDTH))
    inp['state_gdn_S'] = nrm((L, DEC_BATCH, GDN_HEADS, GDN_HD, GDN_HD), 0.1)
    inp['state_gdn_conv'] = nrm((L, DEC_BATCH, CONV_W - 1, 3 * BR_WIDTH))
    inp['page_table'] = jax.random.permutation(next(ks), pool)[:n_used].reshape(DEC_BATCH, n_pages).astype(jnp.int32)
    inp['mem_prompt'] = nrm((BATCH, MEM_TOKENS, D_MODEL))
    inp['norm_mix'] = gain((L, D_MODEL))
    inp['w_in'] = nrm((L, D_MODEL, D_IN), D_MODEL ** -0.5)
    inp['fox_b_f'] = 2.0 + nrm((L, FOX_HEADS), 0.5)
    inp['fox_q_norm'] = gain((L, FOX_HD))
    inp['fox_k_norm'] = gain((L, FOX_HD))
    inp['lru_conv_w'] = nrm((L, CONV_W, LRU_WIDTH), CONV_W ** -0.5)
    inp['lru_conv_b'] = nrm((L, LRU_WIDTH), 0.1)
    inp['lru_w_a'] = nrm((L, LRU_BLOCKS, LRU_BD, LRU_BD), LRU_BD ** -0.5)
    inp['lru_b_a'] = nrm((L, LRU_WIDTH), 0.1)
    inp['lru_w_x'] = nrm((L, LRU_BLOCKS, LRU_BD, LRU_BD), LRU_BD ** -0.5)
    inp['lru_b_x'] = nrm((L, LRU_WIDTH), 0.1)
    a0 = jax.random.uniform(next(ks), (L, LRU_WIDTH), jnp.float32, 0.9, 0.999) ** (1.0 / LRU_C)
    inp['lru_lambda'] = jnp.log(a0) - jnp.log1p(-a0)
    inp['diff_q_norm'] = gain((L, DIFF_HD))
    inp['diff_k_norm'] = gain((L, DIFF_HD))
    inp['diff_lambda'] = nrm((L, 4, DIFF_HD), 0.1)
    inp['diff_sub_norm'] = gain((L, DIFF_VD))
    inp['gdn_conv_w'] = nrm((L, CONV_W, 3 * BR_WIDTH), CONV_W ** -0.5)
    inp['gdn_A_log'] = jnp.log(jax.random.uniform(next(ks), (L, GDN_HEADS), jnp.float32, 1.0, 16.0))
    dt = jnp.exp(jax.random.uniform(next(ks), (L, GDN_HEADS), jnp.float32, math.log(1e-3), math.log(1e-1)))
    inp['gdn_dt_bias'] = dt + jnp.log(-jnp.expm1(-dt))
    inp['gdn_out_norm'] = gain((L, GDN_HD))
    inp['w_branch'] = nrm((L, N_BRANCH, BR_WIDTH, D_MODEL), BR_WIDTH ** -0.5)
    inp['w_out'] = nrm((L, D_MODEL, D_MODEL), D_MODEL ** -0.5)
    inp['norm_mem'] = gain((L, D_MODEL))
    inp['norm_mem_src'] = gain((L, D_MODEL))
    inp['w_mem_q'] = nrm((L, D_MODEL, MEM_WIDTH), D_MODEL ** -0.5)
    inp['w_mem_kv'] = nrm((L, D_MODEL, 2 * MEM_WIDTH), D_MODEL ** -0.5)
    inp['mem_q_norm'] = gain((L, MEM_HD))
    inp['mem_k_norm'] = gain((L, MEM_HD))
    inp['w_mem_o'] = nrm((L, MEM_WIDTH, D_MODEL), MEM_WIDTH ** -0.5)
    inp['norm_mlp'] = gain((L, D_MODEL))
    inp['w_mlp_up'] = nrm((L, D_MODEL, D_FF), D_MODEL ** -0.5)
    inp['w_mlp_down'] = nrm((L, D_FF, D_MODEL), D_FF ** -0.5)
    return inp


def reference(x_prompt, x_sample, cache_fox_k, cache_fox_v, cache_fox_logf, cache_diff_k, cache_diff_v,
              cache_mem_k, cache_mem_v, state_lru_h, state_lru_conv, state_gdn_S, state_gdn_conv, page_table,
              mem_prompt, norm_mix, w_in, fox_b_f, fox_q_norm, fox_k_norm, lru_conv_w, lru_conv_b, lru_w_a,
              lru_b_a, lru_w_x, lru_b_x, lru_lambda, diff_q_norm, diff_k_norm, diff_lambda, diff_sub_norm,
              gdn_conv_w, gdn_A_log, gdn_dt_bias, gdn_out_norm, w_branch, w_out, norm_mem, norm_mem_src,
              w_mem_q, w_mem_kv, mem_q_norm, mem_k_norm, w_mem_o, norm_mlp, w_mlp_up, w_mlp_down):
    stacked = dict(norm_mix=norm_mix, w_in=w_in, fox_b_f=fox_b_f, fox_q_norm=fox_q_norm, fox_k_norm=fox_k_norm,
                   lru_conv_w=lru_conv_w, lru_conv_b=lru_conv_b, lru_w_a=lru_w_a, lru_b_a=lru_b_a,
                   lru_w_x=lru_w_x, lru_b_x=lru_b_x, lru_lambda=lru_lambda, diff_q_norm=diff_q_norm,
                   diff_k_norm=diff_k_norm, diff_lambda=diff_lambda, diff_sub_norm=diff_sub_norm,
                   gdn_conv_w=gdn_conv_w, gdn_A_log=gdn_A_log, gdn_dt_bias=gdn_dt_bias, gdn_out_norm=gdn_out_norm,
                   w_branch=w_branch, w_out=w_out, norm_mem=norm_mem, norm_mem_src=norm_mem_src, w_mem_q=w_mem_q,
                   w_mem_kv=w_mem_kv, mem_q_norm=mem_q_norm, mem_k_norm=mem_k_norm, w_mem_o=w_mem_o,
                   norm_mlp=norm_mlp, w_mlp_up=w_mlp_up, w_mlp_down=w_mlp_down)
    Bp, Tp, _ = x_prompt.shape
    Bs, Ts, _ = x_sample.shape
    past_len = page_table.shape[1] * cache_fox_k.shape[2]
    pos_p = jnp.arange(Tp)
    pos_s = past_len + jnp.arange(Ts)
    h0_p = jnp.zeros((Bp, LRU_WIDTH), jnp.float32)
    lbuf_p = jnp.zeros((Bp, CONV_W - 1, LRU_WIDTH), x_prompt.dtype)
    S0_p = jnp.zeros((Bp, GDN_HEADS, GDN_HD, GDN_HD), jnp.float32)
    gbuf_p = jnp.zeros((Bp, CONV_W - 1, 3 * BR_WIDTH), x_prompt.dtype)

    xp, xs = x_prompt, x_sample
    pst = [[] for _ in range(11)]
    sst = [[] for _ in range(9)]
    for l in range(DEPTH):
        p = {name: arr[l] for name, arr in stacked.items()}
        lam_init = 0.8 - 0.6 * math.exp(-0.3 * l)
        xp, st_p = mixer_sublayer(xp, pos_p, p, lam_init, None, None, h0_p, lbuf_p, S0_p, gbuf_p)
        mk, mv = memory_kv(mem_prompt, p)
        xp = memory_sublayer(xp, mk, mv, p)
        xp = mlp_sublayer(xp, p)
        for lst, val in zip(pst, (st_p[0], st_p[1], st_p[2], st_p[3], st_p[4], mk, mv,
                                  st_p[5], st_p[6], st_p[7], st_p[8])):
            lst.append(val)
        fox_past = (gather_pages(cache_fox_k[l], page_table), gather_pages(cache_fox_v[l], page_table),
                    gather_pages(cache_fox_logf[l], page_table))
        diff_past = (gather_pages(cache_diff_k[l], page_table), gather_pages(cache_diff_v[l], page_table))
        xs, st_s = mixer_sublayer(xs, pos_s, p, lam_init, fox_past, diff_past, state_lru_h[l], state_lru_conv[l],
                                  state_gdn_S[l], state_gdn_conv[l])
        xs = memory_sublayer(xs, cache_mem_k[l], cache_mem_v[l], p)
        xs = mlp_sublayer(xs, p)
        for lst, val in zip(sst, st_s):
            lst.append(val)

    (p_fox_k, p_fox_v, p_fox_logf, p_diff_k, p_diff_v, p_mem_k, p_mem_v,
     p_lru_h, p_lru_conv, p_gdn_S, p_gdn_conv) = [jnp.stack(v_, axis=0) for v_ in pst]
    (s_fox_k, s_fox_v, s_fox_logf, s_diff_k, s_diff_v,
     s_lru_h, s_lru_conv, s_gdn_S, s_gdn_conv) = [jnp.stack(v_, axis=0) for v_ in sst]
    return (xp, xs,
            p_fox_k, p_fox_v, p_fox_logf, p_diff_k, p_diff_v, p_mem_k, p_mem_v,
            p_lru_h, p_lru_conv, p_gdn_S, p_gdn_conv,
            s_fox_k, s_fox_v, s_fox_logf, s_diff_k, s_diff_v,
            s_lru_h, s_lru_conv, s_gdn_S, s_gdn_conv)
```

```python
import functools
import math

import jax
import jax.numpy as jnp
import numpy as np
from jax import lax
from jax.experimental import pallas as pl
from jax.experimental.pallas import tpu as pltpu

D_MODEL = 1024
DEPTH = 2
N_BRANCH = 4
BR_WIDTH = D_MODEL // 4
FOX_HEADS = 4
FOX_HD = BR_WIDTH // FOX_HEADS
LRU_WIDTH = BR_WIDTH
LRU_BLOCKS = 4
LRU_BD = LRU_WIDTH // LRU_BLOCKS
LRU_C = 8.0
CONV_W = 4
DIFF_HEADS = 4
DIFF_VD = BR_WIDTH // DIFF_HEADS
DIFF_HD = DIFF_VD // 2
GDN_HEADS = 4
GDN_HD = BR_WIDTH // GDN_HEADS
GDN_CHUNK = 64
MEM_HEADS = 4
MEM_HD = 64
MEM_WIDTH = MEM_HEADS * MEM_HD
D_FF = 4 * D_MODEL
ROPE_THETA = 10000.0
Q_BLOCK = 128
NORM_EPS = 1e-6
IN_SIZES = (3 * BR_WIDTH, FOX_HEADS, LRU_WIDTH, LRU_WIDTH, 3 * BR_WIDTH, 3 * BR_WIDTH, GDN_HEADS, GDN_HEADS,
            BR_WIDTH, N_BRANCH * D_MODEL)

VMEM_LIMIT_BYTES = 56 * 1024 * 1024


def _rmsnorm(x, g):
    xf = x.astype(jnp.float32)
    y = xf * lax.rsqrt(jnp.mean(xf * xf, axis=-1, keepdims=True) + NORM_EPS)
    return (y * g.astype(jnp.float32)).astype(x.dtype)


def _l2norm(x):
    return x * lax.rsqrt(jnp.sum(x * x, axis=-1, keepdims=True) + NORM_EPS)


def _rope(x, pos):
    d = x.shape[-1]
    inv = ROPE_THETA ** (-jnp.arange(0, d, 2, dtype=jnp.float32) / d)
    ang = pos.astype(jnp.float32)[:, None] * inv[None, :]
    ang = ang.reshape((1, ang.shape[0]) + (1,) * (x.ndim - 3) + (d // 2,))
    cos, sin = jnp.cos(ang), jnp.sin(ang)
    xf = x.astype(jnp.float32)
    x1, x2 = xf[..., : d // 2], xf[..., d // 2:]
    return jnp.concatenate([x1 * cos - x2 * sin, x2 * cos + x1 * sin], axis=-1).astype(x.dtype)


def _causal_conv(x, buf, w, b=None):
    xp = jnp.concatenate([buf.astype(x.dtype), x], axis=1)
    T = x.shape[1]
    y = xp[:, 0:T] * w[0]
    for i in range(1, CONV_W):
        y = y + xp[:, i:i + T] * w[i]
    if b is not None:
        y = y + b
    return y, xp[:, -(CONV_W - 1):]


def _gather_pages(cache, page_table):
    g = cache[page_table]
    return g.reshape((g.shape[0], -1) + g.shape[3:])


def _block_attention(q, k, v, q_pos, c_q=None, c_k=None):
    B, Tq, H, M, d = q.shape
    Tk = k.shape[1]
    k_pos = jnp.arange(Tk)
    qb = Q_BLOCK if Tq % Q_BLOCK == 0 else Tq
    nb = Tq // qb
    scale = 1.0 / math.sqrt(d)
    kf = k.astype(jnp.float32)
    vf = v.astype(jnp.float32)
    ck = None if c_k is None else jnp.swapaxes(c_k.astype(jnp.float32), 1, 2)

    def one_block(blk):
        q_blk, pos_blk, cq_blk = blk
        s = jnp.einsum('bqhmd,bkhmd->bhmqk', q_blk.astype(jnp.float32), kf) * scale
        if ck is not None:
            cq = jnp.swapaxes(cq_blk, 1, 2)
            s = s + (cq[:, :, None, :, None] - ck[:, :, None, None, :])
        s = jnp.where(k_pos[None, :] <= pos_blk[:, None], s, -jnp.inf)
        p_ = jax.nn.softmax(s, axis=-1)
        return jnp.einsum('bhmqk,bkhe->bqhme', p_, vf)

    q_blocks = jnp.swapaxes(q.reshape(B, nb, qb, H, M, d), 0, 1)
    pos_blocks = q_pos.reshape(nb, qb)
    cq_blocks = None if c_q is None else jnp.swapaxes(c_q.astype(jnp.float32).reshape(B, nb, qb, H), 0, 1)
    out = lax.map(one_block, (q_blocks, pos_blocks, cq_blocks))
    return jnp.swapaxes(out, 0, 1).reshape(B, Tq, H, M, v.shape[-1])


def _gated_delta_chunked(q, k, v, g, beta, S0):
    B, T, H, dk = k.shape
    dv = v.shape[-1]
    C = GDN_CHUNK if T % GDN_CHUNK == 0 else T
    N = T // C

    def chunk4(a):
        a = jnp.swapaxes(a, 1, 2)
        return jnp.moveaxis(a.reshape(B, H, N, C, a.shape[-1]), 2, 0)

    def chunk3(a):
        a = jnp.swapaxes(a, 1, 2)
        return jnp.moveaxis(a.reshape(B, H, N, C), 2, 0)

    tril = jnp.tril(jnp.ones((C, C), dtype=bool))
    strict = jnp.tril(jnp.ones((C, C), dtype=bool), -1)
    eye = jnp.eye(C, dtype=jnp.float32)

    def step(S, blk):
        qc, kc, vc, gc, bc = blk
        gcum = jnp.cumsum(gc, axis=-1)
        diff = gcum[..., :, None] - gcum[..., None, :]
        decay = jnp.where(tril, jnp.exp(jnp.where(tril, diff, 0.0)), 0.0)
        kk = jnp.einsum('bhid,bhjd->bhij', kc, kc)
        a_mat = jnp.where(strict, kk * decay * bc[..., :, None], 0.0) + eye
        rhs = jnp.concatenate([vc * bc[..., None], kc * (bc * jnp.exp(gcum))[..., None]], axis=-1)
        sol = lax.linalg.triangular_solve(a_mat, rhs, left_side=True, lower=True, unit_diagonal=True)
        u, w = sol[..., :dv], sol[..., dv:]
        v_new = u - jnp.einsum('bhck,bhkv->bhcv', w, S)
        qk = jnp.einsum('bhid,bhjd->bhij', qc, kc) * decay
        o = jnp.einsum('bhck,bhkv->bhcv', qc * jnp.exp(gcum)[..., None], S) + jnp.einsum('bhij,bhjv->bhiv', qk, v_new)
        g_last = gcum[..., -1:]
        S_new = S * jnp.exp(g_last)[..., None] + jnp.einsum('bhck,bhcv->bhkv', kc * jnp.exp(g_last - gcum)[..., None], v_new)
        return S_new, o

    S, o = lax.scan(step, S0.astype(jnp.float32), (chunk4(q), chunk4(k), chunk4(v), chunk3(g), chunk3(beta)))
    o = jnp.moveaxis(o, 0, 2).reshape(B, H, T, dv)
    return jnp.swapaxes(o, 1, 2), S


def _mixer_sublayer(x, pos, p, lam_init, fox_past, diff_past, lru_h0, lru_buf, gdn_S0, gdn_buf):
    B, T, _ = x.shape
    f32 = jnp.float32
    h = _rmsnorm(x, p['norm_mix'])
    z = h @ p['w_in']
    (fox_qkv, fox_f, lru_x, lru_g, diff_qkv, gdn_qkv, gdn_a, gdn_b, gdn_z, gates) = jnp.split(
        z, np.cumsum(IN_SIZES)[:-1].tolist(), axis=-1)

    r = fox_qkv.reshape(B, T, 3, FOX_HEADS, FOX_HD)
    fq = _rmsnorm(r[:, :, 0], p['fox_q_norm'])
    fk = _rmsnorm(r[:, :, 1], p['fox_k_norm'])
    fv = r[:, :, 2]
    flogf = jax.nn.log_sigmoid(fox_f.astype(f32) + p['fox_b_f'].astype(f32))
    if fox_past is None:
        ka, va, la = fk, fv, flogf
    else:
        ka = jnp.concatenate([fox_past[0].astype(fk.dtype), fk], axis=1)
        va = jnp.concatenate([fox_past[1].astype(fv.dtype), fv], axis=1)
        la = jnp.concatenate([fox_past[2].astype(f32), flogf], axis=1)
    c = jnp.cumsum(la, axis=1)
    fo = _block_attention(fq[:, :, :, None], ka[:, :, :, None], va, pos, c[:, -T:], c)[:, :, :, 0]
    fox_out = fo.reshape(B, T, BR_WIDTH).astype(x.dtype)

    xc, lru_buf_new = _causal_conv(lru_x, lru_buf, p['lru_conv_w'], p['lru_conv_b'])
    xf = xc.astype(f32)
    xblk = xf.reshape(B, T, LRU_BLOCKS, LRU_BD)
    rg = jax.nn.sigmoid(jnp.einsum('btni,nij->btnj', xblk, p['lru_w_a'].astype(f32)).reshape(B, T, LRU_WIDTH) + p['lru_b_a'])
    ig = jax.nn.sigmoid(jnp.einsum('btni,nij->btnj', xblk, p['lru_w_x'].astype(f32)).reshape(B, T, LRU_WIDTH) + p['lru_b_x'])
    log_a = -LRU_C * rg * jax.nn.softplus(-p['lru_lambda'].astype(f32))
    a = jnp.exp(log_a)
    bterm = jnp.sqrt(-jnp.expm1(2.0 * log_a)) * (ig * xf)
    bterm = bterm.at[:, 0].add(a[:, 0] * lru_h0.astype(f32))

    def lin_comb(lhs, rhs):
        return lhs[0] * rhs[0], rhs[0] * lhs[1] + rhs[1]

    _, hs = lax.associative_scan(lin_comb, (a, bterm), axis=1)
    lru_out = (hs * jax.nn.gelu(lru_g.astype(f32))).astype(x.dtype)
    lru_hT = hs[:, -1]

    r = diff_qkv.reshape(B, T, 3, DIFF_HEADS, DIFF_VD)
    dq = _rope(_rmsnorm(r[:, :, 0].reshape(B, T, DIFF_HEADS, 2, DIFF_HD), p['diff_q_norm']), pos)
    dk = _rope(_rmsnorm(r[:, :, 1].reshape(B, T, DIFF_HEADS, 2, DIFF_HD), p['diff_k_norm']), pos)
    dvv = r[:, :, 2]
    if diff_past is None:
        dka, dva = dk, dvv
    else:
        dka = jnp.concatenate([diff_past[0].astype(dk.dtype), dk], axis=1)
        dva = jnp.concatenate([diff_past[1].astype(dvv.dtype), dvv], axis=1)
    do = _block_attention(dq, dka, dva, pos)
    lp = p['diff_lambda'].astype(f32)
    lam = jnp.exp(jnp.sum(lp[0] * lp[1])) - jnp.exp(jnp.sum(lp[2] * lp[3])) + lam_init
    do = do[:, :, :, 0] - lam * do[:, :, :, 1]
    do = _rmsnorm(do, p['diff_sub_norm']) * (1.0 - lam_init)
    diff_out = do.reshape(B, T, BR_WIDTH).astype(x.dtype)

    gc, gdn_buf_new = _causal_conv(gdn_qkv, gdn_buf, p['gdn_conv_w'])
    gc = jax.nn.silu(gc.astype(f32)).reshape(B, T, 3, GDN_HEADS, GDN_HD)
    gq = _l2norm(gc[:, :, 0]) * (GDN_HD ** -0.5)
    gk = _l2norm(gc[:, :, 1])
    gvv = gc[:, :, 2]
    gbeta = jax.nn.sigmoid(gdn_b.astype(f32))
    gg = -jnp.exp(p['gdn_A_log'].astype(f32)) * jax.nn.softplus(gdn_a.astype(f32) + p['gdn_dt_bias'].astype(f32))
    go, gS = _gated_delta_chunked(gq, gk, gvv, gg, gbeta, gdn_S0)
    go = _rmsnorm(go, p['gdn_out_norm']) * jax.nn.silu(gdn_z.astype(f32).reshape(B, T, GDN_HEADS, GDN_HD))
    gdn_out = go.reshape(B, T, BR_WIDTH).astype(x.dtype)

    branches = jnp.stack([fox_out, lru_out, diff_out, gdn_out], axis=2)
    proj = jnp.einsum('btnc,ncd->btnd', branches, p['w_branch'])
    gate = jax.nn.sigmoid(gates.reshape(B, T, N_BRANCH, D_MODEL).astype(f32))
    merged = jnp.sum(gate * proj.astype(f32), axis=2).astype(x.dtype)
    x = x + merged @ p['w_out']
    return x, (fk, fv, flogf, dk, dvv, lru_hT, lru_buf_new, gS, gdn_buf_new)


def _memory_kv(mem, p):
    m = _rmsnorm(mem, p['norm_mem_src'])
    kv = (m @ p['w_mem_kv']).reshape(mem.shape[0], mem.shape[1], 2, MEM_HEADS, MEM_HD)
    return _rmsnorm(kv[:, :, 0], p['mem_k_norm']), kv[:, :, 1]


def _memory_sublayer(x, mk, mv, p):
    B, T, _ = x.shape
    h = _rmsnorm(x, p['norm_mem'])
    q = _rmsnorm((h @ p['w_mem_q']).reshape(B, T, MEM_HEADS, MEM_HD), p['mem_q_norm'])
    s = jnp.einsum('bthd,bmhd->bhtm', q.astype(jnp.float32), mk.astype(jnp.float32)) / math.sqrt(MEM_HD)
    a = jax.nn.softmax(s, axis=-1)
    o = jnp.einsum('bhtm,bmhd->bthd', a, mv.astype(jnp.float32)).reshape(B, T, MEM_WIDTH).astype(x.dtype)
    return x + o @ p['w_mem_o']


def _mlp_body(x_ref, g_ref, wu_ref, wd_ref, o_ref, *, ff_chunk):
    x = x_ref[...]
    ms = jnp.mean(x * x, axis=-1, keepdims=True)
    h = (x * lax.rsqrt(ms + NORM_EPS) * g_ref[...]).astype(jnp.bfloat16)
    acc = x
    for c in range(D_FF // ff_chunk):
        u = jnp.dot(h, wu_ref[:, c * ff_chunk:(c + 1) * ff_chunk], preferred_element_type=jnp.float32)
        u = jnp.square(jnp.maximum(u, 0.0)).astype(jnp.bfloat16)
        acc = acc + jnp.dot(u, wd_ref[c * ff_chunk:(c + 1) * ff_chunk, :], preferred_element_type=jnp.float32)
    o_ref[...] = acc


def _mlp_sublayer(x, g, w_up_bf16, w_down_bf16, *, row_tile, ff_chunk=1024):
    B, T, D = x.shape
    M = B * T
    x2 = x.reshape(M, D)
    resident = dict(pipeline_mode=pl.Buffered(1))
    out = pl.pallas_call(
        functools.partial(_mlp_body, ff_chunk=ff_chunk),
        grid=(M // row_tile,),
        in_specs=[
            pl.BlockSpec((row_tile, D), lambda i: (i, 0)),
            pl.BlockSpec((1, D), lambda i: (0, 0)),
            pl.BlockSpec((D, D_FF), lambda i: (0, 0), **resident),
            pl.BlockSpec((D_FF, D), lambda i: (0, 0), **resident),
        ],
        out_specs=pl.BlockSpec((row_tile, D), lambda i: (i, 0)),
        out_shape=jax.ShapeDtypeStruct((M, D), jnp.float32),
        compiler_params=pltpu.CompilerParams(dimension_semantics=("arbitrary",), vmem_limit_bytes=VMEM_LIMIT_BYTES),
        name="mlp",
    )(x2, g.reshape(1, D), w_up_bf16, w_down_bf16)
    return out.reshape(B, T, D)


def kernel(x_prompt, x_sample, cache_fox_k, cache_fox_v, cache_fox_logf, cache_diff_k, cache_diff_v, cache_mem_k, cache_mem_v, state_lru_h, state_lru_conv, state_gdn_S, state_gdn_conv, page_table, mem_prompt, norm_mix, w_in, fox_b_f, fox_q_norm, fox_k_norm, lru_conv_w, lru_conv_b, lru_w_a, lru_b_a, lru_w_x, lru_b_x, lru_lambda, diff_q_norm, diff_k_norm, diff_lambda, diff_sub_norm, gdn_conv_w, gdn_A_log, gdn_dt_bias, gdn_out_norm, w_branch, w_out, norm_mem, norm_mem_src, w_mem_q, w_mem_kv, mem_q_norm, mem_k_norm, w_mem_o, norm_mlp, w_mlp_up, w_mlp_down):
    stacked = dict(norm_mix=norm_mix, w_in=w_in, fox_b_f=fox_b_f, fox_q_norm=fox_q_norm, fox_k_norm=fox_k_norm,
                   lru_conv_w=lru_conv_w, lru_conv_b=lru_conv_b, lru_w_a=lru_w_a, lru_b_a=lru_b_a,
                   lru_w_x=lru_w_x, lru_b_x=lru_b_x, lru_lambda=lru_lambda, diff_q_norm=diff_q_norm,
                   diff_k_norm=diff_k_norm, diff_lambda=diff_lambda, diff_sub_norm=diff_sub_norm,
                   gdn_conv_w=gdn_conv_w, gdn_A_log=gdn_A_log, gdn_dt_bias=gdn_dt_bias, gdn_out_norm=gdn_out_norm,
                   w_branch=w_branch, w_out=w_out, norm_mem=norm_mem, norm_mem_src=norm_mem_src, w_mem_q=w_mem_q,
                   w_mem_kv=w_mem_kv, mem_q_norm=mem_q_norm, mem_k_norm=mem_k_norm, w_mem_o=w_mem_o,
                   norm_mlp=norm_mlp)
    Bp, Tp, _ = x_prompt.shape
    Bs, Ts, _ = x_sample.shape
    past_len = page_table.shape[1] * cache_fox_k.shape[2]
    pos_p = jnp.arange(Tp)
    pos_s = past_len + jnp.arange(Ts)
    h0_p = jnp.zeros((Bp, LRU_WIDTH), jnp.float32)
    lbuf_p = jnp.zeros((Bp, CONV_W - 1, LRU_WIDTH), x_prompt.dtype)
    S0_p = jnp.zeros((Bp, GDN_HEADS, GDN_HD, GDN_HD), jnp.float32)
    gbuf_p = jnp.zeros((Bp, CONV_W - 1, 3 * BR_WIDTH), x_prompt.dtype)
    w_up_bf16 = w_mlp_up.astype(jnp.bfloat16)
    w_down_bf16 = w_mlp_down.astype(jnp.bfloat16)

    xp, xs = x_prompt, x_sample
    pst = [[] for _ in range(11)]
    sst = [[] for _ in range(9)]
    for l in range(DEPTH):
        p = {name: arr[l] for name, arr in stacked.items()}
        lam_init = 0.8 - 0.6 * math.exp(-0.3 * l)
        xp, st_p = _mixer_sublayer(xp, pos_p, p, lam_init, None, None, h0_p, lbuf_p, S0_p, gbuf_p)
        mk, mv = _memory_kv(mem_prompt, p)
        xp = _memory_sublayer(xp, mk, mv, p)
        xp = _mlp_sublayer(xp, p['norm_mlp'], w_up_bf16[l], w_down_bf16[l], row_tile=512)
        for lst, val in zip(pst, (st_p[0], st_p[1], st_p[2], st_p[3], st_p[4], mk, mv,
                                  st_p[5], st_p[6], st_p[7], st_p[8])):
            lst.append(val)
        fox_past = (_gather_pages(cache_fox_k[l], page_table), _gather_pages(cache_fox_v[l], page_table),
                    _gather_pages(cache_fox_logf[l], page_table))
        diff_past = (_gather_pages(cache_diff_k[l], page_table), _gather_pages(cache_diff_v[l], page_table))
        xs, st_s = _mixer_sublayer(xs, pos_s, p, lam_init, fox_past, diff_past, state_lru_h[l], state_lru_conv[l],
                                   state_gdn_S[l], state_gdn_conv[l])
        xs = _memory_sublayer(xs, cache_mem_k[l], cache_mem_v[l], p)
        xs = _mlp_sublayer(xs, p['norm_mlp'], w_up_bf16[l], w_down_bf16[l], row_tile=128)
        for lst, val in zip(sst, st_s):
            lst.append(val)

    return tuple([xp, xs] + [jnp.stack(v_, axis=0) for v_ in pst] + [jnp.stack(v_, axis=0) for v_ in sst])
```

```python
import functools
import math

import jax
import jax.numpy as jnp
import numpy as np
from jax import lax
from jax.experimental import pallas as pl
from jax.experimental.pallas import tpu as pltpu

D_MODEL = 1024
DEPTH = 2
N_BRANCH = 4
BR_WIDTH = D_MODEL // 4
FOX_HEADS = 4
FOX_HD = BR_WIDTH // FOX_HEADS
LRU_WIDTH = BR_WIDTH
LRU_BLOCKS = 4
LRU_BD = LRU_WIDTH // LRU_BLOCKS
LRU_C = 8.0
CONV_W = 4
DIFF_HEADS = 4
DIFF_VD = BR_WIDTH // DIFF_HEADS
DIFF_HD = DIFF_VD // 2
GDN_HEADS = 4
GDN_HD = BR_WIDTH // GDN_HEADS
MEM_HEADS = 4
MEM_HD = 64
MEM_WIDTH = MEM_HEADS * MEM_HD
D_FF = 4 * D_MODEL
ROPE_THETA = 10000.0
Q_BLOCK = 128
NORM_EPS = 1e-6
HEADS = 4
HEAD_DIM = 64
SMALL_COLS = 128
MAIN_COLS = 12 * BR_WIDTH + SMALL_COLS

F32 = jnp.float32
BF16 = jnp.bfloat16
NEG_BIG = -1e30
LANES = 128
VMEM_LIMIT_BYTES = 56 * 1024 * 1024


def _params(*sem):
    return pltpu.CompilerParams(dimension_semantics=sem, vmem_limit_bytes=VMEM_LIMIT_BYTES)


def _resident(shape):
    nd = len(shape)
    return pl.BlockSpec(shape, lambda *_: (0,) * nd, pipeline_mode=pl.Buffered(1))


def _dot(a, b):
    return jnp.dot(a, b, preferred_element_type=F32)


def _dot_nt(a, b):
    return lax.dot_general(a, b, (((1,), (1,)), ((), ())), preferred_element_type=F32)


def _dot_tn(a, b):
    return lax.dot_general(a, b, (((0,), (0,)), ((), ())), preferred_element_type=F32)


def _split3(x):
    p0 = x.astype(BF16)
    r = x - p0.astype(F32)
    p1 = r.astype(BF16)
    p2 = (r - p1.astype(F32)).astype(BF16)
    return p0, p1, p2


def _dot_sel(x, sel_bf16):
    p0, p1, p2 = _split3(x)
    return _dot(p0, sel_bf16) + (_dot(p1, sel_bf16) + _dot(p2, sel_bf16))


def _sel_dot_nt(sel_bf16, x):
    p0, p1, p2 = _split3(x)
    return _dot_nt(sel_bf16, p0) + (_dot_nt(sel_bf16, p1) + _dot_nt(sel_bf16, p2))


def _sel_dot(sel_bf16, x):
    p0, p1, p2 = _split3(x)
    return _dot(sel_bf16, p0) + (_dot(sel_bf16, p1) + _dot(sel_bf16, p2))


def _dot_hi(a, b):
    ah = a.astype(BF16)
    al = (a - ah.astype(F32)).astype(BF16)
    bh = b.astype(BF16)
    bl = (b - bh.astype(F32)).astype(BF16)
    return _dot(ah, bh) + (_dot(al, bh) + _dot(ah, bl))


def _group_id(shape, axis, size):
    return lax.shift_right_logical(lax.broadcasted_iota(jnp.int32, shape, axis), int(math.log2(size)))


def _rms_scale(x):
    return lax.rsqrt(jnp.mean(x * x, axis=-1, keepdims=True) + NORM_EPS)


def _sigmoid(x):
    return 1.0 / (1.0 + jnp.exp(-x))


def _norm_matmul_body(x_ref, g_ref, *refs, n_out, col_chunk):
    w_refs, o_refs = refs[:n_out], refs[n_out:]
    x = x_ref[...]
    h = (x * _rms_scale(x) * g_ref[...]).astype(BF16)
    for w_ref, o_ref in zip(w_refs, o_refs):
        n = w_ref.shape[1]
        for c0 in range(0, n, col_chunk):
            c1 = min(n, c0 + col_chunk)
            o_ref[:, c0:c1] = _dot(h, w_ref[:, c0:c1])


def _norm_matmul(x2, g, ws, *, row_tile, name):
    M, D = x2.shape
    return pl.pallas_call(
        functools.partial(_norm_matmul_body, n_out=len(ws), col_chunk=1024),
        grid=(M // row_tile,),
        in_specs=[pl.BlockSpec((row_tile, D), lambda i: (i, 0)), _resident((1, D))] + [_resident(w.shape) for w in ws],
        out_specs=[pl.BlockSpec((row_tile, w.shape[1]), lambda i: (i, 0)) for w in ws],
        out_shape=[jax.ShapeDtypeStruct((M, w.shape[1]), F32) for w in ws],
        compiler_params=_params("arbitrary"),
        name=name,
    )(x2, g.reshape(1, D), *ws)


def _flash_body(*refs, n_maps, use_bias, scale, tile, sub_scale):
    if use_bias:
        q_ref, k_ref, v_ref, cc_ref, cr_ref, o_ref, m_ref, l_ref, acc_ref = refs
    else:
        q_ref, k_ref, v_ref, lam_ref, gsub_ref, o_ref, m_ref, l_ref, acc_ref = refs
    i = pl.program_id(1)
    j = pl.program_id(2)

    @pl.when(j == 0)
    def _init():
        m_ref[...] = jnp.full(m_ref.shape, NEG_BIG, F32)
        l_ref[...] = jnp.zeros(l_ref.shape, F32)
        acc_ref[...] = jnp.zeros(acc_ref.shape, F32)

    @pl.when(j <= i)
    def _step():
        row = i * tile + lax.broadcasted_iota(jnp.int32, (tile, tile), 0)
        col = j * tile + lax.broadcasted_iota(jnp.int32, (tile, tile), 1)
        keep = col <= row
        lane_map = _group_id((tile, HEAD_DIM), 1, HEAD_DIM // n_maps)
        for h in range(HEADS):
            kh = k_ref[0, h]
            vh = v_ref[0, h]
            qh = q_ref[0, h]
            for mp in range(n_maps):
                qm = qh if n_maps == 1 else jnp.where(lane_map == mp, qh, jnp.zeros_like(qh))
                s = _dot_nt(qm, kh) * scale
                if use_bias:
                    s = s + (cc_ref[0][:, h:h + 1] - cr_ref[0][h:h + 1, :])
                s = jnp.where(keep, s, NEG_BIG)
                idx = h * n_maps + mp
                m_prev = m_ref[idx]
                m_new = jnp.maximum(m_prev, jnp.max(s, axis=1, keepdims=True))
                alpha = jnp.exp(m_prev - m_new)
                p = jnp.exp(s - m_new)
                l_ref[idx] = alpha * l_ref[idx] + jnp.sum(p, axis=1, keepdims=True)
                acc_ref[idx] = alpha * acc_ref[idx] + _dot(p.astype(BF16), vh)
                m_ref[idx] = m_new

    @pl.when(j == i)
    def _finish():
        outs = []
        for h in range(HEADS):
            if n_maps == 1:
                outs.append(acc_ref[h] / l_ref[h])
            else:
                d = acc_ref[2 * h] / l_ref[2 * h] - lam_ref[...] * (acc_ref[2 * h + 1] / l_ref[2 * h + 1])
                outs.append(d * _rms_scale(d) * gsub_ref[...] * sub_scale)
        o_ref[...] = jnp.concatenate(outs, axis=-1).astype(o_ref.dtype)


def _flash_attention(q, k, v, *, tile, n_maps, scale, bias=None, lam=None, gsub=None, sub_scale=1.0, name):
    B, H, T, hd = q.shape
    nq = T // tile
    qspec = pl.BlockSpec((1, H, tile, hd), lambda b, i, j: (b, 0, i, 0))
    kspec = pl.BlockSpec((1, H, tile, hd), lambda b, i, j: (b, 0, jnp.minimum(i, j), 0))
    if bias is not None:
        c_col, c_row = bias
        extra = [c_col, c_row]
        extra_specs = [pl.BlockSpec((1, tile, H), lambda b, i, j: (b, i, 0)),
                       pl.BlockSpec((1, H, tile), lambda b, i, j: (b, 0, jnp.minimum(i, j)))]
    else:
        extra = [lam.reshape(1, 1), gsub.reshape(1, hd)]
        extra_specs = [_resident((1, 1)), _resident((1, hd))]
    return pl.pallas_call(
        functools.partial(_flash_body, n_maps=n_maps, use_bias=bias is not None, scale=scale, tile=tile,
                          sub_scale=sub_scale),
        grid=(B, nq, nq),
        in_specs=[qspec, kspec, kspec] + extra_specs,
        out_specs=pl.BlockSpec((tile, H * hd), lambda b, i, j: (b * nq + i, 0)),
        out_shape=jax.ShapeDtypeStruct((B * T, H * hd), BF16),
        scratch_shapes=[pltpu.VMEM((H * n_maps, tile, 1), F32), pltpu.VMEM((H * n_maps, tile, 1), F32),
                        pltpu.VMEM((H * n_maps, tile, hd), F32)],
        compiler_params=_params("arbitrary", "arbitrary", "arbitrary"),
        name=name,
    )(q, k, v, *extra)


def _gelu_tanh(x):
    return 0.5 * x * (1.0 + jnp.tanh(math.sqrt(2.0 / math.pi) * (x + 0.044715 * (x * x * x))))


def _lru_body(xc_ref, lg_ref, wa_ref, wx_ref, ba_ref, bx_ref, sp_ref, h0_ref, o_ref, ht_ref, carry_ref, *,
              rows, last_tile, last_row):
    t = pl.program_id(1)

    @pl.when(t == 0)
    def _init():
        carry_ref[...] = h0_ref[0]

    xc = xc_ref[...]
    xb = xc.astype(BF16)
    rg = _sigmoid(_dot(xb, wa_ref[...]) + ba_ref[...])
    ig = _sigmoid(_dot(xb, wx_ref[...]) + bx_ref[...])
    log_a = (-LRU_C) * rg * sp_ref[...]
    a = jnp.exp(log_a)
    b = jnp.sqrt(1.0 - jnp.exp(2.0 * log_a)) * (ig * xc)
    row = lax.broadcasted_iota(jnp.int32, a.shape, 0)
    d = 1
    while d < rows:
        a_prev = jnp.where(row >= d, pltpu.roll(a, d, 0), 1.0)
        b_prev = jnp.where(row >= d, pltpu.roll(b, d, 0), 0.0)
        b = a * b_prev + b
        a = a * a_prev
        d *= 2
    hs = b + a * carry_ref[...]
    carry_ref[...] = hs[rows - 1:rows]
    o_ref[...] = (hs * _gelu_tanh(lg_ref[...])).astype(o_ref.dtype)

    @pl.when(t == last_tile)
    def _state():
        ht_ref[0] = hs[last_row:last_row + 1]


def _lru(xc, lg, wa_bd, wx_bd, ba, bx, sp, h0, *, rows, t_valid, name):
    B, T, W = xc.shape
    nt = T // rows
    vec = lambda a: a.reshape(1, W)
    out, ht = pl.pallas_call(
        functools.partial(_lru_body, rows=rows, last_tile=(t_valid - 1) // rows, last_row=(t_valid - 1) % rows),
        grid=(B, nt),
        in_specs=[pl.BlockSpec((rows, W), lambda b, t: (b * nt + t, 0)),
                  pl.BlockSpec((rows, W), lambda b, t: (b * nt + t, 0)),
                  _resident((W, W)), _resident((W, W)), _resident((1, W)), _resident((1, W)), _resident((1, W)),
                  pl.BlockSpec((1, 1, W), lambda b, t: (b, 0, 0))],
        out_specs=[pl.BlockSpec((rows, W), lambda b, t: (b * nt + t, 0)),
                   pl.BlockSpec((1, 1, W), lambda b, t: (b, 0, 0))],
        out_shape=[jax.ShapeDtypeStruct((B * T, W), BF16), jax.ShapeDtypeStruct((B, 1, W), F32)],
        scratch_shapes=[pltpu.VMEM((1, W), F32)],
        compiler_params=_params("arbitrary", "arbitrary"),
        name=name,
    )(xc.reshape(B * T, W), lg.reshape(B * T, W), wa_bd, wx_bd, vec(ba), vec(bx), vec(sp), h0.reshape(B, 1, W))
    return out, ht.reshape(B, W)


def _gdn_chunk(q, k, v, gb, S, chunk):
    R = HEADS * chunk
    SD = HEADS * HEAD_DIM
    r0 = lax.broadcasted_iota(jnp.int32, (R, R), 0)
    r1 = lax.broadcasted_iota(jnp.int32, (R, R), 1)
    same = _group_id((R, R), 0, chunk) == _group_id((R, R), 1, chunk)
    low = same & (r0 >= r1)
    strict = same & (r0 > r1)
    upper_sel = jnp.where(same & (r0 <= r1), 1.0, 0.0).astype(BF16)
    block_sel = jnp.where(same, 1.0, 0.0).astype(BF16)
    eye = jnp.where(r0 == r1, 1.0, 0.0)

    g_rows = jnp.concatenate([gb, jnp.zeros((LANES - gb.shape[0], R), F32)], axis=0)
    cum = _dot_sel(g_rows, upper_sel)
    tot = _dot_sel(g_rows, block_sel)
    rid = lax.broadcasted_iota(jnp.int32, (LANES, R), 0)
    stacked = jnp.where(rid == 0, cum, jnp.where(rid == 1, g_rows, tot))
    cols = _sel_dot_nt(eye.astype(BF16), stacked)
    gcum_col, beta_col, gtot_col = cols[:, 0:1], cols[:, 1:2], cols[:, 2:3]
    gcum_row = cum[0:1, :]

    decay = jnp.where(low, jnp.exp(jnp.where(low, gcum_col - gcum_row, 0.0)), 0.0)
    kb = k.astype(BF16)
    kk = _dot_nt(kb, kb)
    qk = _dot_nt(q.astype(BF16), kb)
    nmat = jnp.where(strict, kk * decay * beta_col, 0.0)
    inv = eye - nmat
    pw = nmat
    width = 2
    while width < chunk:
        pw = _dot_hi(pw, pw)
        inv = inv + _dot_hi(inv, pw)
        width *= 2
    u = _dot_hi(inv, v * beta_col)
    w = _dot_hi(inv, k * (beta_col * jnp.exp(gcum_col)))

    head_match = _group_id((R, SD), 0, chunk) == _group_id((R, SD), 1, HEAD_DIM)

    def spread(x):
        return jnp.where(head_match, jnp.concatenate([x] * HEADS, axis=1), 0.0).astype(BF16)

    sb = S.astype(BF16)
    v_new = u - _dot(spread(w), sb)
    o = _dot(spread(q * jnp.exp(gcum_col)), sb) + _dot((qk * decay).astype(BF16), v_new.astype(BF16))
    kg = spread(k * jnp.exp(gtot_col - gcum_col))
    tok = jnp.bitwise_and(lax.broadcasted_iota(jnp.int32, (R, SD), 0), chunk - 1)
    first = jnp.where(head_match & (tok == 0), 1.0, 0.0).astype(BF16)
    c0, c1, c2 = _split3(cols)
    gtot_state = _dot_tn(first, c0) + (_dot_tn(first, c1) + _dot_tn(first, c2))
    s_new = S * jnp.exp(gtot_state[:, 2:3]) + _dot_tn(kg, v_new.astype(BF16))
    return o, s_new


def _gdn_body(q_ref, k_ref, v_ref, gb_ref, s0_ref, o_ref, sT_ref, s_ref, *, chunk, group):
    c = pl.program_id(1)

    @pl.when(c == 0)
    def _init():
        s_ref[...] = s0_ref[...]

    R = HEADS * chunk
    for g in range(group):
        o, s_new = _gdn_chunk(q_ref[g].reshape(R, HEAD_DIM), k_ref[g].reshape(R, HEAD_DIM),
                              v_ref[g].reshape(R, HEAD_DIM), gb_ref[g, 0], s_ref[g], chunk)
        o_ref[g] = o.reshape(HEADS, chunk, HEAD_DIM)
        s_ref[g] = s_new

    @pl.when(c == pl.num_programs(1) - 1)
    def _state():
        sT_ref[...] = s_ref[...]


def _gdn(q, k, v, gb, S0, *, chunk, group, name):
    B, H, T, hd = q.shape
    nc = T // chunk
    SD = H * hd
    tok = pl.BlockSpec((group, H, chunk, hd), lambda b, c: (b, 0, c, 0))
    st = pl.BlockSpec((group, SD, hd), lambda b, c: (b, 0, 0))
    return pl.pallas_call(
        functools.partial(_gdn_body, chunk=chunk, group=group),
        grid=(B // group, nc),
        in_specs=[tok, tok, tok, pl.BlockSpec((group, 1, 8, H * chunk), lambda b, c: (b, c, 0, 0)), st],
        out_specs=[tok, st],
        out_shape=[jax.ShapeDtypeStruct((B, H, T, hd), F32), jax.ShapeDtypeStruct((B, SD, hd), F32)],
        scratch_shapes=[pltpu.VMEM((group, SD, hd), F32)],
        compiler_params=_params("arbitrary", "arbitrary"),
        name=name,
    )(q, k, v, gb, S0)


def _merge_body(x_ref, b0_ref, b1_ref, b2_ref, b3_ref, gates_ref, wb_ref, wo_ref, o_ref):
    merged = None
    for n, b_ref in enumerate((b0_ref, b1_ref, b2_ref, b3_ref)):
        proj = _dot(b_ref[...], wb_ref[n])
        term = _sigmoid(gates_ref[:, n * D_MODEL:(n + 1) * D_MODEL]) * proj
        merged = term if merged is None else merged + term
    o_ref[...] = x_ref[...] + _dot(merged.astype(BF16), wo_ref[...])


def _merge(x2, branches, gates, wb, wo, *, row_tile, name):
    M, D = x2.shape
    row = lambda w: pl.BlockSpec((row_tile, w), lambda i: (i, 0))
    return pl.pallas_call(
        _merge_body,
        grid=(M // row_tile,),
        in_specs=[row(D)] + [row(BR_WIDTH)] * N_BRANCH + [row(N_BRANCH * D), _resident(wb.shape), _resident(wo.shape)],
        out_specs=row(D),
        out_shape=jax.ShapeDtypeStruct((M, D), F32),
        compiler_params=_params("arbitrary"),
        name=name,
    )(x2, *branches, gates, wb, wo)


def _mem_body(x_ref, g_ref, wq_ref, qg_ref, mk_ref, mv_ref, wo_ref, o_ref):
    x = x_ref[0]
    h = (x * _rms_scale(x) * g_ref[...]).astype(BF16)
    q = _dot(h, wq_ref[...])
    W = q.shape[1]
    head_sum = jnp.where(_group_id((W, W), 0, MEM_HD) == _group_id((W, W), 1, MEM_HD), 1.0, 0.0).astype(BF16)
    ms = _dot_sel(q * q, head_sum) * (1.0 / MEM_HD)
    qn = q * lax.rsqrt(ms + NORM_EPS) * qg_ref[...] * (1.0 / math.sqrt(MEM_HD))
    mk = mk_ref[0].astype(BF16)
    mv = mv_ref[0].astype(BF16)
    lane_head = _group_id(q.shape, 1, MEM_HD)
    o = jnp.zeros(q.shape, F32)
    for hh in range(MEM_HEADS):
        s = _dot_nt(jnp.where(lane_head == hh, qn, 0.0).astype(BF16), mk)
        p = jnp.exp(s - jnp.max(s, axis=1, keepdims=True))
        p = p / jnp.sum(p, axis=1, keepdims=True)
        o = o + jnp.where(lane_head == hh, _dot(p.astype(BF16), mv), 0.0)
    o_ref[0] = x + _dot(o.astype(BF16), wo_ref[...])


def _mem_sublayer(x, g, wq, qg, mk, mv, wo, *, row_tile, name):
    B, T, D = x.shape
    nt = T // row_tile
    kv = pl.BlockSpec((1,) + mk.shape[1:], lambda b, t: (b, 0, 0))
    return pl.pallas_call(
        _mem_body,
        grid=(B, nt),
        in_specs=[pl.BlockSpec((1, row_tile, D), lambda b, t: (b, t, 0)), _resident((1, D)), _resident(wq.shape),
                  _resident((1, MEM_WIDTH)), kv, kv, _resident(wo.shape)],
        out_specs=pl.BlockSpec((1, row_tile, D), lambda b, t: (b, t, 0)),
        out_shape=jax.ShapeDtypeStruct((B, T, D), F32),
        compiler_params=_params("arbitrary", "arbitrary"),
        name=name,
    )(x, g.reshape(1, D), wq, jnp.tile(qg, MEM_HEADS).reshape(1, MEM_WIDTH), mk, mv, wo)


def _mlp_body(x_ref, g_ref, wu_ref, wd_ref, o_ref, *, ff_chunk):
    x = x_ref[...]
    h = (x * _rms_scale(x) * g_ref[...]).astype(BF16)
    acc = x
    for c in range(D_FF // ff_chunk):
        u = _dot(h, wu_ref[:, c * ff_chunk:(c + 1) * ff_chunk])
        u = jnp.square(jnp.maximum(u, 0.0)).astype(BF16)
        acc = acc + _dot(u, wd_ref[c * ff_chunk:(c + 1) * ff_chunk, :])
    o_ref[...] = acc


def _mlp_sublayer(x2, g, w_up, w_down, *, row_tile, name):
    M, D = x2.shape
    return pl.pallas_call(
        functools.partial(_mlp_body, ff_chunk=1024),
        grid=(M // row_tile,),
        in_specs=[pl.BlockSpec((row_tile, D), lambda i: (i, 0)), _resident((1, D)), _resident(w_up.shape),
                  _resident(w_down.shape)],
        out_specs=pl.BlockSpec((row_tile, D), lambda i: (i, 0)),
        out_shape=jax.ShapeDtypeStruct((M, D), F32),
        compiler_params=_params("arbitrary"),
        name=name,
    )(x2, g.reshape(1, D), w_up, w_down)


def _rmsnorm(x, g):
    xf = x.astype(F32)
    return xf * lax.rsqrt(jnp.mean(xf * xf, axis=-1, keepdims=True) + NORM_EPS) * g.astype(F32)


def _l2norm(x):
    return x * lax.rsqrt(jnp.sum(x * x, axis=-1, keepdims=True) + NORM_EPS)


def _rope(x, pos):
    d = x.shape[-1]
    inv = ROPE_THETA ** (-jnp.arange(0, d, 2, dtype=F32) / d)
    ang = pos.astype(F32)[:, None] * inv[None, :]
    ang = ang.reshape((1, ang.shape[0]) + (1,) * (x.ndim - 3) + (d // 2,))
    cos, sin = jnp.cos(ang), jnp.sin(ang)
    x1, x2 = x[..., : d // 2], x[..., d // 2:]
    return jnp.concatenate([x1 * cos - x2 * sin, x2 * cos + x1 * sin], axis=-1)


def _causal_conv(x, buf, w, b=None):
    xp = jnp.concatenate([buf.astype(x.dtype), x], axis=1)
    T = x.shape[1]
    y = xp[:, 0:T] * w[0]
    for i in range(1, CONV_W):
        y = y + xp[:, i:i + T] * w[i]
    if b is not None:
        y = y + b
    return y, xp[:, -(CONV_W - 1):]


def _gather_pages(cache, page_table):
    g = cache[page_table]
    return g.reshape((g.shape[0], -1) + g.shape[3:])


def _block_attention(q, k, v, q_pos, c_q=None, c_k=None):
    B, Tq, H, M, d = q.shape
    Tk = k.shape[1]
    k_pos = jnp.arange(Tk)
    scale = 1.0 / math.sqrt(d)
    s = jnp.einsum('bqhmd,bkhmd->bhmqk', q, k) * scale
    if c_k is not None:
        cq = jnp.swapaxes(c_q, 1, 2)
        ck = jnp.swapaxes(c_k, 1, 2)
        s = s + (cq[:, :, None, :, None] - ck[:, :, None, None, :])
    s = jnp.where(k_pos[None, :] <= q_pos[:, None], s, -jnp.inf)
    p_ = jax.nn.softmax(s, axis=-1)
    return jnp.einsum('bhmqk,bkhe->bqhme', p_, v)


def _head_major(x):
    return jnp.swapaxes(x, 1, 2)


def _pad_time(x, t_pad, axis=1):
    pad = [(0, 0)] * x.ndim
    pad[axis] = (0, t_pad - x.shape[axis])
    return jnp.pad(x, pad)


def _pack_layer_weights(p):
    o = np.cumsum((0,) + (3 * BR_WIDTH, FOX_HEADS, LRU_WIDTH, LRU_WIDTH, 3 * BR_WIDTH, 3 * BR_WIDTH, GDN_HEADS,
                          GDN_HEADS, BR_WIDTH, N_BRANCH * D_MODEL)).tolist()
    w = p['w_in']
    seg = lambda i: w[:, o[i]:o[i + 1]]
    small = jnp.concatenate([seg(1), seg(6), seg(7), jnp.zeros((D_MODEL, SMALL_COLS - 3 * HEADS), w.dtype)], axis=1)
    w_main = jnp.concatenate([seg(0), seg(2), seg(3), seg(4), seg(5), seg(8), small], axis=1).astype(BF16)
    blockdiag = lambda wb: jax.scipy.linalg.block_diag(*[wb[i] for i in range(LRU_BLOCKS)]).astype(BF16)
    return dict(w_main=w_main, w_gates=seg(9).astype(BF16), lru_wa=blockdiag(p['lru_w_a']),
                lru_wx=blockdiag(p['lru_w_x']), w_branch=p['w_branch'].astype(BF16), w_out=p['w_out'].astype(BF16),
                w_mem_q=p['w_mem_q'].astype(BF16), w_mem_kv=p['w_mem_kv'].astype(BF16),
                w_mem_o=p['w_mem_o'].astype(BF16), w_mlp_up=p['w_mlp_up'].astype(BF16),
                w_mlp_down=p['w_mlp_down'].astype(BF16))


_C_FOX, _C_LRUX, _C_LRUG, _C_DIFF, _C_GDN, _C_GDNZ, _C_SMALL = 0, 768, 1024, 1280, 2048, 2816, 3072


def _mixer_sublayer(x, pos, p, pw, lam_init, fox_past, diff_past, lru_h0, lru_buf, gdn_S0, gdn_buf, *, row_tile,
                    attn_tile, tag):
    B, T, D = x.shape
    M = B * T
    x2 = x.reshape(M, D)
    zm, gates = _norm_matmul(x2, p['norm_mix'], [pw['w_main'], pw['w_gates']], row_tile=row_tile, name=tag + "in_proj")
    zm = zm.reshape(B, T, MAIN_COLS)
    seg = lambda c0, w: zm[..., c0:c0 + w]

    r = seg(_C_FOX, 3 * BR_WIDTH).reshape(B, T, 3, FOX_HEADS, FOX_HD)
    fq = _rmsnorm(r[:, :, 0], p['fox_q_norm'])
    fk = _rmsnorm(r[:, :, 1], p['fox_k_norm'])
    fv = r[:, :, 2]
    flogf = jax.nn.log_sigmoid(seg(_C_SMALL, FOX_HEADS) + p['fox_b_f'])
    if fox_past is None:
        c = jnp.cumsum(flogf, axis=1)
        fox_out = _flash_attention(_head_major(fq).astype(BF16), _head_major(fk).astype(BF16),
                                   _head_major(fv).astype(BF16), tile=attn_tile, n_maps=1,
                                   scale=1.0 / math.sqrt(FOX_HD), bias=(c, jnp.swapaxes(c, 1, 2)), name=tag + "fox_attn")
    else:
        ka = jnp.concatenate([fox_past[0], fk], axis=1)
        va = jnp.concatenate([fox_past[1], fv], axis=1)
        c = jnp.cumsum(jnp.concatenate([fox_past[2], flogf], axis=1), axis=1)
        fo = _block_attention(fq[:, :, :, None], ka[:, :, :, None], va, pos, c[:, -T:], c)[:, :, :, 0]
        fox_out = fo.reshape(M, BR_WIDTH).astype(BF16)

    xc, lru_buf_new = _causal_conv(seg(_C_LRUX, LRU_WIDTH), lru_buf, p['lru_conv_w'], p['lru_conv_b'])
    lru_rows = min(256, T) if T % 8 == 0 else 8
    t_pad = -(-T // lru_rows) * lru_rows
    lru_out, lru_hT = _lru(_pad_time(xc, t_pad), _pad_time(seg(_C_LRUG, LRU_WIDTH), t_pad), pw['lru_wa'], pw['lru_wx'],
                           p['lru_b_a'], p['lru_b_x'], jax.nn.softplus(-p['lru_lambda']), lru_h0,
                           rows=lru_rows, t_valid=T, name=tag + "lru")
    if t_pad != T:
        lru_out = lru_out.reshape(B, t_pad, LRU_WIDTH)[:, :T].reshape(M, LRU_WIDTH)

    r = seg(_C_DIFF, 3 * BR_WIDTH).reshape(B, T, 3, DIFF_HEADS, DIFF_VD)
    dq = _rope(_rmsnorm(r[:, :, 0].reshape(B, T, DIFF_HEADS, 2, DIFF_HD), p['diff_q_norm']), pos)
    dk = _rope(_rmsnorm(r[:, :, 1].reshape(B, T, DIFF_HEADS, 2, DIFF_HD), p['diff_k_norm']), pos)
    dvv = r[:, :, 2]
    lp = p['diff_lambda']
    lam = jnp.exp(jnp.sum(lp[0] * lp[1])) - jnp.exp(jnp.sum(lp[2] * lp[3])) + lam_init
    if diff_past is None:
        flat = lambda a: _head_major(a.reshape(B, T, DIFF_HEADS, DIFF_VD)).astype(BF16)
        diff_out = _flash_attention(flat(dq), flat(dk), flat(dvv), tile=attn_tile, n_maps=2,
                                    scale=1.0 / math.sqrt(DIFF_HD), lam=lam, gsub=p['diff_sub_norm'],
                                    sub_scale=1.0 - lam_init, name=tag + "diff_attn")
    else:
        dka = jnp.concatenate([diff_past[0], dk], axis=1)
        dva = jnp.concatenate([diff_past[1], dvv], axis=1)
        do = _block_attention(dq, dka, dva, pos)
        do = do[:, :, :, 0] - lam * do[:, :, :, 1]
        do = _rmsnorm(do, p['diff_sub_norm']) * (1.0 - lam_init)
        diff_out = do.reshape(M, BR_WIDTH).astype(BF16)

    gc, gdn_buf_new = _causal_conv(seg(_C_GDN, 3 * BR_WIDTH), gdn_buf, p['gdn_conv_w'])
    gc = jax.nn.silu(gc).reshape(B, T, 3, GDN_HEADS, GDN_HD)
    gq = _l2norm(gc[:, :, 0]) * (GDN_HD ** -0.5)
    gk = _l2norm(gc[:, :, 1])
    gvv = gc[:, :, 2]
    gbeta = jax.nn.sigmoid(seg(_C_SMALL + 2 * HEADS, GDN_HEADS))
    gg = -jnp.exp(p['gdn_A_log']) * jax.nn.softplus(seg(_C_SMALL + HEADS, GDN_HEADS) + p['gdn_dt_bias'])
    chunk = 64 if T % 64 == 0 else 32
    tg = -(-T // chunk) * chunk
    nc = tg // chunk
    rows_hc = lambda a: jnp.swapaxes(_pad_time(a, tg).reshape(B, nc, chunk, GDN_HEADS), 2, 3).reshape(B, nc, 1, GDN_HEADS * chunk)
    g_r, b_r = rows_hc(gg), rows_hc(gbeta)
    gb = jnp.concatenate([g_r, b_r, g_r, jnp.zeros((B, nc, 5, GDN_HEADS * chunk), F32)], axis=2)
    hm = lambda a: _head_major(_pad_time(a, tg))
    go, gS = _gdn(hm(gq), hm(gk), hm(gvv), gb, gdn_S0.reshape(B, GDN_HEADS * GDN_HD, GDN_HD), chunk=chunk,
                  group=2, name=tag + "gdn")
    go = _head_major(go)[:, :T]
    go = _rmsnorm(go, p['gdn_out_norm']) * jax.nn.silu(seg(_C_GDNZ, BR_WIDTH).reshape(B, T, GDN_HEADS, GDN_HD))
    gdn_out = go.reshape(M, BR_WIDTH).astype(BF16)

    x2 = _merge(x2, (fox_out, lru_out, diff_out, gdn_out), gates, pw['w_branch'], pw['w_out'], row_tile=row_tile,
                name=tag + "merge")
    state = (fk, fv, flogf, dk, dvv, lru_hT, lru_buf_new, gS.reshape(B, GDN_HEADS, GDN_HD, GDN_HD), gdn_buf_new)
    return x2.reshape(B, T, D), state


def _memory_kv(mem, p, pw):
    B, Tm, D = mem.shape
    kv, = _norm_matmul(mem.reshape(B * Tm, D), p['norm_mem_src'], [pw['w_mem_kv']], row_tile=256, name="mem_kv")
    kv = kv.reshape(B, Tm, 2, MEM_HEADS, MEM_HD)
    return _rmsnorm(kv[:, :, 0], p['mem_k_norm']), kv[:, :, 1]


def _memory_sublayer(x, mk, mv, p, pw, *, row_tile, name):
    B, T, D = x.shape
    t_pad = -(-T // row_tile) * row_tile
    flat = lambda a: a.reshape(a.shape[0], a.shape[1], MEM_WIDTH)
    out = _mem_sublayer(_pad_time(x, t_pad), p['norm_mem'], pw['w_mem_q'], p['mem_q_norm'], flat(mk), flat(mv),
                        pw['w_mem_o'], row_tile=row_tile, name=name)
    return out[:, :T]


def kernel(x_prompt, x_sample, cache_fox_k, cache_fox_v, cache_fox_logf, cache_diff_k, cache_diff_v, cache_mem_k, cache_mem_v, state_lru_h, state_lru_conv, state_gdn_S, state_gdn_conv, page_table, mem_prompt, norm_mix, w_in, fox_b_f, fox_q_norm, fox_k_norm, lru_conv_w, lru_conv_b, lru_w_a, lru_b_a, lru_w_x, lru_b_x, lru_lambda, diff_q_norm, diff_k_norm, diff_lambda, diff_sub_norm, gdn_conv_w, gdn_A_log, gdn_dt_bias, gdn_out_norm, w_branch, w_out, norm_mem, norm_mem_src, w_mem_q, w_mem_kv, mem_q_norm, mem_k_norm, w_mem_o, norm_mlp, w_mlp_up, w_mlp_down):
    stacked = dict(norm_mix=norm_mix, w_in=w_in, fox_b_f=fox_b_f, fox_q_norm=fox_q_norm, fox_k_norm=fox_k_norm,
                   lru_conv_w=lru_conv_w, lru_conv_b=lru_conv_b, lru_w_a=lru_w_a, lru_b_a=lru_b_a,
                   lru_w_x=lru_w_x, lru_b_x=lru_b_x, lru_lambda=lru_lambda, diff_q_norm=diff_q_norm,
                   diff_k_norm=diff_k_norm, diff_lambda=diff_lambda, diff_sub_norm=diff_sub_norm,
                   gdn_conv_w=gdn_conv_w, gdn_A_log=gdn_A_log, gdn_dt_bias=gdn_dt_bias, gdn_out_norm=gdn_out_norm,
                   w_branch=w_branch, w_out=w_out, norm_mem=norm_mem, norm_mem_src=norm_mem_src, w_mem_q=w_mem_q,
                   w_mem_kv=w_mem_kv, mem_q_norm=mem_q_norm, mem_k_norm=mem_k_norm, w_mem_o=w_mem_o,
                   norm_mlp=norm_mlp, w_mlp_up=w_mlp_up, w_mlp_down=w_mlp_down)
    Bp, Tp, D = x_prompt.shape
    Bs, Ts, _ = x_sample.shape
    past_len = page_table.shape[1] * cache_fox_k.shape[2]
    pos_p = jnp.arange(Tp)
    pos_s = past_len + jnp.arange(Ts)
    h0_p = jnp.zeros((Bp, LRU_WIDTH), F32)
    lbuf_p = jnp.zeros((Bp, CONV_W - 1, LRU_WIDTH), F32)
    S0_p = jnp.zeros((Bp, GDN_HEADS, GDN_HD, GDN_HD), F32)
    gbuf_p = jnp.zeros((Bp, CONV_W - 1, 3 * BR_WIDTH), F32)

    xp, xs = x_prompt, x_sample
    pst = [[] for _ in range(11)]
    sst = [[] for _ in range(9)]
    for l in range(DEPTH):
        p = {name: arr[l] for name, arr in stacked.items()}
        pw = _pack_layer_weights(p)
        lam_init = 0.8 - 0.6 * math.exp(-0.3 * l)
        xp, st_p = _mixer_sublayer(xp, pos_p, p, pw, lam_init, None, None, h0_p, lbuf_p, S0_p, gbuf_p,
                                   row_tile=256, attn_tile=512, tag="p_")
        mk, mv = _memory_kv(mem_prompt, p, pw)
        xp = _memory_sublayer(xp, mk, mv, p, pw, row_tile=512, name="p_mem")
        xp = _mlp_sublayer(xp.reshape(Bp * Tp, D), p['norm_mlp'], pw['w_mlp_up'], pw['w_mlp_down'], row_tile=512,
                           name="p_mlp").reshape(Bp, Tp, D)
        for lst, val in zip(pst, (st_p[0], st_p[1], st_p[2], st_p[3], st_p[4], mk, mv,
                                  st_p[5], st_p[6], st_p[7], st_p[8])):
            lst.append(val)
        fox_past = (_gather_pages(cache_fox_k[l], page_table), _gather_pages(cache_fox_v[l], page_table),
                    _gather_pages(cache_fox_logf[l], page_table))
        diff_past = (_gather_pages(cache_diff_k[l], page_table), _gather_pages(cache_diff_v[l], page_table))
        xs, st_s = _mixer_sublayer(xs, pos_s, p, pw, lam_init, fox_past, diff_past, state_lru_h[l], state_lru_conv[l],
                                   state_gdn_S[l], state_gdn_conv[l], row_tile=Bs * Ts, attn_tile=None, tag="s_")
        xs = _memory_sublayer(xs, cache_mem_k[l], cache_mem_v[l], p, pw, row_tile=8, name="s_mem")
        xs = _mlp_sublayer(xs.reshape(Bs * Ts, D), p['norm_mlp'], pw['w_mlp_up'], pw['w_mlp_down'], row_tile=Bs * Ts,
                           name="s_mlp").reshape(Bs, Ts, D)
        for lst, val in zip(sst, st_s):
            lst.append(val)

    return tuple([xp, xs] + [jnp.stack(v_, axis=0) for v_ in pst] + [jnp.stack(v_, axis=0) for v_ in sst])
```

```python
import functools
import math

import jax
import jax.numpy as jnp
import numpy as np
from jax import lax
from jax.experimental import pallas as pl
from jax.experimental.pallas import tpu as pltpu

D_MODEL = 1024
DEPTH = 2
N_BRANCH = 4
BR_WIDTH = D_MODEL // 4
FOX_HEADS = 4
FOX_HD = BR_WIDTH // FOX_HEADS
LRU_WIDTH = BR_WIDTH
LRU_BLOCKS = 4
LRU_BD = LRU_WIDTH // LRU_BLOCKS
LRU_C = 8.0
CONV_W = 4
DIFF_HEADS = 4
DIFF_VD = BR_WIDTH // DIFF_HEADS
DIFF_HD = DIFF_VD // 2
GDN_HEADS = 4
GDN_HD = BR_WIDTH // GDN_HEADS
MEM_HEADS = 4
MEM_HD = 64
MEM_WIDTH = MEM_HEADS * MEM_HD
D_FF = 4 * D_MODEL
ROPE_THETA = 10000.0
Q_BLOCK = 128
NORM_EPS = 1e-6
HEADS = 4
HEAD_DIM = 64
SMALL_COLS = 128
MAIN_COLS = 12 * BR_WIDTH + SMALL_COLS

F32 = jnp.float32
BF16 = jnp.bfloat16
NEG_BIG = -1e30
LANES = 128
VMEM_LIMIT_BYTES = 56 * 1024 * 1024


def _params(*sem):
    return pltpu.CompilerParams(dimension_semantics=sem, vmem_limit_bytes=VMEM_LIMIT_BYTES)


def _resident(shape):
    nd = len(shape)
    return pl.BlockSpec(shape, lambda *_: (0,) * nd, pipeline_mode=pl.Buffered(1))


def _dot(a, b):
    return jnp.dot(a, b, preferred_element_type=F32)


def _dot_nt(a, b):
    return lax.dot_general(a, b, (((1,), (1,)), ((), ())), preferred_element_type=F32)


def _dot_tn(a, b):
    return lax.dot_general(a, b, (((0,), (0,)), ((), ())), preferred_element_type=F32)


def _split3(x):
    p0 = x.astype(BF16)
    r = x - p0.astype(F32)
    p1 = r.astype(BF16)
    p2 = (r - p1.astype(F32)).astype(BF16)
    return p0, p1, p2


def _dot_sel(x, sel_bf16):
    p0, p1, p2 = _split3(x)
    return _dot(p0, sel_bf16) + (_dot(p1, sel_bf16) + _dot(p2, sel_bf16))


def _sel_dot_nt(sel_bf16, x):
    p0, p1, p2 = _split3(x)
    return _dot_nt(sel_bf16, p0) + (_dot_nt(sel_bf16, p1) + _dot_nt(sel_bf16, p2))


def _sel_dot(sel_bf16, x):
    p0, p1, p2 = _split3(x)
    return _dot(sel_bf16, p0) + (_dot(sel_bf16, p1) + _dot(sel_bf16, p2))


def _dot_hi(a, b):
    ah = a.astype(BF16)
    al = (a - ah.astype(F32)).astype(BF16)
    bh = b.astype(BF16)
    bl = (b - bh.astype(F32)).astype(BF16)
    return _dot(ah, bh) + (_dot(al, bh) + _dot(ah, bl))


def _group_id(shape, axis, size):
    return lax.shift_right_logical(lax.broadcasted_iota(jnp.int32, shape, axis), int(math.log2(size)))


def _rms_scale(x):
    return lax.rsqrt(jnp.mean(x * x, axis=-1, keepdims=True) + NORM_EPS)


def _sigmoid(x):
    return 1.0 / (1.0 + jnp.exp(-x))


def _norm_matmul_body(x_ref, g_ref, *refs, n_out, col_chunk):
    w_refs, o_refs = refs[:n_out], refs[n_out:]
    x = x_ref[...]
    h = (x * _rms_scale(x) * g_ref[...]).astype(BF16)
    for w_ref, o_ref in zip(w_refs, o_refs):
        n = w_ref.shape[1]
        for c0 in range(0, n, col_chunk):
            c1 = min(n, c0 + col_chunk)
            o_ref[:, c0:c1] = _dot(h, w_ref[:, c0:c1])


def _norm_matmul(x2, g, ws, *, row_tile, name):
    M, D = x2.shape
    return pl.pallas_call(
        functools.partial(_norm_matmul_body, n_out=len(ws), col_chunk=1024),
        grid=(M // row_tile,),
        in_specs=[pl.BlockSpec((row_tile, D), lambda i: (i, 0)), _resident((1, D))] + [_resident(w.shape) for w in ws],
        out_specs=[pl.BlockSpec((row_tile, w.shape[1]), lambda i: (i, 0)) for w in ws],
        out_shape=[jax.ShapeDtypeStruct((M, w.shape[1]), F32) for w in ws],
        compiler_params=_params("arbitrary"),
        name=name,
    )(x2, g.reshape(1, D), *ws)


def _flash_body(*refs, n_maps, use_bias, scale, tile, sub_scale):
    if use_bias:
        q_ref, k_ref, v_ref, cc_ref, cr_ref, o_ref, m_ref, l_ref, acc_ref = refs
    else:
        q_ref, k_ref, v_ref, lam_ref, gsub_ref, o_ref, m_ref, l_ref, acc_ref = refs
    i = pl.program_id(1)
    j = pl.program_id(2)

    @pl.when(j == 0)
    def _init():
        m_ref[...] = jnp.full(m_ref.shape, NEG_BIG, F32)
        l_ref[...] = jnp.zeros(l_ref.shape, F32)
        acc_ref[...] = jnp.zeros(acc_ref.shape, F32)

    @pl.when(j <= i)
    def _step():
        row = i * tile + lax.broadcasted_iota(jnp.int32, (tile, tile), 0)
        col = j * tile + lax.broadcasted_iota(jnp.int32, (tile, tile), 1)
        keep = col <= row
        lane_map = _group_id((tile, HEAD_DIM), 1, HEAD_DIM // n_maps)
        for h in range(HEADS):
            kh = k_ref[0, h]
            vh = v_ref[0, h]
            qh = q_ref[0, h]
            for mp in range(n_maps):
                qm = qh if n_maps == 1 else jnp.where(lane_map == mp, qh, jnp.zeros_like(qh))
                s = _dot_nt(qm, kh) * scale
                if use_bias:
                    s = s + (cc_ref[0][:, h:h + 1] - cr_ref[0][h:h + 1, :])
                s = jnp.where(keep, s, NEG_BIG)
                idx = h * n_maps + mp
                m_prev = m_ref[idx]
                m_new = jnp.maximum(m_prev, jnp.max(s, axis=1, keepdims=True))
                alpha = jnp.exp(m_prev - m_new)
                p = jnp.exp(s - m_new)
                l_ref[idx] = alpha * l_ref[idx] + jnp.sum(p, axis=1, keepdims=True)
                acc_ref[idx] = alpha * acc_ref[idx] + _dot(p.astype(BF16), vh)
                m_ref[idx] = m_new

    @pl.when(j == i)
    def _finish():
        outs = []
        for h in range(HEADS):
            if n_maps == 1:
                outs.append(acc_ref[h] / l_ref[h])
            else:
                d = acc_ref[2 * h] / l_ref[2 * h] - lam_ref[...] * (acc_ref[2 * h + 1] / l_ref[2 * h + 1])
                outs.append(d * _rms_scale(d) * gsub_ref[...] * sub_scale)
        o_ref[...] = jnp.concatenate(outs, axis=-1).astype(o_ref.dtype)


def _flash_attention(q, k, v, *, tile, n_maps, scale, bias=None, lam=None, gsub=None, sub_scale=1.0, name):
    B, H, T, hd = q.shape
    nq = T // tile
    qspec = pl.BlockSpec((1, H, tile, hd), lambda b, i, j: (b, 0, i, 0))
    kspec = pl.BlockSpec((1, H, tile, hd), lambda b, i, j: (b, 0, jnp.minimum(i, j), 0))
    if bias is not None:
        c_col, c_row = bias
        extra = [c_col, c_row]
        extra_specs = [pl.BlockSpec((1, tile, H), lambda b, i, j: (b, i, 0)),
                       pl.BlockSpec((1, H, tile), lambda b, i, j: (b, 0, jnp.minimum(i, j)))]
    else:
        extra = [lam.reshape(1, 1), gsub.reshape(1, hd)]
        extra_specs = [_resident((1, 1)), _resident((1, hd))]
    return pl.pallas_call(
        functools.partial(_flash_body, n_maps=n_maps, use_bias=bias is not None, scale=scale, tile=tile,
                          sub_scale=sub_scale),
        grid=(B, nq, nq),
        in_specs=[qspec, kspec, kspec] + extra_specs,
        out_specs=pl.BlockSpec((tile, H * hd), lambda b, i, j: (b * nq + i, 0)),
        out_shape=jax.ShapeDtypeStruct((B * T, H * hd), BF16),
        scratch_shapes=[pltpu.VMEM((H * n_maps, tile, 1), F32), pltpu.VMEM((H * n_maps, tile, 1), F32),
                        pltpu.VMEM((H * n_maps, tile, hd), F32)],
        compiler_params=_params("arbitrary", "arbitrary", "arbitrary"),
        name=name,
    )(q, k, v, *extra)


Q_ROWS = 8


def _paged_body(pt_ref, *refs, n_pages, n_maps, use_bias, scale, sub_scale):
    del pt_ref
    refs = list(refs)
    q_ref = refs.pop(0)
    lam_ref, gsub_ref = (None, None) if use_bias else (refs.pop(0), refs.pop(0))
    knew_ref, vnew_ref = refs.pop(0), refs.pop(0)
    lfnew_ref = refs.pop(0) if use_bias else None
    k_refs, refs = refs[:n_pages], refs[n_pages:]
    v_refs, refs = refs[:n_pages], refs[n_pages:]
    lf_refs, refs = (refs[:n_pages], refs[n_pages:]) if use_bias else (None, refs)
    o_ref, m_ref, l_ref, acc_ref = refs[:4]
    carry_ref = refs[4] if use_bias else None
    g = pl.program_id(1)
    R = HEADS * n_maps * Q_ROWS
    C = HEADS * HEAD_DIM

    @pl.when(g == 0)
    def _init():
        m_ref[...] = jnp.full(m_ref.shape, NEG_BIG, F32)
        l_ref[...] = jnp.zeros(l_ref.shape, F32)
        acc_ref[...] = jnp.zeros(acc_ref.shape, F32)
        if use_bias:
            carry_ref[...] = jnp.zeros(carry_ref.shape, F32)

    q = q_ref[0]

    def head_rows(x):
        return jnp.concatenate([jnp.broadcast_to(x[h:h + 1], (Q_ROWS, x.shape[1])) for h in range(HEADS)], axis=0)

    def attend(kts, vts, lfs, causal_new):
        kt = jnp.concatenate(kts, axis=1).astype(BF16)
        s = _dot(q, kt) * scale
        if use_bias:
            upper = jnp.where(lax.broadcasted_iota(jnp.int32, (LANES, LANES), 0)
                              <= lax.broadcasted_iota(jnp.int32, (LANES, LANES), 1), 1.0, 0.0).astype(BF16)
            carry = carry_ref[...]
            within = _dot_sel(jnp.concatenate([head_rows(lf) for lf in lfs], axis=0), upper)
            parts = []
            for i in range(len(lfs)):
                c = within[i * HEADS * Q_ROWS:(i + 1) * HEADS * Q_ROWS] + carry
                carry = jnp.broadcast_to(c[:, LANES - 1:LANES], c.shape)
                parts.append(c)
            carry_ref[...] = carry
            s = s - jnp.concatenate(parts, axis=1)
        if causal_new:
            t = jnp.bitwise_and(lax.broadcasted_iota(jnp.int32, s.shape, 0), Q_ROWS - 1)
            s = jnp.where(lax.broadcasted_iota(jnp.int32, s.shape, 1) <= t, s, NEG_BIG)
        m_prev = m_ref[...]
        m_new = jnp.maximum(m_prev, jnp.max(s, axis=1, keepdims=True))
        alpha = jnp.exp(m_prev - m_new)
        p = jnp.exp(s - m_new)
        l_ref[...] = alpha * l_ref[...] + jnp.sum(p, axis=1, keepdims=True)
        vt = jnp.concatenate(vts, axis=1).astype(BF16)
        acc_ref[...] = alpha * acc_ref[...] + _dot_nt(p.astype(BF16), vt)
        m_ref[...] = m_new

    attend([r[0, 0] for r in k_refs], [r[0, 0] for r in v_refs], [r[0, 0] for r in lf_refs] if use_bias else None, False)

    @pl.when(g == pl.num_programs(1) - 1)
    def _finish():
        attend([knew_ref[0]], [vnew_ref[0]], [lfnew_ref[0]] if use_bias else None, True)
        full = acc_ref[...] / l_ref[...]
        lane_head = _group_id((Q_ROWS, C), 1, HEAD_DIM)
        outs = []
        for mp in range(n_maps):
            o = jnp.zeros((Q_ROWS, C), F32)
            for h in range(HEADS):
                r0 = (h * n_maps + mp) * Q_ROWS
                o = o + jnp.where(lane_head == h, full[r0:r0 + Q_ROWS], 0.0)
            outs.append(o)
        if n_maps == 1:
            res = outs[0]
        else:
            d = outs[0] - lam_ref[...] * outs[1]
            head_sum = jnp.where(_group_id((C, C), 0, HEAD_DIM) == _group_id((C, C), 1, HEAD_DIM), 1.0, 0.0).astype(BF16)
            ms = _dot_sel(d * d, head_sum) * (1.0 / HEAD_DIM)
            res = d * lax.rsqrt(ms + NORM_EPS) * gsub_ref[...] * sub_scale
        o_ref[0] = res.astype(o_ref.dtype)


def _paged_attention(q_rows, k_cache_t, v_cache_t, knew_t, vnew_t, page_table, layer, *, pages_per_step, n_maps, scale,
                     lf_cache_t=None, lfnew_t=None, lam=None, gsub=None, sub_scale=1.0, name):
    B, R, C = q_rows.shape
    n_pages_total = page_table.shape[1]
    P = pages_per_step
    use_bias = lf_cache_t is not None
    per_seq = lambda shape: pl.BlockSpec((1,) + shape, lambda b, g, pt: (b,) + (0,) * len(shape))
    const = lambda shape: pl.BlockSpec(shape, lambda b, g, pt: (0,) * len(shape))

    def page_spec(i, tail):
        return pl.BlockSpec((1, 1) + tail, lambda b, g, pt: (layer, pt[b, g * P + i]) + (0,) * len(tail))

    operands = [q_rows]
    in_specs = [per_seq((R, C))]
    if not use_bias:
        operands += [lam.reshape(1, 1), jnp.tile(gsub, HEADS).reshape(1, C)]
        in_specs += [const((1, 1)), const((1, C))]
    operands += [knew_t, vnew_t]
    in_specs += [per_seq((C, LANES)), per_seq((C, LANES))]
    if use_bias:
        operands.append(lfnew_t)
        in_specs.append(per_seq((HEADS, LANES)))
    kv_tail = (C, LANES)
    operands += [k_cache_t] * P + [v_cache_t] * P
    in_specs += [page_spec(i, kv_tail) for i in range(P)] + [page_spec(i, kv_tail) for i in range(P)]
    scratch = [pltpu.VMEM((R, 1), F32), pltpu.VMEM((R, 1), F32), pltpu.VMEM((R, C), F32)]
    if use_bias:
        operands += [lf_cache_t] * P
        in_specs += [page_spec(i, (HEADS, LANES)) for i in range(P)]
        scratch.append(pltpu.VMEM((HEADS * Q_ROWS, LANES), F32))
    return pl.pallas_call(
        functools.partial(_paged_body, n_pages=P, n_maps=n_maps, use_bias=use_bias, scale=scale, sub_scale=sub_scale),
        grid_spec=pltpu.PrefetchScalarGridSpec(
            num_scalar_prefetch=1, grid=(B, n_pages_total // P), in_specs=in_specs,
            out_specs=pl.BlockSpec((1, Q_ROWS, C), lambda b, g, pt: (b, 0, 0)), scratch_shapes=scratch),
        out_shape=jax.ShapeDtypeStruct((B, Q_ROWS, C), F32),
        compiler_params=_params("arbitrary", "arbitrary"),
        name=name,
    )(page_table, *operands)


def _query_rows(q, n_maps):
    B, T, H, M, d = q.shape
    eye = jnp.eye(H * M, dtype=q.dtype).reshape(H, M, H, M)
    rows = jnp.einsum('bthmd,hmgn->bhmtgnd', q, eye)
    rows = jnp.pad(rows, ((0, 0), (0, 0), (0, 0), (0, Q_ROWS - T), (0, 0), (0, 0), (0, 0)))
    return rows.reshape(B, H * M * Q_ROWS, H * M * d).astype(BF16)


def _new_page(x):
    return jnp.pad(jnp.swapaxes(x, 1, 2), ((0, 0), (0, 0), (0, LANES - x.shape[1])))


def _gelu_tanh(x):
    return 0.5 * x * (1.0 + jnp.tanh(math.sqrt(2.0 / math.pi) * (x + 0.044715 * (x * x * x))))


def _lru_body(xc_ref, lg_ref, wa_ref, wx_ref, ba_ref, bx_ref, sp_ref, h0_ref, o_ref, ht_ref, carry_ref, *,
              rows, last_tile, last_row):
    t = pl.program_id(1)

    @pl.when(t == 0)
    def _init():
        carry_ref[...] = h0_ref[0]

    xc = xc_ref[...]
    xb = xc.astype(BF16)
    rg = _sigmoid(_dot(xb, wa_ref[...]) + ba_ref[...])
    ig = _sigmoid(_dot(xb, wx_ref[...]) + bx_ref[...])
    log_a = (-LRU_C) * rg * sp_ref[...]
    a = jnp.exp(log_a)
    b = jnp.sqrt(1.0 - jnp.exp(2.0 * log_a)) * (ig * xc)
    row = lax.broadcasted_iota(jnp.int32, a.shape, 0)
    d = 1
    while d < rows:
        a_prev = jnp.where(row >= d, pltpu.roll(a, d, 0), 1.0)
        b_prev = jnp.where(row >= d, pltpu.roll(b, d, 0), 0.0)
        b = a * b_prev + b
        a = a * a_prev
        d *= 2
    hs = b + a * carry_ref[...]
    carry_ref[...] = hs[rows - 1:rows]
    o_ref[...] = (hs * _gelu_tanh(lg_ref[...])).astype(o_ref.dtype)

    @pl.when(t == last_tile)
    def _state():
        ht_ref[0] = hs[last_row:last_row + 1]


def _lru(xc, lg, wa_bd, wx_bd, ba, bx, sp, h0, *, rows, t_valid, name):
    B, T, W = xc.shape
    nt = T // rows
    vec = lambda a: a.reshape(1, W)
    out, ht = pl.pallas_call(
        functools.partial(_lru_body, rows=rows, last_tile=(t_valid - 1) // rows, last_row=(t_valid - 1) % rows),
        grid=(B, nt),
        in_specs=[pl.BlockSpec((rows, W), lambda b, t: (b * nt + t, 0)),
                  pl.BlockSpec((rows, W), lambda b, t: (b * nt + t, 0)),
                  _resident((W, W)), _resident((W, W)), _resident((1, W)), _resident((1, W)), _resident((1, W)),
                  pl.BlockSpec((1, 1, W), lambda b, t: (b, 0, 0))],
        out_specs=[pl.BlockSpec((rows, W), lambda b, t: (b * nt + t, 0)),
                   pl.BlockSpec((1, 1, W), lambda b, t: (b, 0, 0))],
        out_shape=[jax.ShapeDtypeStruct((B * T, W), BF16), jax.ShapeDtypeStruct((B, 1, W), F32)],
        scratch_shapes=[pltpu.VMEM((1, W), F32)],
        compiler_params=_params("arbitrary", "arbitrary"),
        name=name,
    )(xc.reshape(B * T, W), lg.reshape(B * T, W), wa_bd, wx_bd, vec(ba), vec(bx), vec(sp), h0.reshape(B, 1, W))
    return out, ht.reshape(B, W)


def _gdn_chunk(q, k, v, gb, S, chunk):
    R = HEADS * chunk
    SD = HEADS * HEAD_DIM
    r0 = lax.broadcasted_iota(jnp.int32, (R, R), 0)
    r1 = lax.broadcasted_iota(jnp.int32, (R, R), 1)
    same = _group_id((R, R), 0, chunk) == _group_id((R, R), 1, chunk)
    low = same & (r0 >= r1)
    strict = same & (r0 > r1)
    upper_sel = jnp.where(same & (r0 <= r1), 1.0, 0.0).astype(BF16)
    block_sel = jnp.where(same, 1.0, 0.0).astype(BF16)
    eye = jnp.where(r0 == r1, 1.0, 0.0)

    g_rows = jnp.concatenate([gb, jnp.zeros((LANES - gb.shape[0], R), F32)], axis=0)
    cum = _dot_sel(g_rows, upper_sel)
    tot = _dot_sel(g_rows, block_sel)
    rid = lax.broadcasted_iota(jnp.int32, (LANES, R), 0)
    stacked = jnp.where(rid == 0, cum, jnp.where(rid == 1, g_rows, tot))
    cols = _sel_dot_nt(eye.astype(BF16), stacked)
    gcum_col, beta_col, gtot_col = cols[:, 0:1], cols[:, 1:2], cols[:, 2:3]
    gcum_row = cum[0:1, :]

    decay = jnp.where(low, jnp.exp(jnp.where(low, gcum_col - gcum_row, 0.0)), 0.0)
    kb = k.astype(BF16)
    kk = _dot_nt(kb, kb)
    qk = _dot_nt(q.astype(BF16), kb)
    nmat = jnp.where(strict, kk * decay * beta_col, 0.0)
    inv = eye - nmat
    pw = nmat
    width = 2
    while width < chunk:
        pw = _dot_hi(pw, pw)
        inv = inv + _dot_hi(inv, pw)
        width *= 2
    u = _dot_hi(inv, v * beta_col)
    w = _dot_hi(inv, k * (beta_col * jnp.exp(gcum_col)))

    head_match = _group_id((R, SD), 0, chunk) == _group_id((R, SD), 1, HEAD_DIM)

    def spread(x):
        return jnp.where(head_match, jnp.concatenate([x] * HEADS, axis=1), 0.0).astype(BF16)

    sb = S.astype(BF16)
    v_new = u - _dot(spread(w), sb)
    o = _dot(spread(q * jnp.exp(gcum_col)), sb) + _dot((qk * decay).astype(BF16), v_new.astype(BF16))
    kg = spread(k * jnp.exp(gtot_col - gcum_col))
    tok = jnp.bitwise_and(lax.broadcasted_iota(jnp.int32, (R, SD), 0), chunk - 1)
    first = jnp.where(head_match & (tok == 0), 1.0, 0.0).astype(BF16)
    c0, c1, c2 = _split3(cols)
    gtot_state = _dot_tn(first, c0) + (_dot_tn(first, c1) + _dot_tn(first, c2))
    s_new = S * jnp.exp(gtot_state[:, 2:3]) + _dot_tn(kg, v_new.astype(BF16))
    return o, s_new


def _gdn_body(q_ref, k_ref, v_ref, gb_ref, s0_ref, o_ref, sT_ref, s_ref, *, chunk, group):
    c = pl.program_id(1)

    @pl.when(c == 0)
    def _init():
        s_ref[...] = s0_ref[...]

    R = HEADS * chunk
    for g in range(group):
        o, s_new = _gdn_chunk(q_ref[g].reshape(R, HEAD_DIM), k_ref[g].reshape(R, HEAD_DIM),
                              v_ref[g].reshape(R, HEAD_DIM), gb_ref[g, 0], s_ref[g], chunk)
        o_ref[g] = o.reshape(HEADS, chunk, HEAD_DIM)
        s_ref[g] = s_new

    @pl.when(c == pl.num_programs(1) - 1)
    def _state():
        sT_ref[...] = s_ref[...]


def _gdn(q, k, v, gb, S0, *, chunk, group, name):
    B, H, T, hd = q.shape
    nc = T // chunk
    SD = H * hd
    tok = pl.BlockSpec((group, H, chunk, hd), lambda b, c: (b, 0, c, 0))
    st = pl.BlockSpec((group, SD, hd), lambda b, c: (b, 0, 0))
    return pl.pallas_call(
        functools.partial(_gdn_body, chunk=chunk, group=group),
        grid=(B // group, nc),
        in_specs=[tok, tok, tok, pl.BlockSpec((group, 1, 8, H * chunk), lambda b, c: (b, c, 0, 0)), st],
        out_specs=[tok, st],
        out_shape=[jax.ShapeDtypeStruct((B, H, T, hd), F32), jax.ShapeDtypeStruct((B, SD, hd), F32)],
        scratch_shapes=[pltpu.VMEM((group, SD, hd), F32)],
        compiler_params=_params("arbitrary", "arbitrary"),
        name=name,
    )(q, k, v, gb, S0)


def _merge_body(x_ref, b0_ref, b1_ref, b2_ref, b3_ref, gates_ref, wb_ref, wo_ref, o_ref):
    merged = None
    for n, b_ref in enumerate((b0_ref, b1_ref, b2_ref, b3_ref)):
        proj = _dot(b_ref[...], wb_ref[n])
        term = _sigmoid(gates_ref[:, n * D_MODEL:(n + 1) * D_MODEL]) * proj
        merged = term if merged is None else merged + term
    o_ref[...] = x_ref[...] + _dot(merged.astype(BF16), wo_ref[...])


def _merge(x2, branches, gates, wb, wo, *, row_tile, name):
    M, D = x2.shape
    row = lambda w: pl.BlockSpec((row_tile, w), lambda i: (i, 0))
    return pl.pallas_call(
        _merge_body,
        grid=(M // row_tile,),
        in_specs=[row(D)] + [row(BR_WIDTH)] * N_BRANCH + [row(N_BRANCH * D), _resident(wb.shape), _resident(wo.shape)],
        out_specs=row(D),
        out_shape=jax.ShapeDtypeStruct((M, D), F32),
        compiler_params=_params("arbitrary"),
        name=name,
    )(x2, *branches, gates, wb, wo)


def _mem_body(x_ref, g_ref, wq_ref, qg_ref, mk_ref, mv_ref, wo_ref, o_ref):
    x = x_ref[0]
    h = (x * _rms_scale(x) * g_ref[...]).astype(BF16)
    q = _dot(h, wq_ref[...])
    W = q.shape[1]
    head_sum = jnp.where(_group_id((W, W), 0, MEM_HD) == _group_id((W, W), 1, MEM_HD), 1.0, 0.0).astype(BF16)
    ms = _dot_sel(q * q, head_sum) * (1.0 / MEM_HD)
    qn = q * lax.rsqrt(ms + NORM_EPS) * qg_ref[...] * (1.0 / math.sqrt(MEM_HD))
    mk = mk_ref[0].astype(BF16)
    mv = mv_ref[0].astype(BF16)
    lane_head = _group_id(q.shape, 1, MEM_HD)
    o = jnp.zeros(q.shape, F32)
    for hh in range(MEM_HEADS):
        s = _dot_nt(jnp.where(lane_head == hh, qn, 0.0).astype(BF16), mk)
        p = jnp.exp(s - jnp.max(s, axis=1, keepdims=True))
        p = p / jnp.sum(p, axis=1, keepdims=True)
        o = o + jnp.where(lane_head == hh, _dot(p.astype(BF16), mv), 0.0)
    o_ref[0] = x + _dot(o.astype(BF16), wo_ref[...])


def _mem_sublayer(x, g, wq, qg, mk, mv, wo, *, row_tile, name):
    B, T, D = x.shape
    nt = T // row_tile
    kv = pl.BlockSpec((1,) + mk.shape[1:], lambda b, t: (b, 0, 0))
    return pl.pallas_call(
        _mem_body,
        grid=(B, nt),
        in_specs=[pl.BlockSpec((1, row_tile, D), lambda b, t: (b, t, 0)), _resident((1, D)), _resident(wq.shape),
                  _resident((1, MEM_WIDTH)), kv, kv, _resident(wo.shape)],
        out_specs=pl.BlockSpec((1, row_tile, D), lambda b, t: (b, t, 0)),
        out_shape=jax.ShapeDtypeStruct((B, T, D), F32),
        compiler_params=_params("arbitrary", "arbitrary"),
        name=name,
    )(x, g.reshape(1, D), wq, jnp.tile(qg, MEM_HEADS).reshape(1, MEM_WIDTH), mk, mv, wo)


def _mlp_body(x_ref, g_ref, wu_ref, wd_ref, o_ref, *, ff_chunk):
    x = x_ref[...]
    h = (x * _rms_scale(x) * g_ref[...]).astype(BF16)
    acc = x
    for c in range(D_FF // ff_chunk):
        u = _dot(h, wu_ref[:, c * ff_chunk:(c + 1) * ff_chunk])
        u = jnp.square(jnp.maximum(u, 0.0)).astype(BF16)
        acc = acc + _dot(u, wd_ref[c * ff_chunk:(c + 1) * ff_chunk, :])
    o_ref[...] = acc


def _mlp_sublayer(x2, g, w_up, w_down, *, row_tile, name):
    M, D = x2.shape
    return pl.pallas_call(
        functools.partial(_mlp_body, ff_chunk=1024),
        grid=(M // row_tile,),
        in_specs=[pl.BlockSpec((row_tile, D), lambda i: (i, 0)), _resident((1, D)), _resident(w_up.shape),
                  _resident(w_down.shape)],
        out_specs=pl.BlockSpec((row_tile, D), lambda i: (i, 0)),
        out_shape=jax.ShapeDtypeStruct((M, D), F32),
        compiler_params=_params("arbitrary"),
        name=name,
    )(x2, g.reshape(1, D), w_up, w_down)


def _rmsnorm(x, g):
    xf = x.astype(F32)
    return xf * lax.rsqrt(jnp.mean(xf * xf, axis=-1, keepdims=True) + NORM_EPS) * g.astype(F32)


def _l2norm(x):
    return x * lax.rsqrt(jnp.sum(x * x, axis=-1, keepdims=True) + NORM_EPS)


def _rope(x, pos):
    d = x.shape[-1]
    inv = ROPE_THETA ** (-jnp.arange(0, d, 2, dtype=F32) / d)
    ang = pos.astype(F32)[:, None] * inv[None, :]
    ang = ang.reshape((1, ang.shape[0]) + (1,) * (x.ndim - 3) + (d // 2,))
    cos, sin = jnp.cos(ang), jnp.sin(ang)
    x1, x2 = x[..., : d // 2], x[..., d // 2:]
    return jnp.concatenate([x1 * cos - x2 * sin, x2 * cos + x1 * sin], axis=-1)


def _causal_conv(x, buf, w, b=None):
    xp = jnp.concatenate([buf.astype(x.dtype), x], axis=1)
    T = x.shape[1]
    y = xp[:, 0:T] * w[0]
    for i in range(1, CONV_W):
        y = y + xp[:, i:i + T] * w[i]
    if b is not None:
        y = y + b
    return y, xp[:, -(CONV_W - 1):]


def _key_minor_pages(cache):
    nd = cache.ndim
    t = jnp.transpose(cache, (0, 1) + tuple(range(3, nd)) + (2,))
    return t.reshape(cache.shape[:2] + (-1, cache.shape[2]))


def _head_major(x):
    return jnp.swapaxes(x, 1, 2)


def _pad_time(x, t_pad, axis=1):
    pad = [(0, 0)] * x.ndim
    pad[axis] = (0, t_pad - x.shape[axis])
    return jnp.pad(x, pad)


def _pack_layer_weights(p):
    o = np.cumsum((0,) + (3 * BR_WIDTH, FOX_HEADS, LRU_WIDTH, LRU_WIDTH, 3 * BR_WIDTH, 3 * BR_WIDTH, GDN_HEADS,
                          GDN_HEADS, BR_WIDTH, N_BRANCH * D_MODEL)).tolist()
    w = p['w_in']
    seg = lambda i: w[:, o[i]:o[i + 1]]
    small = jnp.concatenate([seg(1), seg(6), seg(7), jnp.zeros((D_MODEL, SMALL_COLS - 3 * HEADS), w.dtype)], axis=1)
    w_main = jnp.concatenate([seg(0), seg(2), seg(3), seg(4), seg(5), seg(8), small], axis=1).astype(BF16)
    blockdiag = lambda wb: jax.scipy.linalg.block_diag(*[wb[i] for i in range(LRU_BLOCKS)]).astype(BF16)
    return dict(w_main=w_main, w_gates=seg(9).astype(BF16), lru_wa=blockdiag(p['lru_w_a']),
                lru_wx=blockdiag(p['lru_w_x']), w_branch=p['w_branch'].astype(BF16), w_out=p['w_out'].astype(BF16),
                w_mem_q=p['w_mem_q'].astype(BF16), w_mem_kv=p['w_mem_kv'].astype(BF16),
                w_mem_o=p['w_mem_o'].astype(BF16), w_mlp_up=p['w_mlp_up'].astype(BF16),
                w_mlp_down=p['w_mlp_down'].astype(BF16))


_C_FOX, _C_LRUX, _C_LRUG, _C_DIFF, _C_GDN, _C_GDNZ, _C_SMALL = 0, 768, 1024, 1280, 2048, 2816, 3072


def _mixer_sublayer(x, pos, p, pw, lam_init, fox_past, diff_past, lru_h0, lru_buf, gdn_S0, gdn_buf, *, row_tile,
                    attn_tile, tag):
    B, T, D = x.shape
    M = B * T
    x2 = x.reshape(M, D)
    zm, gates = _norm_matmul(x2, p['norm_mix'], [pw['w_main'], pw['w_gates']], row_tile=row_tile, name=tag + "in_proj")
    zm = zm.reshape(B, T, MAIN_COLS)
    seg = lambda c0, w: zm[..., c0:c0 + w]

    r = seg(_C_FOX, 3 * BR_WIDTH).reshape(B, T, 3, FOX_HEADS, FOX_HD)
    fq = _rmsnorm(r[:, :, 0], p['fox_q_norm'])
    fk = _rmsnorm(r[:, :, 1], p['fox_k_norm'])
    fv = r[:, :, 2]
    flogf = jax.nn.log_sigmoid(seg(_C_SMALL, FOX_HEADS) + p['fox_b_f'])
    if fox_past is None:
        c = jnp.cumsum(flogf, axis=1)
        fox_out = _flash_attention(_head_major(fq).astype(BF16), _head_major(fk).astype(BF16),
                                   _head_major(fv).astype(BF16), tile=attn_tile, n_maps=1,
                                   scale=1.0 / math.sqrt(FOX_HD), bias=(c, jnp.swapaxes(c, 1, 2)), name=tag + "fox_attn")
    else:
        fo = _paged_attention(_query_rows(fq[:, :, :, None], 1), fox_past['k'], fox_past['v'],
                              _new_page(fk.reshape(B, T, BR_WIDTH)), _new_page(fv.reshape(B, T, BR_WIDTH)),
                              fox_past['page_table'], fox_past['layer'], pages_per_step=fox_past['pages_per_step'],
                              n_maps=1, scale=1.0 / math.sqrt(FOX_HD), lf_cache_t=fox_past['logf'],
                              lfnew_t=_new_page(flogf), name=tag + "fox_paged")
        fox_out = fo[:, :T].reshape(M, BR_WIDTH).astype(BF16)

    xc, lru_buf_new = _causal_conv(seg(_C_LRUX, LRU_WIDTH), lru_buf, p['lru_conv_w'], p['lru_conv_b'])
    lru_rows = min(256, T) if T % 8 == 0 else 8
    t_pad = -(-T // lru_rows) * lru_rows
    lru_out, lru_hT = _lru(_pad_time(xc, t_pad), _pad_time(seg(_C_LRUG, LRU_WIDTH), t_pad), pw['lru_wa'], pw['lru_wx'],
                           p['lru_b_a'], p['lru_b_x'], jax.nn.softplus(-p['lru_lambda']), lru_h0,
                           rows=lru_rows, t_valid=T, name=tag + "lru")
    if t_pad != T:
        lru_out = lru_out.reshape(B, t_pad, LRU_WIDTH)[:, :T].reshape(M, LRU_WIDTH)

    r = seg(_C_DIFF, 3 * BR_WIDTH).reshape(B, T, 3, DIFF_HEADS, DIFF_VD)
    dq = _rope(_rmsnorm(r[:, :, 0].reshape(B, T, DIFF_HEADS, 2, DIFF_HD), p['diff_q_norm']), pos)
    dk = _rope(_rmsnorm(r[:, :, 1].reshape(B, T, DIFF_HEADS, 2, DIFF_HD), p['diff_k_norm']), pos)
    dvv = r[:, :, 2]
    lp = p['diff_lambda']
    lam = jnp.exp(jnp.sum(lp[0] * lp[1])) - jnp.exp(jnp.sum(lp[2] * lp[3])) + lam_init
    if diff_past is None:
        flat = lambda a: _head_major(a.reshape(B, T, DIFF_HEADS, DIFF_VD)).astype(BF16)
        diff_out = _flash_attention(flat(dq), flat(dk), flat(dvv), tile=attn_tile, n_maps=2,
                                    scale=1.0 / math.sqrt(DIFF_HD), lam=lam, gsub=p['diff_sub_norm'],
                                    sub_scale=1.0 - lam_init, name=tag + "diff_attn")
    else:
        do = _paged_attention(_query_rows(dq, 2), diff_past['k'], diff_past['v'],
                              _new_page(dk.reshape(B, T, BR_WIDTH)), _new_page(dvv.reshape(B, T, BR_WIDTH)),
                              diff_past['page_table'], diff_past['layer'], pages_per_step=diff_past['pages_per_step'],
                              n_maps=2, scale=1.0 / math.sqrt(DIFF_HD), lam=lam, gsub=p['diff_sub_norm'],
                              sub_scale=1.0 - lam_init, name=tag + "diff_paged")
        diff_out = do[:, :T].reshape(M, BR_WIDTH).astype(BF16)

    gc, gdn_buf_new = _causal_conv(seg(_C_GDN, 3 * BR_WIDTH), gdn_buf, p['gdn_conv_w'])
    gc = jax.nn.silu(gc).reshape(B, T, 3, GDN_HEADS, GDN_HD)
    gq = _l2norm(gc[:, :, 0]) * (GDN_HD ** -0.5)
    gk = _l2norm(gc[:, :, 1])
    gvv = gc[:, :, 2]
    gbeta = jax.nn.sigmoid(seg(_C_SMALL + 2 * HEADS, GDN_HEADS))
    gg = -jnp.exp(p['gdn_A_log']) * jax.nn.softplus(seg(_C_SMALL + HEADS, GDN_HEADS) + p['gdn_dt_bias'])
    chunk = 64 if T % 64 == 0 else 32
    tg = -(-T // chunk) * chunk
    nc = tg // chunk
    rows_hc = lambda a: jnp.swapaxes(_pad_time(a, tg).reshape(B, nc, chunk, GDN_HEADS), 2, 3).reshape(B, nc, 1, GDN_HEADS * chunk)
    g_r, b_r = rows_hc(gg), rows_hc(gbeta)
    gb = jnp.concatenate([g_r, b_r, g_r, jnp.zeros((B, nc, 5, GDN_HEADS * chunk), F32)], axis=2)
    hm = lambda a: _head_major(_pad_time(a, tg))
    go, gS = _gdn(hm(gq), hm(gk), hm(gvv), gb, gdn_S0.reshape(B, GDN_HEADS * GDN_HD, GDN_HD), chunk=chunk,
                  group=2, name=tag + "gdn")
    go = _head_major(go)[:, :T]
    go = _rmsnorm(go, p['gdn_out_norm']) * jax.nn.silu(seg(_C_GDNZ, BR_WIDTH).reshape(B, T, GDN_HEADS, GDN_HD))
    gdn_out = go.reshape(M, BR_WIDTH).astype(BF16)

    x2 = _merge(x2, (fox_out, lru_out, diff_out, gdn_out), gates, pw['w_branch'], pw['w_out'], row_tile=row_tile,
                name=tag + "merge")
    state = (fk, fv, flogf, dk, dvv, lru_hT, lru_buf_new, gS.reshape(B, GDN_HEADS, GDN_HD, GDN_HD), gdn_buf_new)
    return x2.reshape(B, T, D), state


def _memory_kv(mem, p, pw):
    B, Tm, D = mem.shape
    kv, = _norm_matmul(mem.reshape(B * Tm, D), p['norm_mem_src'], [pw['w_mem_kv']], row_tile=256, name="mem_kv")
    kv = kv.reshape(B, Tm, 2, MEM_HEADS, MEM_HD)
    return _rmsnorm(kv[:, :, 0], p['mem_k_norm']), kv[:, :, 1]


def _memory_sublayer(x, mk, mv, p, pw, *, row_tile, name):
    B, T, D = x.shape
    t_pad = -(-T // row_tile) * row_tile
    flat = lambda a: a.reshape(a.shape[0], a.shape[1], MEM_WIDTH)
    out = _mem_sublayer(_pad_time(x, t_pad), p['norm_mem'], pw['w_mem_q'], p['mem_q_norm'], flat(mk), flat(mv),
                        pw['w_mem_o'], row_tile=row_tile, name=name)
    return out[:, :T]


def kernel(x_prompt, x_sample, cache_fox_k, cache_fox_v, cache_fox_logf, cache_diff_k, cache_diff_v, cache_mem_k, cache_mem_v, state_lru_h, state_lru_conv, state_gdn_S, state_gdn_conv, page_table, mem_prompt, norm_mix, w_in, fox_b_f, fox_q_norm, fox_k_norm, lru_conv_w, lru_conv_b, lru_w_a, lru_b_a, lru_w_x, lru_b_x, lru_lambda, diff_q_norm, diff_k_norm, diff_lambda, diff_sub_norm, gdn_conv_w, gdn_A_log, gdn_dt_bias, gdn_out_norm, w_branch, w_out, norm_mem, norm_mem_src, w_mem_q, w_mem_kv, mem_q_norm, mem_k_norm, w_mem_o, norm_mlp, w_mlp_up, w_mlp_down):
    stacked = dict(norm_mix=norm_mix, w_in=w_in, fox_b_f=fox_b_f, fox_q_norm=fox_q_norm, fox_k_norm=fox_k_norm,
                   lru_conv_w=lru_conv_w, lru_conv_b=lru_conv_b, lru_w_a=lru_w_a, lru_b_a=lru_b_a,
                   lru_w_x=lru_w_x, lru_b_x=lru_b_x, lru_lambda=lru_lambda, diff_q_norm=diff_q_norm,
                   diff_k_norm=diff_k_norm, diff_lambda=diff_lambda, diff_sub_norm=diff_sub_norm,
                   gdn_conv_w=gdn_conv_w, gdn_A_log=gdn_A_log, gdn_dt_bias=gdn_dt_bias, gdn_out_norm=gdn_out_norm,
                   w_branch=w_branch, w_out=w_out, norm_mem=norm_mem, norm_mem_src=norm_mem_src, w_mem_q=w_mem_q,
                   w_mem_kv=w_mem_kv, mem_q_norm=mem_q_norm, mem_k_norm=mem_k_norm, w_mem_o=w_mem_o,
                   norm_mlp=norm_mlp, w_mlp_up=w_mlp_up, w_mlp_down=w_mlp_down)
    Bp, Tp, D = x_prompt.shape
    Bs, Ts, _ = x_sample.shape
    past_len = page_table.shape[1] * cache_fox_k.shape[2]
    pos_p = jnp.arange(Tp)
    pos_s = past_len + jnp.arange(Ts)
    h0_p = jnp.zeros((Bp, LRU_WIDTH), F32)
    lbuf_p = jnp.zeros((Bp, CONV_W - 1, LRU_WIDTH), F32)
    S0_p = jnp.zeros((Bp, GDN_HEADS, GDN_HD, GDN_HD), F32)
    gbuf_p = jnp.zeros((Bp, CONV_W - 1, 3 * BR_WIDTH), F32)

    fox_k_t, fox_v_t, diff_k_t, diff_v_t = (_key_minor_pages(c) for c in (cache_fox_k, cache_fox_v, cache_diff_k,
                                                                      cache_diff_v))
    fox_logf_t = jnp.swapaxes(cache_fox_logf, 2, 3)

    xp, xs = x_prompt, x_sample
    pst = [[] for _ in range(11)]
    sst = [[] for _ in range(9)]
    for l in range(DEPTH):
        p = {name: arr[l] for name, arr in stacked.items()}
        pw = _pack_layer_weights(p)
        lam_init = 0.8 - 0.6 * math.exp(-0.3 * l)
        xp, st_p = _mixer_sublayer(xp, pos_p, p, pw, lam_init, None, None, h0_p, lbuf_p, S0_p, gbuf_p,
                                   row_tile=256, attn_tile=512, tag="p_")
        mk, mv = _memory_kv(mem_prompt, p, pw)
        xp = _memory_sublayer(xp, mk, mv, p, pw, row_tile=512, name="p_mem")
        xp = _mlp_sublayer(xp.reshape(Bp * Tp, D), p['norm_mlp'], pw['w_mlp_up'], pw['w_mlp_down'], row_tile=512,
                           name="p_mlp").reshape(Bp, Tp, D)
        for lst, val in zip(pst, (st_p[0], st_p[1], st_p[2], st_p[3], st_p[4], mk, mv,
                                  st_p[5], st_p[6], st_p[7], st_p[8])):
            lst.append(val)
        fox_past = dict(k=fox_k_t, v=fox_v_t, logf=fox_logf_t, page_table=page_table, layer=l, pages_per_step=16)
        diff_past = dict(k=diff_k_t, v=diff_v_t, page_table=page_table, layer=l, pages_per_step=16)
        xs, st_s = _mixer_sublayer(xs, pos_s, p, pw, lam_init, fox_past, diff_past, state_lru_h[l], state_lru_conv[l],
                                   state_gdn_S[l], state_gdn_conv[l], row_tile=Bs * Ts, attn_tile=None, tag="s_")
        xs = _memory_sublayer(xs, cache_mem_k[l], cache_mem_v[l], p, pw, row_tile=8, name="s_mem")
        xs = _mlp_sublayer(xs.reshape(Bs * Ts, D), p['norm_mlp'], pw['w_mlp_up'], pw['w_mlp_down'], row_tile=Bs * Ts,
                           name="s_mlp").reshape(Bs, Ts, D)
        for lst, val in zip(sst, st_s):
            lst.append(val)

    return tuple([xp, xs] + [jnp.stack(v_, axis=0) for v_ in pst] + [jnp.stack(v_, axis=0) for v_ in sst])
```

```python
import functools
import math

import jax
import jax.numpy as jnp
import numpy as np
from jax import lax
from jax.experimental import pallas as pl
from jax.experimental.pallas import tpu as pltpu

D_MODEL = 1024
DEPTH = 2
N_BRANCH = 4
BR_WIDTH = D_MODEL // 4
FOX_HEADS = 4
FOX_HD = BR_WIDTH // FOX_HEADS
LRU_WIDTH = BR_WIDTH
LRU_BLOCKS = 4
LRU_BD = LRU_WIDTH // LRU_BLOCKS
LRU_C = 8.0
CONV_W = 4
DIFF_HEADS = 4
DIFF_VD = BR_WIDTH // DIFF_HEADS
DIFF_HD = DIFF_VD // 2
GDN_HEADS = 4
GDN_HD = BR_WIDTH // GDN_HEADS
MEM_HEADS = 4
MEM_HD = 64
MEM_WIDTH = MEM_HEADS * MEM_HD
D_FF = 4 * D_MODEL
ROPE_THETA = 10000.0
Q_BLOCK = 128
NORM_EPS = 1e-6
HEADS = 4
HEAD_DIM = 64
SMALL_COLS = 128
MAIN_COLS = 12 * BR_WIDTH + SMALL_COLS

F32 = jnp.float32
BF16 = jnp.bfloat16
NEG_BIG = -1e30
LANES = 128
VMEM_LIMIT_BYTES = 56 * 1024 * 1024


def _params(*sem):
    return pltpu.CompilerParams(dimension_semantics=sem, vmem_limit_bytes=VMEM_LIMIT_BYTES)


def _resident(shape):
    nd = len(shape)
    return pl.BlockSpec(shape, lambda *_: (0,) * nd, pipeline_mode=pl.Buffered(1))


def _dot(a, b):
    return jnp.dot(a, b, preferred_element_type=F32)


def _dot_nt(a, b):
    return lax.dot_general(a, b, (((1,), (1,)), ((), ())), preferred_element_type=F32)


def _dot_tn(a, b):
    return lax.dot_general(a, b, (((0,), (0,)), ((), ())), preferred_element_type=F32)


def _split3(x):
    p0 = x.astype(BF16)
    r = x - p0.astype(F32)
    p1 = r.astype(BF16)
    p2 = (r - p1.astype(F32)).astype(BF16)
    return p0, p1, p2


def _split3_trunc(x):
    def top(v):
        return lax.bitcast_convert_type(lax.bitcast_convert_type(v, jnp.uint32) & jnp.uint32(0xFFFF0000), F32)
    p0 = top(x)
    r = x - p0
    p1 = top(r)
    p2 = top(r - p1)
    return p0.astype(BF16), p1.astype(BF16), p2.astype(BF16)


def _dot_sel(x, sel_bf16):
    p0, p1, p2 = _split3(x)
    return _dot(p0, sel_bf16) + (_dot(p1, sel_bf16) + _dot(p2, sel_bf16))


def _sel_dot_nt(sel_bf16, x):
    p0, p1, p2 = _split3(x)
    return _dot_nt(sel_bf16, p0) + (_dot_nt(sel_bf16, p1) + _dot_nt(sel_bf16, p2))


def _sel_dot(sel_bf16, x):
    p0, p1, p2 = _split3(x)
    return _dot(sel_bf16, p0) + (_dot(sel_bf16, p1) + _dot(sel_bf16, p2))


def _group_id(shape, axis, size):
    return lax.shift_right_logical(lax.broadcasted_iota(jnp.int32, shape, axis), int(math.log2(size)))


def _rms_scale(x):
    return lax.rsqrt(jnp.mean(x * x, axis=-1, keepdims=True) + NORM_EPS)


def _sigmoid(x):
    return 1.0 / (1.0 + jnp.exp(-x))


def _norm_matmul_body(x_ref, g_ref, *refs, n_out, col_chunk, transposed):
    w_refs, o_refs = refs[:n_out], refs[n_out:]
    x = x_ref[...]
    h = (x * _rms_scale(x) * g_ref[...]).astype(BF16)
    for w_ref, o_ref in zip(w_refs, o_refs):
        n = w_ref.shape[0 if transposed else 1]
        for c0 in range(0, n, col_chunk):
            c1 = min(n, c0 + col_chunk)
            o_ref[:, c0:c1] = _dot_nt(h, w_ref[c0:c1, :]) if transposed else _dot(h, w_ref[:, c0:c1])


def _norm_matmul(x2, g, ws, *, row_tile, name, transposed=False):
    M, D = x2.shape
    widths = [w.shape[0 if transposed else 1] for w in ws]
    return pl.pallas_call(
        functools.partial(_norm_matmul_body, n_out=len(ws), col_chunk=1024, transposed=transposed),
        grid=(M // row_tile,),
        in_specs=[pl.BlockSpec((row_tile, D), lambda i: (i, 0)), _resident((1, D))] + [_resident(w.shape) for w in ws],
        out_specs=[pl.BlockSpec((row_tile, n), lambda i: (i, 0)) for n in widths],
        out_shape=[jax.ShapeDtypeStruct((M, n), F32) for n in widths],
        compiler_params=_params("arbitrary"),
        name=name,
    )(x2, g.reshape(1, D), *ws)


def _flash_t_body(*refs, n_maps, tq, tk, group, sub_scale):
    if n_maps == 1:
        qt_ref, k_ref, vt_ref, o_ref = refs
    else:
        qt_ref, k_ref, vt_ref, lam_ref, gsub_ref, o_ref = refs
    i = pl.program_id(1)
    q0 = i * tq
    n_full = q0 // tk
    n_diag = tq // tk

    def step(pairs, qts, j, carries, masked):
        start = pl.multiple_of(j * tk, tk)
        if masked:
            kpos = start + lax.broadcasted_iota(jnp.int32, (tk, tq), 0)
            qpos = q0 + lax.broadcasted_iota(jnp.int32, (tk, tq), 1)
            keep = kpos <= qpos
        new = []
        for (h, _), qt, (m, l, acc) in zip(pairs, qts, carries):
            st = _dot(k_ref[0, h, pl.ds(start, tk), :], qt)
            if masked:
                st = jnp.where(keep, st, NEG_BIG)
            m_new = jnp.maximum(m, jnp.max(st, axis=0, keepdims=True))
            p = jnp.exp(st - m_new)
            alpha = jnp.exp(m - m_new)
            l = alpha * l + jnp.sum(p, axis=0, keepdims=True)
            acc = alpha * acc + _dot(vt_ref[0, h, :, pl.ds(start, tk)], p.astype(BF16))
            new.append((m_new, l, acc))
        return tuple(new)

    all_pairs = [(h, mp) for h in range(HEADS) for mp in range(n_maps)]
    normalized = {}
    for g0 in range(0, len(all_pairs), group):
        pairs = all_pairs[g0:g0 + group]
        qts = [qt_ref[0, h * n_maps + mp] for h, mp in pairs]
        init = tuple((jnp.full((1, tq), NEG_BIG, F32), jnp.zeros((1, tq), F32), jnp.zeros((HEAD_DIM, tq), F32))
                     for _ in pairs)
        carries = lax.fori_loop(0, n_full, lambda j, c: step(pairs, qts, j, c, False), init)
        for dj in range(n_diag):
            carries = step(pairs, qts, n_full + dj, carries, True)
        for pair, (_, l, acc) in zip(pairs, carries):
            normalized[pair] = acc / l
    outs = []
    for h in range(HEADS):
        if n_maps == 1:
            outs.append(normalized[(h, 0)])
        else:
            d = normalized[(h, 0)] - lam_ref[...] * normalized[(h, 1)]
            ms = jnp.mean(d * d, axis=0, keepdims=True)
            outs.append(d * lax.rsqrt(ms + NORM_EPS) * gsub_ref[...] * sub_scale)
    o_ref[...] = jnp.concatenate(outs, axis=0).T.astype(o_ref.dtype)


def _flash_attention_t(qt, kx, vt, *, tq, tk, group=4, lam=None, gsub=None, sub_scale=1.0, name):
    B, H, T, kw = kx.shape
    n_maps = qt.shape[1] // H
    nq = T // tq
    operands = [qt, kx, vt]
    in_specs = [pl.BlockSpec((1, H * n_maps, kw, tq), lambda b, i: (b, 0, 0, i)),
                pl.BlockSpec((1, H, T, kw), lambda b, i: (b, 0, 0, 0)),
                pl.BlockSpec((1, H, HEAD_DIM, T), lambda b, i: (b, 0, 0, 0))]
    if n_maps > 1:
        operands += [lam.reshape(1, 1), gsub.reshape(HEAD_DIM, 1)]
        in_specs += [_resident((1, 1)), _resident((HEAD_DIM, 1))]
    return pl.pallas_call(
        functools.partial(_flash_t_body, n_maps=n_maps, tq=tq, tk=tk, group=group, sub_scale=sub_scale),
        grid=(B, nq),
        in_specs=in_specs,
        out_specs=pl.BlockSpec((tq, H * HEAD_DIM), lambda b, i: (b * nq + i, 0)),
        out_shape=jax.ShapeDtypeStruct((B * T, H * HEAD_DIM), BF16),
        compiler_params=_params("arbitrary", "arbitrary"),
        name=name,
    )(*operands)


def _pad_lanes(x, width=LANES):
    return jnp.pad(x, [(0, 0)] * (x.ndim - 1) + [(0, width - x.shape[-1])])


Q_ROWS = 8


def _paged_body(pt_ref, *refs, n_pages, n_maps, use_bias, scale, sub_scale):
    del pt_ref
    refs = list(refs)
    q_ref = refs.pop(0)
    lam_ref, gsub_ref = (None, None) if use_bias else (refs.pop(0), refs.pop(0))
    knew_ref, vnew_ref = refs.pop(0), refs.pop(0)
    lfnew_ref = refs.pop(0) if use_bias else None
    k_refs, refs = refs[:n_pages], refs[n_pages:]
    v_refs, refs = refs[:n_pages], refs[n_pages:]
    lf_refs, refs = (refs[:n_pages], refs[n_pages:]) if use_bias else (None, refs)
    o_ref, m_ref, l_ref, acc_ref = refs[:4]
    carry_ref = refs[4] if use_bias else None
    g = pl.program_id(1)
    R = HEADS * n_maps * Q_ROWS
    C = HEADS * HEAD_DIM

    @pl.when(g == 0)
    def _init():
        m_ref[...] = jnp.full(m_ref.shape, NEG_BIG, F32)
        l_ref[...] = jnp.zeros(l_ref.shape, F32)
        acc_ref[...] = jnp.zeros(acc_ref.shape, F32)
        if use_bias:
            carry_ref[...] = jnp.zeros(carry_ref.shape, F32)

    q = q_ref[0]

    def head_rows(x):
        return jnp.concatenate([jnp.broadcast_to(x[h:h + 1], (Q_ROWS, x.shape[1])) for h in range(HEADS)], axis=0)

    def attend(kts, vts, lfs, causal_new):
        kt = jnp.concatenate(kts, axis=1).astype(BF16)
        s = _dot(q, kt) * scale
        if use_bias:
            upper = jnp.where(lax.broadcasted_iota(jnp.int32, (LANES, LANES), 0)
                              <= lax.broadcasted_iota(jnp.int32, (LANES, LANES), 1), 1.0, 0.0).astype(BF16)
            rows = HEADS * Q_ROWS
            n = len(lfs)
            within = _dot_sel(jnp.concatenate([head_rows(lf) for lf in lfs], axis=0), upper)
            incl = jnp.broadcast_to(within[:, LANES - 1:LANES], within.shape)
            d = 1
            while d < n:
                incl = incl + jnp.concatenate([jnp.zeros((d * rows, LANES), F32), incl[:(n - d) * rows]], axis=0)
                d *= 2
            carry = carry_ref[...]
            before = jnp.concatenate([jnp.zeros((rows, LANES), F32), incl[:(n - 1) * rows]], axis=0) if n > 1 else 0.0
            c_all = within + before
            carry_ref[...] = carry + incl[(n - 1) * rows:]
            s = s - (jnp.concatenate([c_all[i * rows:(i + 1) * rows] for i in range(n)], axis=1)
                     + jnp.concatenate([carry] * n, axis=1))
        if causal_new:
            t = jnp.bitwise_and(lax.broadcasted_iota(jnp.int32, s.shape, 0), Q_ROWS - 1)
            s = jnp.where(lax.broadcasted_iota(jnp.int32, s.shape, 1) <= t, s, NEG_BIG)
        m_prev = m_ref[...]
        m_new = jnp.maximum(m_prev, jnp.max(s, axis=1, keepdims=True))
        alpha = jnp.exp(m_prev - m_new)
        p = jnp.exp(s - m_new)
        l_ref[...] = alpha * l_ref[...] + jnp.sum(p, axis=1, keepdims=True)
        vt = jnp.concatenate(vts, axis=1).astype(BF16)
        acc_ref[...] = alpha * acc_ref[...] + _dot_nt(p.astype(BF16), vt)
        m_ref[...] = m_new

    attend([r[0, 0] for r in k_refs], [r[0, 0] for r in v_refs], [r[0, 0] for r in lf_refs] if use_bias else None, False)

    @pl.when(g == pl.num_programs(1) - 1)
    def _finish():
        attend([knew_ref[0]], [vnew_ref[0]], [lfnew_ref[0]] if use_bias else None, True)
        full = acc_ref[...] / l_ref[...]
        lane_head = _group_id((Q_ROWS, C), 1, HEAD_DIM)
        outs = []
        for mp in range(n_maps):
            o = jnp.zeros((Q_ROWS, C), F32)
            for h in range(HEADS):
                r0 = (h * n_maps + mp) * Q_ROWS
                o = o + jnp.where(lane_head == h, full[r0:r0 + Q_ROWS], 0.0)
            outs.append(o)
        if n_maps == 1:
            res = outs[0]
        else:
            d = outs[0] - lam_ref[...] * outs[1]
            head_sum = jnp.where(_group_id((C, C), 0, HEAD_DIM) == _group_id((C, C), 1, HEAD_DIM), 1.0, 0.0).astype(BF16)
            ms = _dot_sel(d * d, head_sum) * (1.0 / HEAD_DIM)
            res = d * lax.rsqrt(ms + NORM_EPS) * gsub_ref[...] * sub_scale
        o_ref[0] = res.astype(o_ref.dtype)


def _paged_attention(q_rows, k_cache_t, v_cache_t, knew_t, vnew_t, page_table, layer, *, pages_per_step, n_maps, scale,
                     lf_cache_t=None, lfnew_t=None, lam=None, gsub=None, sub_scale=1.0, name):
    B, R, C = q_rows.shape
    n_pages_total = page_table.shape[1]
    P = pages_per_step
    use_bias = lf_cache_t is not None
    per_seq = lambda shape: pl.BlockSpec((1,) + shape, lambda b, g, pt: (b,) + (0,) * len(shape))
    const = lambda shape: pl.BlockSpec(shape, lambda b, g, pt: (0,) * len(shape))

    def page_spec(i, tail):
        return pl.BlockSpec((1, 1) + tail, lambda b, g, pt: (layer, pt[b, g * P + i]) + (0,) * len(tail))

    operands = [q_rows]
    in_specs = [per_seq((R, C))]
    if not use_bias:
        operands += [lam.reshape(1, 1), jnp.tile(gsub, HEADS).reshape(1, C)]
        in_specs += [const((1, 1)), const((1, C))]
    operands += [knew_t, vnew_t]
    in_specs += [per_seq((C, LANES)), per_seq((C, LANES))]
    if use_bias:
        operands.append(lfnew_t)
        in_specs.append(per_seq((HEADS, LANES)))
    kv_tail = (C, LANES)
    operands += [k_cache_t] * P + [v_cache_t] * P
    in_specs += [page_spec(i, kv_tail) for i in range(P)] + [page_spec(i, kv_tail) for i in range(P)]
    scratch = [pltpu.VMEM((R, 1), F32), pltpu.VMEM((R, 1), F32), pltpu.VMEM((R, C), F32)]
    if use_bias:
        operands += [lf_cache_t] * P
        in_specs += [page_spec(i, (HEADS, LANES)) for i in range(P)]
        scratch.append(pltpu.VMEM((HEADS * Q_ROWS, LANES), F32))
    return pl.pallas_call(
        functools.partial(_paged_body, n_pages=P, n_maps=n_maps, use_bias=use_bias, scale=scale, sub_scale=sub_scale),
        grid_spec=pltpu.PrefetchScalarGridSpec(
            num_scalar_prefetch=1, grid=(B, n_pages_total // P), in_specs=in_specs,
            out_specs=pl.BlockSpec((1, Q_ROWS, C), lambda b, g, pt: (b, 0, 0)), scratch_shapes=scratch),
        out_shape=jax.ShapeDtypeStruct((B, Q_ROWS, C), F32),
        compiler_params=_params("arbitrary", "arbitrary"),
        name=name,
    )(page_table, *operands)


def _query_rows(q, n_maps):
    B, T, H, M, d = q.shape
    eye = jnp.eye(H * M, dtype=q.dtype).reshape(H, M, H, M)
    rows = jnp.einsum('bthmd,hmgn->bhmtgnd', q, eye)
    rows = jnp.pad(rows, ((0, 0), (0, 0), (0, 0), (0, Q_ROWS - T), (0, 0), (0, 0), (0, 0)))
    return rows.reshape(B, H * M * Q_ROWS, H * M * d).astype(BF16)


def _new_page(x):
    return jnp.pad(jnp.swapaxes(x, 1, 2), ((0, 0), (0, 0), (0, LANES - x.shape[1])))


def _gelu_tanh(x):
    return 0.5 * x * (1.0 + jnp.tanh(math.sqrt(2.0 / math.pi) * (x + 0.044715 * (x * x * x))))


def _lru_body(xc_ref, lg_ref, wa_ref, wx_ref, ba_ref, bx_ref, sp_ref, h0_ref, o_ref, ht_ref, carry_ref, *,
              rows, last_tile, last_row):
    t = pl.program_id(1)

    @pl.when(t == 0)
    def _init():
        carry_ref[...] = h0_ref[0]

    xc = xc_ref[...]
    xb = xc.astype(BF16)
    rg = _sigmoid(_dot(xb, wa_ref[...]) + ba_ref[...])
    ig = _sigmoid(_dot(xb, wx_ref[...]) + bx_ref[...])
    log_a = (-LRU_C) * rg * sp_ref[...]
    a = jnp.exp(log_a)
    b = jnp.sqrt(1.0 - jnp.exp(2.0 * log_a)) * (ig * xc)
    row = lax.broadcasted_iota(jnp.int32, a.shape, 0)
    d = 1
    while d < rows:
        a_prev = jnp.where(row >= d, pltpu.roll(a, d, 0), 1.0)
        b_prev = jnp.where(row >= d, pltpu.roll(b, d, 0), 0.0)
        b = a * b_prev + b
        a = a * a_prev
        d *= 2
    hs = b + a * carry_ref[...]
    carry_ref[...] = hs[rows - 1:rows]
    o_ref[...] = (hs * _gelu_tanh(lg_ref[...])).astype(o_ref.dtype)

    @pl.when(t == last_tile)
    def _state():
        ht_ref[0] = hs[last_row:last_row + 1]


def _lru(xc, lg, wa_bd, wx_bd, ba, bx, sp, h0, *, rows, t_valid, name):
    B, T, W = xc.shape
    nt = T // rows
    vec = lambda a: a.reshape(1, W)
    out, ht = pl.pallas_call(
        functools.partial(_lru_body, rows=rows, last_tile=(t_valid - 1) // rows, last_row=(t_valid - 1) % rows),
        grid=(B, nt),
        in_specs=[pl.BlockSpec((rows, W), lambda b, t: (b * nt + t, 0)),
                  pl.BlockSpec((rows, W), lambda b, t: (b * nt + t, 0)),
                  _resident((W, W)), _resident((W, W)), _resident((1, W)), _resident((1, W)), _resident((1, W)),
                  pl.BlockSpec((1, 1, W), lambda b, t: (b, 0, 0))],
        out_specs=[pl.BlockSpec((rows, W), lambda b, t: (b * nt + t, 0)),
                   pl.BlockSpec((1, 1, W), lambda b, t: (b, 0, 0))],
        out_shape=[jax.ShapeDtypeStruct((B * T, W), BF16), jax.ShapeDtypeStruct((B, 1, W), F32)],
        scratch_shapes=[pltpu.VMEM((1, W), F32)],
        compiler_params=_params("arbitrary", "arbitrary"),
        name=name,
    )(xc.reshape(B * T, W), lg.reshape(B * T, W), wa_bd, wx_bd, vec(ba), vec(bx), vec(sp), h0.reshape(B, 1, W))
    return out, ht.reshape(B, W)


def _gdn_chunk(q, k, v, gb, S, chunk):
    R = HEADS * chunk
    SD = HEADS * HEAD_DIM
    r0 = lax.broadcasted_iota(jnp.int32, (R, R), 0)
    r1 = lax.broadcasted_iota(jnp.int32, (R, R), 1)
    same = _group_id((R, R), 0, chunk) == _group_id((R, R), 1, chunk)
    low = same & (r0 >= r1)
    strict = same & (r0 > r1)
    upper_sel = jnp.where(same & (r0 <= r1), 1.0, 0.0).astype(BF16)
    block_sel = jnp.where(same, 1.0, 0.0).astype(BF16)
    eye = jnp.where(r0 == r1, 1.0, 0.0)

    g_rows = jnp.concatenate([gb, jnp.zeros((LANES - gb.shape[0], R), F32)], axis=0)
    cum = _dot_sel(g_rows, upper_sel)
    tot = _dot_sel(g_rows, block_sel)
    rid = lax.broadcasted_iota(jnp.int32, (LANES, R), 0)
    stacked = jnp.where(rid == 0, cum, jnp.where(rid == 1, g_rows, tot))
    cols = _sel_dot_nt(eye.astype(BF16), stacked)
    gcum_col, beta_col, gtot_col = cols[:, 0:1], cols[:, 1:2], cols[:, 2:3]
    gcum_row = cum[0:1, :]

    decay = jnp.where(low, jnp.exp(jnp.where(low, gcum_col - gcum_row, 0.0)), 0.0)
    kb = k.astype(BF16)
    kk = _dot_nt(kb, kb)
    qk = _dot_nt(q.astype(BF16), kb)
    nmat = jnp.where(strict, kk * decay * beta_col, 0.0)
    inv = eye - nmat
    pw = nmat
    width = 2
    while width < chunk:
        pb = pw.astype(BF16)
        pw = _dot(pb, pb)
        inv = inv + _dot(inv.astype(BF16), pw.astype(BF16))
        width *= 2
    ib = inv.astype(BF16)
    u = _dot(ib, (v * beta_col).astype(BF16))
    w = _dot(ib, (k * (beta_col * jnp.exp(gcum_col))).astype(BF16))

    head_match = _group_id((R, SD), 0, chunk) == _group_id((R, SD), 1, HEAD_DIM)

    def spread(x):
        return jnp.where(head_match, jnp.concatenate([x] * HEADS, axis=1), 0.0).astype(BF16)

    sb = S.astype(BF16)
    v_new = u - _dot(spread(w), sb)
    o = _dot(spread(q * jnp.exp(gcum_col)), sb) + _dot((qk * decay).astype(BF16), v_new.astype(BF16))
    kg = spread(k * jnp.exp(gtot_col - gcum_col))
    tok = jnp.bitwise_and(lax.broadcasted_iota(jnp.int32, (R, SD), 0), chunk - 1)
    first = jnp.where(head_match & (tok == 0), 1.0, 0.0).astype(BF16)
    c0, c1, c2 = _split3(cols)
    gtot_state = _dot_tn(first, c0) + (_dot_tn(first, c1) + _dot_tn(first, c2))
    s_new = S * jnp.exp(gtot_state[:, 2:3]) + _dot_tn(kg, v_new.astype(BF16))
    return o, s_new


def _gdn_body(q_ref, k_ref, v_ref, gb_ref, s0_ref, o_ref, sT_ref, s_ref, *, chunk, group):
    c = pl.program_id(1)

    @pl.when(c == 0)
    def _init():
        s_ref[...] = s0_ref[...]

    R = HEADS * chunk
    for g in range(group):
        o, s_new = _gdn_chunk(q_ref[g].reshape(R, HEAD_DIM), k_ref[g].reshape(R, HEAD_DIM),
                              v_ref[g].reshape(R, HEAD_DIM), gb_ref[g, 0], s_ref[g], chunk)
        o_ref[g] = o.reshape(HEADS, chunk, HEAD_DIM)
        s_ref[g] = s_new

    @pl.when(c == pl.num_programs(1) - 1)
    def _state():
        sT_ref[...] = s_ref[...]


def _gdn(q, k, v, gb, S0, *, chunk, group, name):
    B, H, T, hd = q.shape
    nc = T // chunk
    SD = H * hd
    tok = pl.BlockSpec((group, H, chunk, hd), lambda b, c: (b, 0, c, 0))
    st = pl.BlockSpec((group, SD, hd), lambda b, c: (b, 0, 0))
    return pl.pallas_call(
        functools.partial(_gdn_body, chunk=chunk, group=group),
        grid=(B // group, nc),
        in_specs=[tok, tok, tok, pl.BlockSpec((group, 1, 8, H * chunk), lambda b, c: (b, c, 0, 0)), st],
        out_specs=[tok, st],
        out_shape=[jax.ShapeDtypeStruct((B, H, T, hd), F32), jax.ShapeDtypeStruct((B, SD, hd), F32)],
        scratch_shapes=[pltpu.VMEM((group, SD, hd), F32)],
        compiler_params=_params("arbitrary", "arbitrary"),
        name=name,
    )(q, k, v, gb, S0)


def _merge_body(x_ref, b0_ref, b1_ref, b2_ref, b3_ref, gates_ref, wb_ref, wo_ref, o_ref):
    merged = None
    for n, b_ref in enumerate((b0_ref, b1_ref, b2_ref, b3_ref)):
        proj = _dot(b_ref[...], wb_ref[n])
        term = _sigmoid(gates_ref[:, n * D_MODEL:(n + 1) * D_MODEL]) * proj
        merged = term if merged is None else merged + term
    o_ref[...] = x_ref[...] + _dot(merged.astype(BF16), wo_ref[...])


def _merge(x2, branches, gates, wb, wo, *, row_tile, name):
    M, D = x2.shape
    row = lambda w: pl.BlockSpec((row_tile, w), lambda i: (i, 0))
    return pl.pallas_call(
        _merge_body,
        grid=(M // row_tile,),
        in_specs=[row(D)] + [row(BR_WIDTH)] * N_BRANCH + [row(N_BRANCH * D), _resident(wb.shape), _resident(wo.shape)],
        out_specs=row(D),
        out_shape=jax.ShapeDtypeStruct((M, D), F32),
        compiler_params=_params("arbitrary"),
        name=name,
    )(x2, *branches, gates, wb, wo)


def _mem_body(x_ref, g_ref, wq_ref, qg_ref, mk_ref, mv_ref, wo_ref, o_ref):
    x = x_ref[0]
    h = (x * _rms_scale(x) * g_ref[...]).astype(BF16)
    q = _dot(h, wq_ref[...])
    W = q.shape[1]
    head_sum = jnp.where(_group_id((W, W), 0, MEM_HD) == _group_id((W, W), 1, MEM_HD), 1.0, 0.0).astype(BF16)
    ms = _dot_sel(q * q, head_sum) * (1.0 / MEM_HD)
    qn = q * lax.rsqrt(ms + NORM_EPS) * qg_ref[...] * (1.0 / math.sqrt(MEM_HD))
    mk = mk_ref[0].astype(BF16)
    mv = mv_ref[0].astype(BF16)
    lane_head = _group_id(q.shape, 1, MEM_HD)
    o = jnp.zeros(q.shape, F32)
    for hh in range(MEM_HEADS):
        s = _dot_nt(jnp.where(lane_head == hh, qn, 0.0).astype(BF16), mk)
        p = jnp.exp(s - jnp.max(s, axis=1, keepdims=True))
        p = p / jnp.sum(p, axis=1, keepdims=True)
        o = o + jnp.where(lane_head == hh, _dot(p.astype(BF16), mv), 0.0)
    o_ref[0] = x + _dot(o.astype(BF16), wo_ref[...])


def _mem_sublayer(x, g, wq, qg, mk, mv, wo, *, row_tile, name):
    B, T, D = x.shape
    nt = T // row_tile
    kv = pl.BlockSpec((1,) + mk.shape[1:], lambda b, t: (b, 0, 0))
    return pl.pallas_call(
        _mem_body,
        grid=(B, nt),
        in_specs=[pl.BlockSpec((1, row_tile, D), lambda b, t: (b, t, 0)), _resident((1, D)), _resident(wq.shape),
                  _resident((1, MEM_WIDTH)), kv, kv, _resident(wo.shape)],
        out_specs=pl.BlockSpec((1, row_tile, D), lambda b, t: (b, t, 0)),
        out_shape=jax.ShapeDtypeStruct((B, T, D), F32),
        compiler_params=_params("arbitrary", "arbitrary"),
        name=name,
    )(x, g.reshape(1, D), wq, jnp.tile(qg, MEM_HEADS).reshape(1, MEM_WIDTH), mk, mv, wo)


def _mlp_body(x_ref, g_ref, wu_ref, wd_ref, o_ref, *, ff_chunk):
    x = x_ref[...]
    h = (x * _rms_scale(x) * g_ref[...]).astype(BF16)
    acc = x
    for c in range(D_FF // ff_chunk):
        u = _dot(h, wu_ref[:, c * ff_chunk:(c + 1) * ff_chunk])
        u = jnp.square(jnp.maximum(u, 0.0)).astype(BF16)
        acc = acc + _dot(u, wd_ref[c * ff_chunk:(c + 1) * ff_chunk, :])
    o_ref[...] = acc


def _mlp_sublayer(x2, g, w_up, w_down, *, row_tile, name):
    M, D = x2.shape
    return pl.pallas_call(
        functools.partial(_mlp_body, ff_chunk=1024),
        grid=(M // row_tile,),
        in_specs=[pl.BlockSpec((row_tile, D), lambda i: (i, 0)), _resident((1, D)), _resident(w_up.shape),
                  _resident(w_down.shape)],
        out_specs=pl.BlockSpec((row_tile, D), lambda i: (i, 0)),
        out_shape=jax.ShapeDtypeStruct((M, D), F32),
        compiler_params=_params("arbitrary"),
        name=name,
    )(x2, g.reshape(1, D), w_up, w_down)


def _rmsnorm(x, g):
    xf = x.astype(F32)
    return xf * lax.rsqrt(jnp.mean(xf * xf, axis=-1, keepdims=True) + NORM_EPS) * g.astype(F32)


def _l2norm(x):
    return x * lax.rsqrt(jnp.sum(x * x, axis=-1, keepdims=True) + NORM_EPS)


def _rope(x, pos):
    d = x.shape[-1]
    inv = ROPE_THETA ** (-jnp.arange(0, d, 2, dtype=F32) / d)
    ang = pos.astype(F32)[:, None] * inv[None, :]
    ang = ang.reshape((1, ang.shape[0]) + (1,) * (x.ndim - 3) + (d // 2,))
    cos, sin = jnp.cos(ang), jnp.sin(ang)
    x1, x2 = x[..., : d // 2], x[..., d // 2:]
    return jnp.concatenate([x1 * cos - x2 * sin, x2 * cos + x1 * sin], axis=-1)


def _causal_conv(x, buf, w, b=None):
    xp = jnp.concatenate([buf.astype(x.dtype), x], axis=1)
    T = x.shape[1]
    y = xp[:, 0:T] * w[0]
    for i in range(1, CONV_W):
        y = y + xp[:, i:i + T] * w[i]
    if b is not None:
        y = y + b
    return y, xp[:, -(CONV_W - 1):]


def _key_minor_pages(cache):
    nd = cache.ndim
    t = jnp.transpose(cache, (0, 1) + tuple(range(3, nd)) + (2,))
    return t.reshape(cache.shape[:2] + (-1, cache.shape[2]))


def _head_major(x):
    return jnp.swapaxes(x, 1, 2)


def _pad_time(x, t_pad, axis=1):
    pad = [(0, 0)] * x.ndim
    pad[axis] = (0, t_pad - x.shape[axis])
    return jnp.pad(x, pad)


def _pack_layer_weights(p):
    o = np.cumsum((0,) + (3 * BR_WIDTH, FOX_HEADS, LRU_WIDTH, LRU_WIDTH, 3 * BR_WIDTH, 3 * BR_WIDTH, GDN_HEADS,
                          GDN_HEADS, BR_WIDTH, N_BRANCH * D_MODEL)).tolist()
    wt = p['w_in'].T
    seg = lambda i: wt[o[i]:o[i + 1]]
    small = jnp.concatenate([seg(1), seg(6), seg(7), jnp.zeros((SMALL_COLS - 3 * HEADS, D_MODEL), wt.dtype)], axis=0)
    w_main = jnp.concatenate([seg(0), seg(2), seg(3), seg(4), seg(5), seg(8), small], axis=0).astype(BF16)
    blockdiag = lambda wb: jax.scipy.linalg.block_diag(*[wb[i] for i in range(LRU_BLOCKS)]).astype(BF16)
    return dict(w_main=w_main, w_gates=seg(9).astype(BF16), lru_wa=blockdiag(p['lru_w_a']),
                lru_wx=blockdiag(p['lru_w_x']), w_branch=p['w_branch'].astype(BF16), w_out=p['w_out'].astype(BF16),
                w_mem_q=p['w_mem_q'].astype(BF16), w_mem_kv=p['w_mem_kv'].astype(BF16),
                w_mem_o=p['w_mem_o'].astype(BF16), w_mlp_up=p['w_mlp_up'].astype(BF16),
                w_mlp_down=p['w_mlp_down'].astype(BF16))


_C_FOX, _C_LRUX, _C_LRUG, _C_DIFF, _C_GDN, _C_GDNZ, _C_SMALL = 0, 768, 1024, 1280, 2048, 2816, 3072


def _mixer_sublayer(x, pos, p, pw, lam_init, fox_past, diff_past, lru_h0, lru_buf, gdn_S0, gdn_buf, *, row_tile,
                    attn_tile, tag, attn_chunk=128):
    B, T, D = x.shape
    M = B * T
    x2 = x.reshape(M, D)
    zm, gates = _norm_matmul(x2, p['norm_mix'], [pw['w_main'], pw['w_gates']], row_tile=row_tile, name=tag + "in_proj",
                             transposed=True)
    zm = zm.reshape(B, T, MAIN_COLS)
    seg = lambda c0, w: zm[..., c0:c0 + w]

    r = seg(_C_FOX, 3 * BR_WIDTH).reshape(B, T, 3, FOX_HEADS, FOX_HD)
    fq = _rmsnorm(r[:, :, 0], p['fox_q_norm'])
    fk = _rmsnorm(r[:, :, 1], p['fox_k_norm'])
    fv = r[:, :, 2]
    flogf = jax.nn.log_sigmoid(seg(_C_SMALL, FOX_HEADS) + p['fox_b_f'])
    if fox_past is None:
        c_parts = jnp.stack(_split3_trunc(jnp.cumsum(flogf, axis=1)), axis=-1)
        ones = jnp.ones_like(c_parts)
        kx = jnp.concatenate([fk.astype(BF16), c_parts, ones], axis=-1)
        qx = jnp.concatenate([(fq * (1.0 / math.sqrt(FOX_HD))).astype(BF16), -ones, c_parts], axis=-1)
        fox_out = _flash_attention_t(jnp.transpose(_pad_lanes(qx), (0, 2, 3, 1)), _head_major(_pad_lanes(kx)),
                                     jnp.transpose(fv.astype(BF16), (0, 2, 3, 1)), tq=attn_tile, tk=attn_chunk,
                                     name=tag + "fox_attn")
    else:
        fo = _paged_attention(_query_rows(fq[:, :, :, None], 1), fox_past['k'], fox_past['v'],
                              _new_page(fk.reshape(B, T, BR_WIDTH)), _new_page(fv.reshape(B, T, BR_WIDTH)),
                              fox_past['page_table'], fox_past['layer'], pages_per_step=fox_past['pages_per_step'],
                              n_maps=1, scale=1.0 / math.sqrt(FOX_HD), lf_cache_t=fox_past['logf'],
                              lfnew_t=_new_page(flogf), name=tag + "fox_paged")
        fox_out = fo[:, :T].reshape(M, BR_WIDTH).astype(BF16)

    xc, lru_buf_new = _causal_conv(seg(_C_LRUX, LRU_WIDTH), lru_buf, p['lru_conv_w'], p['lru_conv_b'])
    lru_rows = min(256, T) if T % 8 == 0 else 8
    t_pad = -(-T // lru_rows) * lru_rows
    lru_out, lru_hT = _lru(_pad_time(xc, t_pad), _pad_time(seg(_C_LRUG, LRU_WIDTH), t_pad), pw['lru_wa'], pw['lru_wx'],
                           p['lru_b_a'], p['lru_b_x'], jax.nn.softplus(-p['lru_lambda']), lru_h0,
                           rows=lru_rows, t_valid=T, name=tag + "lru")
    if t_pad != T:
        lru_out = lru_out.reshape(B, t_pad, LRU_WIDTH)[:, :T].reshape(M, LRU_WIDTH)

    r = seg(_C_DIFF, 3 * BR_WIDTH).reshape(B, T, 3, DIFF_HEADS, DIFF_VD)
    dq = _rope(_rmsnorm(r[:, :, 0].reshape(B, T, DIFF_HEADS, 2, DIFF_HD), p['diff_q_norm']), pos)
    dk = _rope(_rmsnorm(r[:, :, 1].reshape(B, T, DIFF_HEADS, 2, DIFF_HD), p['diff_k_norm']), pos)
    dvv = r[:, :, 2]
    lp = p['diff_lambda']
    lam = jnp.exp(jnp.sum(lp[0] * lp[1])) - jnp.exp(jnp.sum(lp[2] * lp[3])) + lam_init
    if diff_past is None:
        qsel = jnp.einsum('bthmd,mn->bthmnd', dq * (1.0 / math.sqrt(DIFF_HD)), jnp.eye(2, dtype=F32))
        qsel = qsel.reshape(B, T, DIFF_HEADS * 2, DIFF_VD).astype(BF16)
        diff_out = _flash_attention_t(jnp.transpose(_pad_lanes(qsel), (0, 2, 3, 1)),
                                      _head_major(_pad_lanes(dk.reshape(B, T, DIFF_HEADS, DIFF_VD).astype(BF16))),
                                      jnp.transpose(dvv.astype(BF16), (0, 2, 3, 1)), tq=attn_tile, tk=attn_chunk,
                                      lam=lam, gsub=p['diff_sub_norm'], sub_scale=1.0 - lam_init,
                                      name=tag + "diff_attn")
    else:
        do = _paged_attention(_query_rows(dq, 2), diff_past['k'], diff_past['v'],
                              _new_page(dk.reshape(B, T, BR_WIDTH)), _new_page(dvv.reshape(B, T, BR_WIDTH)),
                              diff_past['page_table'], diff_past['layer'], pages_per_step=diff_past['pages_per_step'],
                              n_maps=2, scale=1.0 / math.sqrt(DIFF_HD), lam=lam, gsub=p['diff_sub_norm'],
                              sub_scale=1.0 - lam_init, name=tag + "diff_paged")
        diff_out = do[:, :T].reshape(M, BR_WIDTH).astype(BF16)

    gc, gdn_buf_new = _causal_conv(seg(_C_GDN, 3 * BR_WIDTH), gdn_buf, p['gdn_conv_w'])
    gc = jax.nn.silu(gc).reshape(B, T, 3, GDN_HEADS, GDN_HD)
    gq = _l2norm(gc[:, :, 0]) * (GDN_HD ** -0.5)
    gk = _l2norm(gc[:, :, 1])
    gvv = gc[:, :, 2]
    gbeta = jax.nn.sigmoid(seg(_C_SMALL + 2 * HEADS, GDN_HEADS))
    gg = -jnp.exp(p['gdn_A_log']) * jax.nn.softplus(seg(_C_SMALL + HEADS, GDN_HEADS) + p['gdn_dt_bias'])
    chunk = 64 if T % 64 == 0 else 32
    tg = -(-T // chunk) * chunk
    nc = tg // chunk
    rows_hc = lambda a: jnp.swapaxes(_pad_time(a, tg).reshape(B, nc, chunk, GDN_HEADS), 2, 3).reshape(B, nc, 1, GDN_HEADS * chunk)
    g_r, b_r = rows_hc(gg), rows_hc(gbeta)
    gb = jnp.concatenate([g_r, b_r, g_r, jnp.zeros((B, nc, 5, GDN_HEADS * chunk), F32)], axis=2)
    hm = lambda a: _head_major(_pad_time(a, tg))
    go, gS = _gdn(hm(gq), hm(gk), hm(gvv), gb, gdn_S0.reshape(B, GDN_HEADS * GDN_HD, GDN_HD), chunk=chunk,
                  group=math.gcd(B, 4), name=tag + "gdn")
    go = _head_major(go)[:, :T]
    go = _rmsnorm(go, p['gdn_out_norm']) * jax.nn.silu(seg(_C_GDNZ, BR_WIDTH).reshape(B, T, GDN_HEADS, GDN_HD))
    gdn_out = go.reshape(M, BR_WIDTH).astype(BF16)

    x2 = _merge(x2, (fox_out, lru_out, diff_out, gdn_out), gates, pw['w_branch'], pw['w_out'], row_tile=row_tile,
                name=tag + "merge")
    state = (fk, fv, flogf, dk, dvv, lru_hT, lru_buf_new, gS.reshape(B, GDN_HEADS, GDN_HD, GDN_HD), gdn_buf_new)
    return x2.reshape(B, T, D), state


def _memory_kv(mem, p, pw):
    B, Tm, D = mem.shape
    kv, = _norm_matmul(mem.reshape(B * Tm, D), p['norm_mem_src'], [pw['w_mem_kv']], row_tile=256, name="mem_kv")
    kv = kv.reshape(B, Tm, 2, MEM_HEADS, MEM_HD)
    return _rmsnorm(kv[:, :, 0], p['mem_k_norm']), kv[:, :, 1]


def _memory_sublayer(x, mk, mv, p, pw, *, row_tile, name):
    B, T, D = x.shape
    t_pad = -(-T // row_tile) * row_tile
    flat = lambda a: a.reshape(a.shape[0], a.shape[1], MEM_WIDTH)
    out = _mem_sublayer(_pad_time(x, t_pad), p['norm_mem'], pw['w_mem_q'], p['mem_q_norm'], flat(mk), flat(mv),
                        pw['w_mem_o'], row_tile=row_tile, name=name)
    return out[:, :T]


def kernel(x_prompt, x_sample, cache_fox_k, cache_fox_v, cache_fox_logf, cache_diff_k, cache_diff_v, cache_mem_k, cache_mem_v, state_lru_h, state_lru_conv, state_gdn_S, state_gdn_conv, page_table, mem_prompt, norm_mix, w_in, fox_b_f, fox_q_norm, fox_k_norm, lru_conv_w, lru_conv_b, lru_w_a, lru_b_a, lru_w_x, lru_b_x, lru_lambda, diff_q_norm, diff_k_norm, diff_lambda, diff_sub_norm, gdn_conv_w, gdn_A_log, gdn_dt_bias, gdn_out_norm, w_branch, w_out, norm_mem, norm_mem_src, w_mem_q, w_mem_kv, mem_q_norm, mem_k_norm, w_mem_o, norm_mlp, w_mlp_up, w_mlp_down):
    stacked = dict(norm_mix=norm_mix, w_in=w_in, fox_b_f=fox_b_f, fox_q_norm=fox_q_norm, fox_k_norm=fox_k_norm,
                   lru_conv_w=lru_conv_w, lru_conv_b=lru_conv_b, lru_w_a=lru_w_a, lru_b_a=lru_b_a,
                   lru_w_x=lru_w_x, lru_b_x=lru_b_x, lru_lambda=lru_lambda, diff_q_norm=diff_q_norm,
                   diff_k_norm=diff_k_norm, diff_lambda=diff_lambda, diff_sub_norm=diff_sub_norm,
                   gdn_conv_w=gdn_conv_w, gdn_A_log=gdn_A_log, gdn_dt_bias=gdn_dt_bias, gdn_out_norm=gdn_out_norm,
                   w_branch=w_branch, w_out=w_out, norm_mem=norm_mem, norm_mem_src=norm_mem_src, w_mem_q=w_mem_q,
                   w_mem_kv=w_mem_kv, mem_q_norm=mem_q_norm, mem_k_norm=mem_k_norm, w_mem_o=w_mem_o,
                   norm_mlp=norm_mlp, w_mlp_up=w_mlp_up, w_mlp_down=w_mlp_down)
    Bp, Tp, D = x_prompt.shape
    Bs, Ts, _ = x_sample.shape
    past_len = page_table.shape[1] * cache_fox_k.shape[2]
    pos_p = jnp.arange(Tp)
    pos_s = past_len + jnp.arange(Ts)
    h0_p = jnp.zeros((Bp, LRU_WIDTH), F32)
    lbuf_p = jnp.zeros((Bp, CONV_W - 1, LRU_WIDTH), F32)
    S0_p = jnp.zeros((Bp, GDN_HEADS, GDN_HD, GDN_HD), F32)
    gbuf_p = jnp.zeros((Bp, CONV_W - 1, 3 * BR_WIDTH), F32)

    fox_k_t, fox_v_t, diff_k_t, diff_v_t = (_key_minor_pages(c) for c in (cache_fox_k, cache_fox_v, cache_diff_k,
                                                                      cache_diff_v))
    fox_logf_t = jnp.swapaxes(cache_fox_logf, 2, 3)

    xp, xs = x_prompt, x_sample
    pst = [[] for _ in range(11)]
    sst = [[] for _ in range(9)]
    for l in range(DEPTH):
        p = {name: arr[l] for name, arr in stacked.items()}
        pw = _pack_layer_weights(p)
        lam_init = 0.8 - 0.6 * math.exp(-0.3 * l)
        xp, st_p = _mixer_sublayer(xp, pos_p, p, pw, lam_init, None, None, h0_p, lbuf_p, S0_p, gbuf_p,
                                   row_tile=256, attn_tile=512, attn_chunk=512, tag="p_")
        mk, mv = _memory_kv(mem_prompt, p, pw)
        xp = _memory_sublayer(xp, mk, mv, p, pw, row_tile=512, name="p_mem")
        xp = _mlp_sublayer(xp.reshape(Bp * Tp, D), p['norm_mlp'], pw['w_mlp_up'], pw['w_mlp_down'], row_tile=512,
                           name="p_mlp").reshape(Bp, Tp, D)
        for lst, val in zip(pst, (st_p[0], st_p[1], st_p[2], st_p[3], st_p[4], mk, mv,
                                  st_p[5], st_p[6], st_p[7], st_p[8])):
            lst.append(val)
        fox_past = dict(k=fox_k_t, v=fox_v_t, logf=fox_logf_t, page_table=page_table, layer=l, pages_per_step=16)
        diff_past = dict(k=diff_k_t, v=diff_v_t, page_table=page_table, layer=l, pages_per_step=16)
        xs, st_s = _mixer_sublayer(xs, pos_s, p, pw, lam_init, fox_past, diff_past, state_lru_h[l], state_lru_conv[l],
                                   state_gdn_S[l], state_gdn_conv[l], row_tile=Bs * Ts, attn_tile=None, tag="s_")
        xs = _memory_sublayer(xs, cache_mem_k[l], cache_mem_v[l], p, pw, row_tile=8, name="s_mem")
        xs = _mlp_sublayer(xs.reshape(Bs * Ts, D), p['norm_mlp'], pw['w_mlp_up'], pw['w_mlp_down'], row_tile=Bs * Ts,
                           name="s_mlp").reshape(Bs, Ts, D)
        for lst, val in zip(sst, st_s):
            lst.append(val)

    return tuple([xp, xs] + [jnp.stack(v_, axis=0) for v_ in pst] + [jnp.stack(v_, axis=0) for v_ in sst])
```

```python
import functools
import math

import jax
import jax.numpy as jnp
import numpy as np
from jax import lax
from jax.experimental import pallas as pl
from jax.experimental.pallas import tpu as pltpu

D_MODEL = 1024
DEPTH = 2
N_BRANCH = 4
BR_WIDTH = D_MODEL // 4
FOX_HEADS = 4
FOX_HD = BR_WIDTH // FOX_HEADS
LRU_WIDTH = BR_WIDTH
LRU_BLOCKS = 4
LRU_BD = LRU_WIDTH // LRU_BLOCKS
LRU_C = 8.0
CONV_W = 4
DIFF_HEADS = 4
DIFF_VD = BR_WIDTH // DIFF_HEADS
DIFF_HD = DIFF_VD // 2
GDN_HEADS = 4
GDN_HD = BR_WIDTH // GDN_HEADS
MEM_HEADS = 4
MEM_HD = 64
MEM_WIDTH = MEM_HEADS * MEM_HD
D_FF = 4 * D_MODEL
ROPE_THETA = 10000.0
Q_BLOCK = 128
NORM_EPS = 1e-6
HEADS = 4
HEAD_DIM = 64
SMALL_COLS = 128
MAIN_COLS = 12 * BR_WIDTH + SMALL_COLS

F32 = jnp.float32
BF16 = jnp.bfloat16
NEG_BIG = -1e30
LANES = 128
VMEM_LIMIT_BYTES = 56 * 1024 * 1024


def _params(*sem):
    return pltpu.CompilerParams(dimension_semantics=sem, vmem_limit_bytes=VMEM_LIMIT_BYTES)


def _resident(shape):
    nd = len(shape)
    return pl.BlockSpec(shape, lambda *_: (0,) * nd, pipeline_mode=pl.Buffered(1))


def _dot(a, b):
    return jnp.dot(a, b, preferred_element_type=F32)


def _dot_nt(a, b):
    return lax.dot_general(a, b, (((1,), (1,)), ((), ())), preferred_element_type=F32)


def _dot_tn(a, b):
    return lax.dot_general(a, b, (((0,), (0,)), ((), ())), preferred_element_type=F32)


def _split3(x):
    p0 = x.astype(BF16)
    r = x - p0.astype(F32)
    p1 = r.astype(BF16)
    p2 = (r - p1.astype(F32)).astype(BF16)
    return p0, p1, p2


def _split3_trunc(x):
    def top(v):
        return lax.bitcast_convert_type(lax.bitcast_convert_type(v, jnp.uint32) & jnp.uint32(0xFFFF0000), F32)
    p0 = top(x)
    r = x - p0
    p1 = top(r)
    p2 = top(r - p1)
    return p0.astype(BF16), p1.astype(BF16), p2.astype(BF16)


def _dot_sel(x, sel_bf16):
    p0, p1, p2 = _split3(x)
    return _dot(p0, sel_bf16) + (_dot(p1, sel_bf16) + _dot(p2, sel_bf16))


def _sel_dot_nt(sel_bf16, x):
    p0, p1, p2 = _split3(x)
    return _dot_nt(sel_bf16, p0) + (_dot_nt(sel_bf16, p1) + _dot_nt(sel_bf16, p2))


def _sel_dot(sel_bf16, x):
    p0, p1, p2 = _split3(x)
    return _dot(sel_bf16, p0) + (_dot(sel_bf16, p1) + _dot(sel_bf16, p2))


def _split2(x):
    hi = x.astype(BF16)
    return hi, (x - hi.astype(F32)).astype(BF16)


def _dot3(ah, al, bh, bl):
    return _dot(ah, bh) + (_dot(al, bh) + _dot(ah, bl))


def _group_id(shape, axis, size):
    return lax.shift_right_logical(lax.broadcasted_iota(jnp.int32, shape, axis), int(math.log2(size)))


def _rms_scale(x):
    return lax.rsqrt(jnp.mean(x * x, axis=-1, keepdims=True) + NORM_EPS)


def _sigmoid(x):
    return 1.0 / (1.0 + jnp.exp(-x))


def _norm_matmul_body(x_ref, g_ref, *refs, n_out, col_chunk, transposed):
    w_refs, o_refs = refs[:n_out], refs[n_out:]
    x = x_ref[...]
    h = (x * _rms_scale(x) * g_ref[...]).astype(BF16)
    for w_ref, o_ref in zip(w_refs, o_refs):
        n = w_ref.shape[0 if transposed else 1]
        for c0 in range(0, n, col_chunk):
            c1 = min(n, c0 + col_chunk)
            o_ref[:, c0:c1] = _dot_nt(h, w_ref[c0:c1, :]) if transposed else _dot(h, w_ref[:, c0:c1])


def _norm_matmul(x2, g, ws, *, row_tile, name, transposed=False):
    M, D = x2.shape
    widths = [w.shape[0 if transposed else 1] for w in ws]
    return pl.pallas_call(
        functools.partial(_norm_matmul_body, n_out=len(ws), col_chunk=1024, transposed=transposed),
        grid=(M // row_tile,),
        in_specs=[pl.BlockSpec((row_tile, D), lambda i: (i, 0)), _resident((1, D))] + [_resident(w.shape) for w in ws],
        out_specs=[pl.BlockSpec((row_tile, n), lambda i: (i, 0)) for n in widths],
        out_shape=[jax.ShapeDtypeStruct((M, n), F32) for n in widths],
        compiler_params=_params("arbitrary"),
        name=name,
    )(x2, g.reshape(1, D), *ws)


def _flash_t_body(*refs, n_maps, use_bias, tq, tk, group, sub_scale):
    refs = list(refs)
    qt_ref, k_ref, vt_ref = refs[:3]
    ck_ref = refs[3] if use_bias else None
    lam_ref, gsub_ref = (refs[-3], refs[-2]) if n_maps > 1 else (None, None)
    o_ref = refs[-1]
    i = pl.program_id(1)
    q0 = i * tq
    n_full = q0 // tk
    n_diag = tq // tk
    C = HEADS * HEAD_DIM
    width = HEAD_DIM // n_maps

    def step(pairs, qts, j, carries, masked):
        start = pl.multiple_of(j * tk, tk)
        kj = k_ref[pl.ds(start, tk), :]
        if masked:
            kpos = start + lax.broadcasted_iota(jnp.int32, (tk, tq), 0)
            qpos = q0 + lax.broadcasted_iota(jnp.int32, (tk, tq), 1)
            keep = kpos <= qpos
        if use_bias:
            ck = ck_ref[0, pl.ds(start, tk), :]
        new = []
        for (h, _), qt, (m, l, acc) in zip(pairs, qts, carries):
            st = _dot(kj, qt)
            if use_bias:
                st = st - ck[:, h:h + 1]
            if masked:
                st = jnp.where(keep, st, NEG_BIG)
            m_new = jnp.maximum(m, jnp.max(st, axis=0, keepdims=True))
            p = jnp.exp(st - m_new)
            alpha = jnp.exp(m - m_new)
            l = alpha * l + jnp.sum(p, axis=0, keepdims=True)
            vt = vt_ref[0, h * HEAD_DIM:(h + 1) * HEAD_DIM, pl.ds(start, tk)]
            acc = alpha * acc + _dot(vt, p.astype(BF16))
            new.append((m_new, l, acc))
        return tuple(new)

    all_pairs = [(h, mp) for h in range(HEADS) for mp in range(n_maps)]
    qt_all = qt_ref[0]
    row_chain = _group_id((C, tq), 0, width)
    normalized = {}
    for g0 in range(0, len(all_pairs), group):
        pairs = all_pairs[g0:g0 + group]
        qts = [jnp.where(row_chain == h * n_maps + mp, qt_all, jnp.zeros_like(qt_all)) for h, mp in pairs]
        init = tuple((jnp.full((1, tq), NEG_BIG, F32), jnp.zeros((1, tq), F32), jnp.zeros((HEAD_DIM, tq), F32))
                     for _ in pairs)
        carries = lax.fori_loop(0, n_full, lambda j, c: step(pairs, qts, j, c, False), init)
        for dj in range(n_diag):
            carries = step(pairs, qts, n_full + dj, carries, True)
        for pair, (_, l, acc) in zip(pairs, carries):
            normalized[pair] = acc / l
    outs = []
    for h in range(HEADS):
        if n_maps == 1:
            outs.append(normalized[(h, 0)])
        else:
            d = normalized[(h, 0)] - lam_ref[...] * normalized[(h, 1)]
            ms = jnp.mean(d * d, axis=0, keepdims=True)
            outs.append(d * lax.rsqrt(ms + NORM_EPS) * gsub_ref[...] * sub_scale)
    o_ref[...] = jnp.concatenate(outs, axis=0).T.astype(o_ref.dtype)


def _flash_attention_t(qt, k, vt, *, n_maps, tq, tk, group=4, ck=None, lam=None, gsub=None, sub_scale=1.0, name):
    B, C, T = qt.shape
    nq = T // tq
    operands = [qt, k, vt]
    in_specs = [pl.BlockSpec((1, C, tq), lambda b, i: (b, 0, i)),
                pl.BlockSpec((T, C), lambda b, i: (b, 0)),
                pl.BlockSpec((1, C, T), lambda b, i: (b, 0, 0))]
    if ck is not None:
        operands.append(ck)
        in_specs.append(pl.BlockSpec((1, T, HEADS), lambda b, i: (b, 0, 0)))
    if n_maps > 1:
        operands += [lam.reshape(1, 1), gsub.reshape(HEAD_DIM, 1)]
        in_specs += [_resident((1, 1)), _resident((HEAD_DIM, 1))]
    return pl.pallas_call(
        functools.partial(_flash_t_body, n_maps=n_maps, use_bias=ck is not None, tq=tq, tk=tk, group=group,
                          sub_scale=sub_scale),
        grid=(B, nq),
        in_specs=in_specs,
        out_specs=pl.BlockSpec((tq, C), lambda b, i: (b * nq + i, 0)),
        out_shape=jax.ShapeDtypeStruct((B * T, C), BF16),
        compiler_params=_params("arbitrary", "arbitrary"),
        name=name,
    )(*operands)


def _pad_lanes(x, width=LANES):
    return jnp.pad(x, [(0, 0)] * (x.ndim - 1) + [(0, width - x.shape[-1])])


Q_ROWS = 8


def _paged_body(pt_ref, *refs, n_pages, n_maps, use_bias, scale, sub_scale):
    del pt_ref
    refs = list(refs)
    q_ref = refs.pop(0)
    lam_ref, gsub_ref = (None, None) if use_bias else (refs.pop(0), refs.pop(0))
    knew_ref, vnew_ref = refs.pop(0), refs.pop(0)
    lfnew_ref = refs.pop(0) if use_bias else None
    k_refs, refs = refs[:n_pages], refs[n_pages:]
    v_refs, refs = refs[:n_pages], refs[n_pages:]
    lf_refs, refs = (refs[:n_pages], refs[n_pages:]) if use_bias else (None, refs)
    o_ref, m_ref, l_ref, acc_ref = refs[:4]
    carry_ref = refs[4] if use_bias else None
    g = pl.program_id(1)
    R = HEADS * n_maps * Q_ROWS
    C = HEADS * HEAD_DIM

    @pl.when(g == 0)
    def _init():
        m_ref[...] = jnp.full(m_ref.shape, NEG_BIG, F32)
        l_ref[...] = jnp.zeros(l_ref.shape, F32)
        acc_ref[...] = jnp.zeros(acc_ref.shape, F32)
        if use_bias:
            carry_ref[...] = jnp.zeros(carry_ref.shape, F32)

    q = q_ref[0]

    def head_rows(x):
        return jnp.concatenate([jnp.broadcast_to(x[h:h + 1], (Q_ROWS, x.shape[1])) for h in range(HEADS)], axis=0)

    def attend(kts, vts, lfs, causal_new):
        kt = jnp.concatenate(kts, axis=1).astype(BF16)
        s = _dot(q, kt) * scale
        if use_bias:
            upper = jnp.where(lax.broadcasted_iota(jnp.int32, (LANES, LANES), 0)
                              <= lax.broadcasted_iota(jnp.int32, (LANES, LANES), 1), 1.0, 0.0).astype(BF16)
            rows = HEADS * Q_ROWS
            n = len(lfs)
            within = _dot_sel(jnp.concatenate([head_rows(lf) for lf in lfs], axis=0), upper)
            incl = jnp.broadcast_to(within[:, LANES - 1:LANES], within.shape)
            d = 1
            while d < n:
                incl = incl + jnp.concatenate([jnp.zeros((d * rows, LANES), F32), incl[:(n - d) * rows]], axis=0)
                d *= 2
            carry = carry_ref[...]
            before = jnp.concatenate([jnp.zeros((rows, LANES), F32), incl[:(n - 1) * rows]], axis=0) if n > 1 else 0.0
            c_all = within + before
            carry_ref[...] = carry + incl[(n - 1) * rows:]
            s = s - (jnp.concatenate([c_all[i * rows:(i + 1) * rows] for i in range(n)], axis=1)
                     + jnp.concatenate([carry] * n, axis=1))
        if causal_new:
            t = jnp.bitwise_and(lax.broadcasted_iota(jnp.int32, s.shape, 0), Q_ROWS - 1)
            s = jnp.where(lax.broadcasted_iota(jnp.int32, s.shape, 1) <= t, s, NEG_BIG)
        m_prev = m_ref[...]
        m_new = jnp.maximum(m_prev, jnp.max(s, axis=1, keepdims=True))
        alpha = jnp.exp(m_prev - m_new)
        p = jnp.exp(s - m_new)
        l_ref[...] = alpha * l_ref[...] + jnp.sum(p, axis=1, keepdims=True)
        vt = jnp.concatenate(vts, axis=1).astype(BF16)
        acc_ref[...] = alpha * acc_ref[...] + _dot_nt(p.astype(BF16), vt)
        m_ref[...] = m_new

    attend([r[0, 0] for r in k_refs], [r[0, 0] for r in v_refs], [r[0, 0] for r in lf_refs] if use_bias else None, False)

    @pl.when(g == pl.num_programs(1) - 1)
    def _finish():
        attend([knew_ref[0]], [vnew_ref[0]], [lfnew_ref[0]] if use_bias else None, True)
        full = acc_ref[...] / l_ref[...]
        lane_head = _group_id((Q_ROWS, C), 1, HEAD_DIM)
        outs = []
        for mp in range(n_maps):
            o = jnp.zeros((Q_ROWS, C), F32)
            for h in range(HEADS):
                r0 = (h * n_maps + mp) * Q_ROWS
                o = o + jnp.where(lane_head == h, full[r0:r0 + Q_ROWS], 0.0)
            outs.append(o)
        if n_maps == 1:
            res = outs[0]
        else:
            d = outs[0] - lam_ref[...] * outs[1]
            head_sum = jnp.where(_group_id((C, C), 0, HEAD_DIM) == _group_id((C, C), 1, HEAD_DIM), 1.0, 0.0).astype(BF16)
            ms = _dot_sel(d * d, head_sum) * (1.0 / HEAD_DIM)
            res = d * lax.rsqrt(ms + NORM_EPS) * gsub_ref[...] * sub_scale
        o_ref[0] = res.astype(o_ref.dtype)


def _paged_attention(q_rows, k_cache_t, v_cache_t, knew_t, vnew_t, page_table, layer, *, pages_per_step, n_maps, scale,
                     lf_cache_t=None, lfnew_t=None, lam=None, gsub=None, sub_scale=1.0, name):
    B, R, C = q_rows.shape
    n_pages_total = page_table.shape[1]
    P = pages_per_step
    use_bias = lf_cache_t is not None
    per_seq = lambda shape: pl.BlockSpec((1,) + shape, lambda b, g, pt: (b,) + (0,) * len(shape))
    const = lambda shape: pl.BlockSpec(shape, lambda b, g, pt: (0,) * len(shape))

    def page_spec(i, tail):
        return pl.BlockSpec((1, 1) + tail, lambda b, g, pt: (layer, pt[b, g * P + i]) + (0,) * len(tail))

    operands = [q_rows]
    in_specs = [per_seq((R, C))]
    if not use_bias:
        operands += [lam.reshape(1, 1), jnp.tile(gsub, HEADS).reshape(1, C)]
        in_specs += [const((1, 1)), const((1, C))]
    operands += [knew_t, vnew_t]
    in_specs += [per_seq((C, LANES)), per_seq((C, LANES))]
    if use_bias:
        operands.append(lfnew_t)
        in_specs.append(per_seq((HEADS, LANES)))
    kv_tail = (C, LANES)
    operands += [k_cache_t] * P + [v_cache_t] * P
    in_specs += [page_spec(i, kv_tail) for i in range(P)] + [page_spec(i, kv_tail) for i in range(P)]
    scratch = [pltpu.VMEM((R, 1), F32), pltpu.VMEM((R, 1), F32), pltpu.VMEM((R, C), F32)]
    if use_bias:
        operands += [lf_cache_t] * P
        in_specs += [page_spec(i, (HEADS, LANES)) for i in range(P)]
        scratch.append(pltpu.VMEM((HEADS * Q_ROWS, LANES), F32))
    return pl.pallas_call(
        functools.partial(_paged_body, n_pages=P, n_maps=n_maps, use_bias=use_bias, scale=scale, sub_scale=sub_scale),
        grid_spec=pltpu.PrefetchScalarGridSpec(
            num_scalar_prefetch=1, grid=(B, n_pages_total // P), in_specs=in_specs,
            out_specs=pl.BlockSpec((1, Q_ROWS, C), lambda b, g, pt: (b, 0, 0)), scratch_shapes=scratch),
        out_shape=jax.ShapeDtypeStruct((B, Q_ROWS, C), F32),
        compiler_params=_params("arbitrary", "arbitrary"),
        name=name,
    )(page_table, *operands)


def _query_rows(q, n_maps):
    B, T, H, M, d = q.shape
    eye = jnp.eye(H * M, dtype=q.dtype).reshape(H, M, H, M)
    rows = jnp.einsum('bthmd,hmgn->bhmtgnd', q, eye)
    rows = jnp.pad(rows, ((0, 0), (0, 0), (0, 0), (0, Q_ROWS - T), (0, 0), (0, 0), (0, 0)))
    return rows.reshape(B, H * M * Q_ROWS, H * M * d).astype(BF16)


def _new_page(x):
    return jnp.pad(jnp.swapaxes(x, 1, 2), ((0, 0), (0, 0), (0, LANES - x.shape[1])))


def _gelu_tanh(x):
    return 0.5 * x * (1.0 + jnp.tanh(math.sqrt(2.0 / math.pi) * (x + 0.044715 * (x * x * x))))


def _lru_body(xc_ref, lg_ref, wa_ref, wx_ref, ba_ref, bx_ref, sp_ref, h0_ref, o_ref, ht_ref, carry_ref, *,
              rows, last_tile, last_row):
    t = pl.program_id(1)

    @pl.when(t == 0)
    def _init():
        carry_ref[...] = h0_ref[0]

    xc = xc_ref[...]
    xb = xc.astype(BF16)
    rg = _sigmoid(_dot(xb, wa_ref[...]) + ba_ref[...])
    ig = _sigmoid(_dot(xb, wx_ref[...]) + bx_ref[...])
    log_a = (-LRU_C) * rg * sp_ref[...]
    a = jnp.exp(log_a)
    b = jnp.sqrt(1.0 - jnp.exp(2.0 * log_a)) * (ig * xc)
    row = lax.broadcasted_iota(jnp.int32, a.shape, 0)
    d = 1
    while d < rows:
        a_prev = jnp.where(row >= d, pltpu.roll(a, d, 0), 1.0)
        b_prev = jnp.where(row >= d, pltpu.roll(b, d, 0), 0.0)
        b = a * b_prev + b
        a = a * a_prev
        d *= 2
    hs = b + a * carry_ref[...]
    carry_ref[...] = hs[rows - 1:rows]
    o_ref[...] = (hs * _gelu_tanh(lg_ref[...])).astype(o_ref.dtype)

    @pl.when(t == last_tile)
    def _state():
        ht_ref[0] = hs[last_row:last_row + 1]


def _lru(xc, lg, wa_bd, wx_bd, ba, bx, sp, h0, *, rows, t_valid, name):
    B, T, W = xc.shape
    nt = T // rows
    vec = lambda a: a.reshape(1, W)
    out, ht = pl.pallas_call(
        functools.partial(_lru_body, rows=rows, last_tile=(t_valid - 1) // rows, last_row=(t_valid - 1) % rows),
        grid=(B, nt),
        in_specs=[pl.BlockSpec((rows, W), lambda b, t: (b * nt + t, 0)),
                  pl.BlockSpec((rows, W), lambda b, t: (b * nt + t, 0)),
                  _resident((W, W)), _resident((W, W)), _resident((1, W)), _resident((1, W)), _resident((1, W)),
                  pl.BlockSpec((1, 1, W), lambda b, t: (b, 0, 0))],
        out_specs=[pl.BlockSpec((rows, W), lambda b, t: (b * nt + t, 0)),
                   pl.BlockSpec((1, 1, W), lambda b, t: (b, 0, 0))],
        out_shape=[jax.ShapeDtypeStruct((B * T, W), BF16), jax.ShapeDtypeStruct((B, 1, W), F32)],
        scratch_shapes=[pltpu.VMEM((1, W), F32)],
        compiler_params=_params("arbitrary", "arbitrary"),
        name=name,
    )(xc.reshape(B * T, W), lg.reshape(B * T, W), wa_bd, wx_bd, vec(ba), vec(bx), vec(sp), h0.reshape(B, 1, W))
    return out, ht.reshape(B, W)


def _gdn_chunk(q, k, v, gb, S, chunk):
    R = HEADS * chunk
    SD = HEADS * HEAD_DIM
    r0 = lax.broadcasted_iota(jnp.int32, (R, R), 0)
    r1 = lax.broadcasted_iota(jnp.int32, (R, R), 1)
    same = _group_id((R, R), 0, chunk) == _group_id((R, R), 1, chunk)
    low = same & (r0 >= r1)
    strict = same & (r0 > r1)
    upper_sel = jnp.where(same & (r0 <= r1), 1.0, 0.0).astype(BF16)
    block_sel = jnp.where(same, 1.0, 0.0).astype(BF16)
    eye = jnp.where(r0 == r1, 1.0, 0.0)

    g_rows = jnp.concatenate([gb, jnp.zeros((LANES - gb.shape[0], R), F32)], axis=0)
    cum = _dot_sel(g_rows, upper_sel)
    tot = _dot_sel(g_rows, block_sel)
    rid = lax.broadcasted_iota(jnp.int32, (LANES, R), 0)
    stacked = jnp.where(rid == 0, cum, jnp.where(rid == 1, g_rows, tot))
    cols = _sel_dot_nt(eye.astype(BF16), stacked)
    gcum_col, beta_col, gtot_col = cols[:, 0:1], cols[:, 1:2], cols[:, 2:3]
    gcum_row = cum[0:1, :]

    decay = jnp.where(low, jnp.exp(jnp.where(low, gcum_col - gcum_row, 0.0)), 0.0)
    kb = k.astype(BF16)
    kk = _dot_nt(kb, kb)
    qk = _dot_nt(q.astype(BF16), kb)
    nmat = jnp.where(strict, kk * decay * beta_col, 0.0)
    inv = eye - nmat
    pw = nmat
    width = 2
    while width < chunk:
        ph, pl_ = _split2(pw)
        pw = _dot3(ph, pl_, ph, pl_)
        ih, il = _split2(inv)
        ph, pl_ = _split2(pw)
        inv = inv + _dot3(ih, il, ph, pl_)
        width *= 2
    ih, il = _split2(inv)
    u = _dot3(ih, il, *_split2(v * beta_col))
    w = _dot3(ih, il, *_split2(k * (beta_col * jnp.exp(gcum_col))))

    head_match = _group_id((R, SD), 0, chunk) == _group_id((R, SD), 1, HEAD_DIM)

    def spread(x):
        return jnp.where(head_match, jnp.concatenate([x] * HEADS, axis=1), 0.0).astype(BF16)

    sb = S.astype(BF16)
    v_new = u - _dot(spread(w), sb)
    o = _dot(spread(q * jnp.exp(gcum_col)), sb) + _dot((qk * decay).astype(BF16), v_new.astype(BF16))
    kg = spread(k * jnp.exp(gtot_col - gcum_col))
    tok = jnp.bitwise_and(lax.broadcasted_iota(jnp.int32, (R, SD), 0), chunk - 1)
    first = jnp.where(head_match & (tok == 0), 1.0, 0.0).astype(BF16)
    c0, c1, c2 = _split3(cols)
    gtot_state = _dot_tn(first, c0) + (_dot_tn(first, c1) + _dot_tn(first, c2))
    s_new = S * jnp.exp(gtot_state[:, 2:3]) + _dot_tn(kg, v_new.astype(BF16))
    return o, s_new


def _gdn_body(q_ref, k_ref, v_ref, gb_ref, s0_ref, o_ref, sT_ref, s_ref, *, chunk, group):
    c = pl.program_id(1)

    @pl.when(c == 0)
    def _init():
        s_ref[...] = s0_ref[...]

    R = HEADS * chunk
    for g in range(group):
        o, s_new = _gdn_chunk(q_ref[g].reshape(R, HEAD_DIM), k_ref[g].reshape(R, HEAD_DIM),
                              v_ref[g].reshape(R, HEAD_DIM), gb_ref[g, 0], s_ref[g], chunk)
        o_ref[g] = o.reshape(HEADS, chunk, HEAD_DIM)
        s_ref[g] = s_new

    @pl.when(c == pl.num_programs(1) - 1)
    def _state():
        sT_ref[...] = s_ref[...]


def _gdn(q, k, v, gb, S0, *, chunk, group, name):
    B, H, T, hd = q.shape
    nc = T // chunk
    SD = H * hd
    tok = pl.BlockSpec((group, H, chunk, hd), lambda b, c: (b, 0, c, 0))
    st = pl.BlockSpec((group, SD, hd), lambda b, c: (b, 0, 0))
    return pl.pallas_call(
        functools.partial(_gdn_body, chunk=chunk, group=group),
        grid=(B // group, nc),
        in_specs=[tok, tok, tok, pl.BlockSpec((group, 1, 8, H * chunk), lambda b, c: (b, c, 0, 0)), st],
        out_specs=[tok, st],
        out_shape=[jax.ShapeDtypeStruct((B, H, T, hd), F32), jax.ShapeDtypeStruct((B, SD, hd), F32)],
        scratch_shapes=[pltpu.VMEM((group, SD, hd), F32)],
        compiler_params=_params("arbitrary", "arbitrary"),
        name=name,
    )(q, k, v, gb, S0)


def _merge_body(x_ref, b0_ref, b1_ref, b2_ref, b3_ref, gates_ref, wb_ref, wo_ref, o_ref):
    merged = None
    for n, b_ref in enumerate((b0_ref, b1_ref, b2_ref, b3_ref)):
        proj = _dot(b_ref[...], wb_ref[n])
        term = _sigmoid(gates_ref[:, n * D_MODEL:(n + 1) * D_MODEL]) * proj
        merged = term if merged is None else merged + term
    o_ref[...] = x_ref[...] + _dot(merged.astype(BF16), wo_ref[...])


def _merge(x2, branches, gates, wb, wo, *, row_tile, name):
    M, D = x2.shape
    row = lambda w: pl.BlockSpec((row_tile, w), lambda i: (i, 0))
    return pl.pallas_call(
        _merge_body,
        grid=(M // row_tile,),
        in_specs=[row(D)] + [row(BR_WIDTH)] * N_BRANCH + [row(N_BRANCH * D), _resident(wb.shape), _resident(wo.shape)],
        out_specs=row(D),
        out_shape=jax.ShapeDtypeStruct((M, D), F32),
        compiler_params=_params("arbitrary"),
        name=name,
    )(x2, *branches, gates, wb, wo)


def _mem_body(x_ref, g_ref, wq_ref, qg_ref, mk_ref, mv_ref, wo_ref, o_ref):
    x = x_ref[0]
    h = (x * _rms_scale(x) * g_ref[...]).astype(BF16)
    q = _dot(h, wq_ref[...])
    W = q.shape[1]
    head_sum = jnp.where(_group_id((W, W), 0, MEM_HD) == _group_id((W, W), 1, MEM_HD), 1.0, 0.0).astype(BF16)
    ms = _dot_sel(q * q, head_sum) * (1.0 / MEM_HD)
    qn = q * lax.rsqrt(ms + NORM_EPS) * qg_ref[...] * (1.0 / math.sqrt(MEM_HD))
    mk = mk_ref[0].astype(BF16)
    mv = mv_ref[0].astype(BF16)
    lane_head = _group_id(q.shape, 1, MEM_HD)
    o = jnp.zeros(q.shape, F32)
    for hh in range(MEM_HEADS):
        s = _dot_nt(jnp.where(lane_head == hh, qn, 0.0).astype(BF16), mk)
        p = jnp.exp(s - jnp.max(s, axis=1, keepdims=True))
        p = p / jnp.sum(p, axis=1, keepdims=True)
        o = o + jnp.where(lane_head == hh, _dot(p.astype(BF16), mv), 0.0)
    o_ref[0] = x + _dot(o.astype(BF16), wo_ref[...])


def _mem_sublayer(x, g, wq, qg, mk, mv, wo, *, row_tile, name):
    B, T, D = x.shape
    nt = T // row_tile
    kv = pl.BlockSpec((1,) + mk.shape[1:], lambda b, t: (b, 0, 0))
    return pl.pallas_call(
        _mem_body,
        grid=(B, nt),
        in_specs=[pl.BlockSpec((1, row_tile, D), lambda b, t: (b, t, 0)), _resident((1, D)), _resident(wq.shape),
                  _resident((1, MEM_WIDTH)), kv, kv, _resident(wo.shape)],
        out_specs=pl.BlockSpec((1, row_tile, D), lambda b, t: (b, t, 0)),
        out_shape=jax.ShapeDtypeStruct((B, T, D), F32),
        compiler_params=_params("arbitrary", "arbitrary"),
        name=name,
    )(x, g.reshape(1, D), wq, jnp.tile(qg, MEM_HEADS).reshape(1, MEM_WIDTH), mk, mv, wo)


def _mlp_body(x_ref, g_ref, wu_ref, wd_ref, o_ref, *, ff_chunk):
    x = x_ref[...]
    h = (x * _rms_scale(x) * g_ref[...]).astype(BF16)
    acc = x
    for c in range(D_FF // ff_chunk):
        u = _dot(h, wu_ref[:, c * ff_chunk:(c + 1) * ff_chunk])
        u = jnp.square(jnp.maximum(u, 0.0)).astype(BF16)
        acc = acc + _dot(u, wd_ref[c * ff_chunk:(c + 1) * ff_chunk, :])
    o_ref[...] = acc


def _mlp_sublayer(x2, g, w_up, w_down, *, row_tile, name):
    M, D = x2.shape
    return pl.pallas_call(
        functools.partial(_mlp_body, ff_chunk=1024),
        grid=(M // row_tile,),
        in_specs=[pl.BlockSpec((row_tile, D), lambda i: (i, 0)), _resident((1, D)), _resident(w_up.shape),
                  _resident(w_down.shape)],
        out_specs=pl.BlockSpec((row_tile, D), lambda i: (i, 0)),
        out_shape=jax.ShapeDtypeStruct((M, D), F32),
        compiler_params=_params("arbitrary"),
        name=name,
    )(x2, g.reshape(1, D), w_up, w_down)


def _rmsnorm(x, g):
    xf = x.astype(F32)
    return xf * lax.rsqrt(jnp.mean(xf * xf, axis=-1, keepdims=True) + NORM_EPS) * g.astype(F32)


def _l2norm(x):
    return x * lax.rsqrt(jnp.sum(x * x, axis=-1, keepdims=True) + NORM_EPS)


def _rope(x, pos):
    d = x.shape[-1]
    inv = ROPE_THETA ** (-jnp.arange(0, d, 2, dtype=F32) / d)
    ang = pos.astype(F32)[:, None] * inv[None, :]
    ang = ang.reshape((1, ang.shape[0]) + (1,) * (x.ndim - 3) + (d // 2,))
    cos, sin = jnp.cos(ang), jnp.sin(ang)
    x1, x2 = x[..., : d // 2], x[..., d // 2:]
    return jnp.concatenate([x1 * cos - x2 * sin, x2 * cos + x1 * sin], axis=-1)


def _causal_conv(x, buf, w, b=None):
    xp = jnp.concatenate([buf.astype(x.dtype), x], axis=1)
    T = x.shape[1]
    y = xp[:, 0:T] * w[0]
    for i in range(1, CONV_W):
        y = y + xp[:, i:i + T] * w[i]
    if b is not None:
        y = y + b
    return y, xp[:, -(CONV_W - 1):]


def _key_minor_pages(cache):
    nd = cache.ndim
    t = jnp.transpose(cache, (0, 1) + tuple(range(3, nd)) + (2,))
    return t.reshape(cache.shape[:2] + (-1, cache.shape[2]))


def _head_major(x):
    return jnp.swapaxes(x, 1, 2)


def _pad_time(x, t_pad, axis=1):
    pad = [(0, 0)] * x.ndim
    pad[axis] = (0, t_pad - x.shape[axis])
    return jnp.pad(x, pad)


def _pack_layer_weights(p):
    o = np.cumsum((0,) + (3 * BR_WIDTH, FOX_HEADS, LRU_WIDTH, LRU_WIDTH, 3 * BR_WIDTH, 3 * BR_WIDTH, GDN_HEADS,
                          GDN_HEADS, BR_WIDTH, N_BRANCH * D_MODEL)).tolist()
    wt = p['w_in'].T
    seg = lambda i: wt[o[i]:o[i + 1]]
    small = jnp.concatenate([seg(1), seg(6), seg(7), jnp.zeros((SMALL_COLS - 3 * HEADS, D_MODEL), wt.dtype)], axis=0)
    w_main = jnp.concatenate([seg(0), seg(2), seg(3), seg(4), seg(5), seg(8), small], axis=0).astype(BF16)
    blockdiag = lambda wb: jax.scipy.linalg.block_diag(*[wb[i] for i in range(LRU_BLOCKS)]).astype(BF16)
    return dict(w_main=w_main, w_gates=seg(9).astype(BF16), lru_wa=blockdiag(p['lru_w_a']),
                lru_wx=blockdiag(p['lru_w_x']), w_branch=p['w_branch'].astype(BF16), w_out=p['w_out'].astype(BF16),
                w_mem_q=p['w_mem_q'].astype(BF16), w_mem_kv=p['w_mem_kv'].astype(BF16),
                w_mem_o=p['w_mem_o'].astype(BF16), w_mlp_up=p['w_mlp_up'].astype(BF16),
                w_mlp_down=p['w_mlp_down'].astype(BF16))


_C_FOX, _C_LRUX, _C_LRUG, _C_DIFF, _C_GDN, _C_GDNZ, _C_SMALL = 0, 768, 1024, 1280, 2048, 2816, 3072


def _mixer_sublayer(x, pos, p, pw, lam_init, fox_past, diff_past, lru_h0, lru_buf, gdn_S0, gdn_buf, *, row_tile,
                    attn_tile, tag, attn_chunk=128):
    B, T, D = x.shape
    M = B * T
    x2 = x.reshape(M, D)
    zm, gates = _norm_matmul(x2, p['norm_mix'], [pw['w_main'], pw['w_gates']], row_tile=row_tile, name=tag + "in_proj",
                             transposed=True)
    zm = zm.reshape(B, T, MAIN_COLS)
    seg = lambda c0, w: zm[..., c0:c0 + w]

    r = seg(_C_FOX, 3 * BR_WIDTH).reshape(B, T, 3, FOX_HEADS, FOX_HD)
    fq = _rmsnorm(r[:, :, 0], p['fox_q_norm'])
    fk = _rmsnorm(r[:, :, 1], p['fox_k_norm'])
    fv = r[:, :, 2]
    flogf = jax.nn.log_sigmoid(seg(_C_SMALL, FOX_HEADS) + p['fox_b_f'])
    if fox_past is None:
        channel_major = lambda a: jnp.swapaxes(a.reshape(B, T, BR_WIDTH), 1, 2).astype(BF16)
        fox_out = _flash_attention_t(channel_major(fq * (1.0 / math.sqrt(FOX_HD))),
                                     fk.reshape(M, BR_WIDTH).astype(BF16), channel_major(fv), n_maps=1,
                                     tq=attn_tile, tk=attn_chunk, ck=jnp.cumsum(flogf, axis=1), name=tag + "fox_attn")
    else:
        fo = _paged_attention(_query_rows(fq[:, :, :, None], 1), fox_past['k'], fox_past['v'],
                              _new_page(fk.reshape(B, T, BR_WIDTH)), _new_page(fv.reshape(B, T, BR_WIDTH)),
                              fox_past['page_table'], fox_past['layer'], pages_per_step=fox_past['pages_per_step'],
                              n_maps=1, scale=1.0 / math.sqrt(FOX_HD), lf_cache_t=fox_past['logf'],
                              lfnew_t=_new_page(flogf), name=tag + "fox_paged")
        fox_out = fo[:, :T].reshape(M, BR_WIDTH).astype(BF16)

    xc, lru_buf_new = _causal_conv(seg(_C_LRUX, LRU_WIDTH), lru_buf, p['lru_conv_w'], p['lru_conv_b'])
    lru_rows = min(256, T) if T % 8 == 0 else 8
    t_pad = -(-T // lru_rows) * lru_rows
    lru_out, lru_hT = _lru(_pad_time(xc, t_pad), _pad_time(seg(_C_LRUG, LRU_WIDTH), t_pad), pw['lru_wa'], pw['lru_wx'],
                           p['lru_b_a'], p['lru_b_x'], jax.nn.softplus(-p['lru_lambda']), lru_h0,
                           rows=lru_rows, t_valid=T, name=tag + "lru")
    if t_pad != T:
        lru_out = lru_out.reshape(B, t_pad, LRU_WIDTH)[:, :T].reshape(M, LRU_WIDTH)

    r = seg(_C_DIFF, 3 * BR_WIDTH).reshape(B, T, 3, DIFF_HEADS, DIFF_VD)
    dq = _rope(_rmsnorm(r[:, :, 0].reshape(B, T, DIFF_HEADS, 2, DIFF_HD), p['diff_q_norm']), pos)
    dk = _rope(_rmsnorm(r[:, :, 1].reshape(B, T, DIFF_HEADS, 2, DIFF_HD), p['diff_k_norm']), pos)
    dvv = r[:, :, 2]
    lp = p['diff_lambda']
    lam = jnp.exp(jnp.sum(lp[0] * lp[1])) - jnp.exp(jnp.sum(lp[2] * lp[3])) + lam_init
    if diff_past is None:
        channel_major = lambda a: jnp.swapaxes(a.reshape(B, T, BR_WIDTH), 1, 2).astype(BF16)
        diff_out = _flash_attention_t(channel_major(dq * (1.0 / math.sqrt(DIFF_HD))),
                                      dk.reshape(M, BR_WIDTH).astype(BF16), channel_major(dvv), n_maps=2,
                                      tq=attn_tile, tk=attn_chunk, lam=lam, gsub=p['diff_sub_norm'],
                                      sub_scale=1.0 - lam_init, name=tag + "diff_attn")
    else:
        do = _paged_attention(_query_rows(dq, 2), diff_past['k'], diff_past['v'],
                              _new_page(dk.reshape(B, T, BR_WIDTH)), _new_page(dvv.reshape(B, T, BR_WIDTH)),
                              diff_past['page_table'], diff_past['layer'], pages_per_step=diff_past['pages_per_step'],
                              n_maps=2, scale=1.0 / math.sqrt(DIFF_HD), lam=lam, gsub=p['diff_sub_norm'],
                              sub_scale=1.0 - lam_init, name=tag + "diff_paged")
        diff_out = do[:, :T].reshape(M, BR_WIDTH).astype(BF16)

    gc, gdn_buf_new = _causal_conv(seg(_C_GDN, 3 * BR_WIDTH), gdn_buf, p['gdn_conv_w'])
    gc = jax.nn.silu(gc).reshape(B, T, 3, GDN_HEADS, GDN_HD)
    gq = _l2norm(gc[:, :, 0]) * (GDN_HD ** -0.5)
    gk = _l2norm(gc[:, :, 1])
    gvv = gc[:, :, 2]
    gbeta = jax.nn.sigmoid(seg(_C_SMALL + 2 * HEADS, GDN_HEADS))
    gg = -jnp.exp(p['gdn_A_log']) * jax.nn.softplus(seg(_C_SMALL + HEADS, GDN_HEADS) + p['gdn_dt_bias'])
    chunk = 64 if T % 64 == 0 else 32
    tg = -(-T // chunk) * chunk
    nc = tg // chunk
    rows_hc = lambda a: jnp.swapaxes(_pad_time(a, tg).reshape(B, nc, chunk, GDN_HEADS), 2, 3).reshape(B, nc, 1, GDN_HEADS * chunk)
    g_r, b_r = rows_hc(gg), rows_hc(gbeta)
    gb = jnp.concatenate([g_r, b_r, g_r, jnp.zeros((B, nc, 5, GDN_HEADS * chunk), F32)], axis=2)
    hm = lambda a: _head_major(_pad_time(a, tg))
    go, gS = _gdn(hm(gq), hm(gk), hm(gvv), gb, gdn_S0.reshape(B, GDN_HEADS * GDN_HD, GDN_HD), chunk=chunk,
                  group=math.gcd(B, 4), name=tag + "gdn")
    go = _head_major(go)[:, :T]
    go = _rmsnorm(go, p['gdn_out_norm']) * jax.nn.silu(seg(_C_GDNZ, BR_WIDTH).reshape(B, T, GDN_HEADS, GDN_HD))
    gdn_out = go.reshape(M, BR_WIDTH).astype(BF16)

    x2 = _merge(x2, (fox_out, lru_out, diff_out, gdn_out), gates, pw['w_branch'], pw['w_out'], row_tile=row_tile,
                name=tag + "merge")
    state = (fk, fv, flogf, dk, dvv, lru_hT, lru_buf_new, gS.reshape(B, GDN_HEADS, GDN_HD, GDN_HD), gdn_buf_new)
    return x2.reshape(B, T, D), state


def _memory_kv(mem, p, pw):
    B, Tm, D = mem.shape
    kv, = _norm_matmul(mem.reshape(B * Tm, D), p['norm_mem_src'], [pw['w_mem_kv']], row_tile=256, name="mem_kv")
    kv = kv.reshape(B, Tm, 2, MEM_HEADS, MEM_HD)
    return _rmsnorm(kv[:, :, 0], p['mem_k_norm']), kv[:, :, 1]


def _memory_sublayer(x, mk, mv, p, pw, *, row_tile, name):
    B, T, D = x.shape
    t_pad = -(-T // row_tile) * row_tile
    flat = lambda a: a.reshape(a.shape[0], a.shape[1], MEM_WIDTH)
    out = _mem_sublayer(_pad_time(x, t_pad), p['norm_mem'], pw['w_mem_q'], p['mem_q_norm'], flat(mk), flat(mv),
                        pw['w_mem_o'], row_tile=row_tile, name=name)
    return out[:, :T]


def kernel(x_prompt, x_sample, cache_fox_k, cache_fox_v, cache_fox_logf, cache_diff_k, cache_diff_v, cache_mem_k, cache_mem_v, state_lru_h, state_lru_conv, state_gdn_S, state_gdn_conv, page_table, mem_prompt, norm_mix, w_in, fox_b_f, fox_q_norm, fox_k_norm, lru_conv_w, lru_conv_b, lru_w_a, lru_b_a, lru_w_x, lru_b_x, lru_lambda, diff_q_norm, diff_k_norm, diff_lambda, diff_sub_norm, gdn_conv_w, gdn_A_log, gdn_dt_bias, gdn_out_norm, w_branch, w_out, norm_mem, norm_mem_src, w_mem_q, w_mem_kv, mem_q_norm, mem_k_norm, w_mem_o, norm_mlp, w_mlp_up, w_mlp_down):
    stacked = dict(norm_mix=norm_mix, w_in=w_in, fox_b_f=fox_b_f, fox_q_norm=fox_q_norm, fox_k_norm=fox_k_norm,
                   lru_conv_w=lru_conv_w, lru_conv_b=lru_conv_b, lru_w_a=lru_w_a, lru_b_a=lru_b_a,
                   lru_w_x=lru_w_x, lru_b_x=lru_b_x, lru_lambda=lru_lambda, diff_q_norm=diff_q_norm,
                   diff_k_norm=diff_k_norm, diff_lambda=diff_lambda, diff_sub_norm=diff_sub_norm,
                   gdn_conv_w=gdn_conv_w, gdn_A_log=gdn_A_log, gdn_dt_bias=gdn_dt_bias, gdn_out_norm=gdn_out_norm,
                   w_branch=w_branch, w_out=w_out, norm_mem=norm_mem, norm_mem_src=norm_mem_src, w_mem_q=w_mem_q,
                   w_mem_kv=w_mem_kv, mem_q_norm=mem_q_norm, mem_k_norm=mem_k_norm, w_mem_o=w_mem_o,
                   norm_mlp=norm_mlp, w_mlp_up=w_mlp_up, w_mlp_down=w_mlp_down)
    Bp, Tp, D = x_prompt.shape
    Bs, Ts, _ = x_sample.shape
    past_len = page_table.shape[1] * cache_fox_k.shape[2]
    pos_p = jnp.arange(Tp)
    pos_s = past_len + jnp.arange(Ts)
    h0_p = jnp.zeros((Bp, LRU_WIDTH), F32)
    lbuf_p = jnp.zeros((Bp, CONV_W - 1, LRU_WIDTH), F32)
    S0_p = jnp.zeros((Bp, GDN_HEADS, GDN_HD, GDN_HD), F32)
    gbuf_p = jnp.zeros((Bp, CONV_W - 1, 3 * BR_WIDTH), F32)

    fox_k_t, fox_v_t, diff_k_t, diff_v_t = (_key_minor_pages(c) for c in (cache_fox_k, cache_fox_v, cache_diff_k,
                                                                      cache_diff_v))
    fox_logf_t = jnp.swapaxes(cache_fox_logf, 2, 3)

    xp, xs = x_prompt, x_sample
    pst = [[] for _ in range(11)]
    sst = [[] for _ in range(9)]
    for l in range(DEPTH):
        p = {name: arr[l] for name, arr in stacked.items()}
        pw = _pack_layer_weights(p)
        lam_init = 0.8 - 0.6 * math.exp(-0.3 * l)
        xp, st_p = _mixer_sublayer(xp, pos_p, p, pw, lam_init, None, None, h0_p, lbuf_p, S0_p, gbuf_p,
                                   row_tile=256, attn_tile=512, attn_chunk=512, tag="p_")
        mk, mv = _memory_kv(mem_prompt, p, pw)
        xp = _memory_sublayer(xp, mk, mv, p, pw, row_tile=512, name="p_mem")
        xp = _mlp_sublayer(xp.reshape(Bp * Tp, D), p['norm_mlp'], pw['w_mlp_up'], pw['w_mlp_down'], row_tile=512,
                           name="p_mlp").reshape(Bp, Tp, D)
        for lst, val in zip(pst, (st_p[0], st_p[1], st_p[2], st_p[3], st_p[4], mk, mv,
                                  st_p[5], st_p[6], st_p[7], st_p[8])):
            lst.append(val)
        fox_past = dict(k=fox_k_t, v=fox_v_t, logf=fox_logf_t, page_table=page_table, layer=l, pages_per_step=16)
        diff_past = dict(k=diff_k_t, v=diff_v_t, page_table=page_table, layer=l, pages_per_step=16)
        xs, st_s = _mixer_sublayer(xs, pos_s, p, pw, lam_init, fox_past, diff_past, state_lru_h[l], state_lru_conv[l],
                                   state_gdn_S[l], state_gdn_conv[l], row_tile=Bs * Ts, attn_tile=None, tag="s_")
        xs = _memory_sublayer(xs, cache_mem_k[l], cache_mem_v[l], p, pw, row_tile=8, name="s_mem")
        xs = _mlp_sublayer(xs.reshape(Bs * Ts, D), p['norm_mlp'], pw['w_mlp_up'], pw['w_mlp_down'], row_tile=Bs * Ts,
                           name="s_mlp").reshape(Bs, Ts, D)
        for lst, val in zip(sst, st_s):
            lst.append(val)

    return tuple([xp, xs] + [jnp.stack(v_, axis=0) for v_ in pst] + [jnp.stack(v_, axis=0) for v_ in sst])
```

```python
import functools
import math

import jax
import jax.numpy as jnp
import numpy as np
from jax import lax
from jax.experimental import pallas as pl
from jax.experimental.pallas import tpu as pltpu

D_MODEL = 1024
DEPTH = 2
N_BRANCH = 4
BR_WIDTH = D_MODEL // 4
FOX_HEADS = 4
FOX_HD = BR_WIDTH // FOX_HEADS
LRU_WIDTH = BR_WIDTH
LRU_BLOCKS = 4
LRU_BD = LRU_WIDTH // LRU_BLOCKS
LRU_C = 8.0
CONV_W = 4
DIFF_HEADS = 4
DIFF_VD = BR_WIDTH // DIFF_HEADS
DIFF_HD = DIFF_VD // 2
GDN_HEADS = 4
GDN_HD = BR_WIDTH // GDN_HEADS
MEM_HEADS = 4
MEM_HD = 64
MEM_WIDTH = MEM_HEADS * MEM_HD
D_FF = 4 * D_MODEL
ROPE_THETA = 10000.0
Q_BLOCK = 128
NORM_EPS = 1e-6
HEADS = 4
HEAD_DIM = 64
SMALL_COLS = 128
MAIN_COLS = 12 * BR_WIDTH + SMALL_COLS

F32 = jnp.float32
BF16 = jnp.bfloat16
NEG_BIG = -1e30
LANES = 128
VMEM_LIMIT_BYTES = 56 * 1024 * 1024


def _params(*sem):
    return pltpu.CompilerParams(dimension_semantics=sem, vmem_limit_bytes=VMEM_LIMIT_BYTES)


def _resident(shape):
    nd = len(shape)
    return pl.BlockSpec(shape, lambda *_: (0,) * nd, pipeline_mode=pl.Buffered(1))


def _dot(a, b):
    return jnp.dot(a, b, preferred_element_type=F32)


def _dot_nt(a, b):
    return lax.dot_general(a, b, (((1,), (1,)), ((), ())), preferred_element_type=F32)


def _dot_tn(a, b):
    return lax.dot_general(a, b, (((0,), (0,)), ((), ())), preferred_element_type=F32)


def _split3(x):
    p0 = x.astype(BF16)
    r = x - p0.astype(F32)
    p1 = r.astype(BF16)
    p2 = (r - p1.astype(F32)).astype(BF16)
    return p0, p1, p2


def _split3_trunc(x):
    def top(v):
        return lax.bitcast_convert_type(lax.bitcast_convert_type(v, jnp.uint32) & jnp.uint32(0xFFFF0000), F32)
    p0 = top(x)
    r = x - p0
    p1 = top(r)
    p2 = top(r - p1)
    return p0.astype(BF16), p1.astype(BF16), p2.astype(BF16)


def _dot_sel(x, sel_bf16):
    p0, p1, p2 = _split3(x)
    return _dot(p0, sel_bf16) + (_dot(p1, sel_bf16) + _dot(p2, sel_bf16))


def _sel_dot_nt(sel_bf16, x):
    p0, p1, p2 = _split3(x)
    return _dot_nt(sel_bf16, p0) + (_dot_nt(sel_bf16, p1) + _dot_nt(sel_bf16, p2))


def _sel_dot(sel_bf16, x):
    p0, p1, p2 = _split3(x)
    return _dot(sel_bf16, p0) + (_dot(sel_bf16, p1) + _dot(sel_bf16, p2))


def _split2(x):
    hi = x.astype(BF16)
    return hi, (x - hi.astype(F32)).astype(BF16)


def _dot3(ah, al, bh, bl):
    return _dot(ah, bh) + (_dot(al, bh) + _dot(ah, bl))


def _group_id(shape, axis, size):
    return lax.shift_right_logical(lax.broadcasted_iota(jnp.int32, shape, axis), int(math.log2(size)))


def _rms_scale(x):
    return lax.rsqrt(jnp.mean(x * x, axis=-1, keepdims=True) + NORM_EPS)


def _sigmoid(x):
    return 1.0 / (1.0 + jnp.exp(-x))


def _norm_matmul_body(x_ref, g_ref, *refs, n_out, col_chunk, transposed):
    w_refs, o_refs = refs[:n_out], refs[n_out:]
    x = x_ref[...]
    h = (x * _rms_scale(x) * g_ref[...]).astype(BF16)
    for w_ref, o_ref in zip(w_refs, o_refs):
        n = w_ref.shape[0 if transposed else 1]
        for c0 in range(0, n, col_chunk):
            c1 = min(n, c0 + col_chunk)
            o_ref[:, c0:c1] = _dot_nt(h, w_ref[c0:c1, :]) if transposed else _dot(h, w_ref[:, c0:c1])


def _norm_matmul(x2, g, ws, *, row_tile, name, transposed=False):
    M, D = x2.shape
    widths = [w.shape[0 if transposed else 1] for w in ws]
    return pl.pallas_call(
        functools.partial(_norm_matmul_body, n_out=len(ws), col_chunk=1024, transposed=transposed),
        grid=(M // row_tile,),
        in_specs=[pl.BlockSpec((row_tile, D), lambda i: (i, 0)), _resident((1, D))] + [_resident(w.shape) for w in ws],
        out_specs=[pl.BlockSpec((row_tile, n), lambda i: (i, 0)) for n in widths],
        out_shape=[jax.ShapeDtypeStruct((M, n), F32) for n in widths],
        compiler_params=_params("arbitrary"),
        name=name,
    )(x2, g.reshape(1, D), *ws)


def _flash_t_body(*refs, n_maps, use_bias, tq, tk, group, sub_scale):
    refs = list(refs)
    qt_ref, k_ref, vt_ref = refs[:3]
    ck_ref = refs[3] if use_bias else None
    lam_ref, gsub_ref = (refs[-3], refs[-2]) if n_maps > 1 else (None, None)
    o_ref = refs[-1]
    i = pl.program_id(1)
    q0 = i * tq
    n_full = q0 // tk
    n_diag = tq // tk
    C = HEADS * HEAD_DIM
    width = HEAD_DIM // n_maps

    def step(pairs, qts, j, carries, masked):
        start = pl.multiple_of(j * tk, tk)
        kj = k_ref[pl.ds(start, tk), :]
        if masked:
            kpos = start + lax.broadcasted_iota(jnp.int32, (tk, tq), 0)
            qpos = q0 + lax.broadcasted_iota(jnp.int32, (tk, tq), 1)
            keep = kpos <= qpos
        if use_bias:
            ck = ck_ref[pl.ds(start, tk), :]
        new = []
        for (h, _), qt, (m, l, acc) in zip(pairs, qts, carries):
            st = _dot(kj, qt)
            if use_bias:
                st = st - ck[:, h:h + 1]
            if masked:
                st = jnp.where(keep, st, NEG_BIG)
            m_new = jnp.maximum(m, jnp.max(st, axis=0, keepdims=True))
            p = jnp.exp(st - m_new)
            alpha = jnp.exp(m - m_new)
            l = alpha * l + jnp.sum(p, axis=0, keepdims=True)
            vt = vt_ref[0, h * HEAD_DIM:(h + 1) * HEAD_DIM, pl.ds(start, tk)]
            acc = alpha * acc + _dot(vt, p.astype(BF16))
            new.append((m_new, l, acc))
        return tuple(new)

    all_pairs = [(h, mp) for h in range(HEADS) for mp in range(n_maps)]
    qt_all = qt_ref[0]
    row_chain = _group_id((C, tq), 0, width)
    normalized = {}
    for g0 in range(0, len(all_pairs), group):
        pairs = all_pairs[g0:g0 + group]
        qts = [jnp.where(row_chain == h * n_maps + mp, qt_all, jnp.zeros_like(qt_all)) for h, mp in pairs]
        init = tuple((jnp.full((1, tq), NEG_BIG, F32), jnp.zeros((1, tq), F32), jnp.zeros((HEAD_DIM, tq), F32))
                     for _ in pairs)
        carries = lax.fori_loop(0, n_full, lambda j, c: step(pairs, qts, j, c, False), init)
        for dj in range(n_diag):
            carries = step(pairs, qts, n_full + dj, carries, True)
        for pair, (_, l, acc) in zip(pairs, carries):
            normalized[pair] = acc / l
    outs = []
    for h in range(HEADS):
        if n_maps == 1:
            outs.append(normalized[(h, 0)])
        else:
            d = normalized[(h, 0)] - lam_ref[...] * normalized[(h, 1)]
            ms = jnp.mean(d * d, axis=0, keepdims=True)
            outs.append(d * lax.rsqrt(ms + NORM_EPS) * gsub_ref[...] * sub_scale)
    o_ref[...] = jnp.concatenate(outs, axis=0).T.astype(o_ref.dtype)


def _flash_attention_t(qt, k, vt, *, n_maps, tq, tk, group=4, ck=None, lam=None, gsub=None, sub_scale=1.0, name):
    B, C, T = qt.shape
    nq = T // tq
    operands = [qt, k, vt]
    in_specs = [pl.BlockSpec((1, C, tq), lambda b, i: (b, 0, i)),
                pl.BlockSpec((T, C), lambda b, i: (b, 0)),
                pl.BlockSpec((1, C, T), lambda b, i: (b, 0, 0))]
    if ck is not None:
        operands.append(ck)
        in_specs.append(pl.BlockSpec((T, LANES), lambda b, i: (b, 0)))
    if n_maps > 1:
        operands += [lam.reshape(1, 1), gsub.reshape(HEAD_DIM, 1)]
        in_specs += [_resident((1, 1)), _resident((HEAD_DIM, 1))]
    return pl.pallas_call(
        functools.partial(_flash_t_body, n_maps=n_maps, use_bias=ck is not None, tq=tq, tk=tk, group=group,
                          sub_scale=sub_scale),
        grid=(B, nq),
        in_specs=in_specs,
        out_specs=pl.BlockSpec((tq, C), lambda b, i: (b * nq + i, 0)),
        out_shape=jax.ShapeDtypeStruct((B * T, C), BF16),
        compiler_params=_params("arbitrary", "arbitrary"),
        name=name,
    )(*operands)


def _pad_lanes(x, width=LANES):
    return jnp.pad(x, [(0, 0)] * (x.ndim - 1) + [(0, width - x.shape[-1])])


Q_ROWS = 8


def _paged_body(pt_ref, *refs, n_pages, n_maps, use_bias, scale, sub_scale):
    del pt_ref
    refs = list(refs)
    q_ref = refs.pop(0)
    lam_ref, gsub_ref = (None, None) if use_bias else (refs.pop(0), refs.pop(0))
    knew_ref, vnew_ref = refs.pop(0), refs.pop(0)
    lfnew_ref = refs.pop(0) if use_bias else None
    k_refs, refs = refs[:n_pages], refs[n_pages:]
    v_refs, refs = refs[:n_pages], refs[n_pages:]
    lf_refs, refs = (refs[:n_pages], refs[n_pages:]) if use_bias else (None, refs)
    o_ref, m_ref, l_ref, acc_ref = refs[:4]
    carry_ref = refs[4] if use_bias else None
    g = pl.program_id(1)
    R = HEADS * n_maps * Q_ROWS
    C = HEADS * HEAD_DIM

    @pl.when(g == 0)
    def _init():
        m_ref[...] = jnp.full(m_ref.shape, NEG_BIG, F32)
        l_ref[...] = jnp.zeros(l_ref.shape, F32)
        acc_ref[...] = jnp.zeros(acc_ref.shape, F32)
        if use_bias:
            carry_ref[...] = jnp.zeros(carry_ref.shape, F32)

    q = q_ref[0]

    def head_rows(x):
        return jnp.concatenate([jnp.broadcast_to(x[h:h + 1], (Q_ROWS, x.shape[1])) for h in range(HEADS)], axis=0)

    def attend(kts, vts, lfs, causal_new):
        kt = jnp.concatenate(kts, axis=1).astype(BF16)
        s = _dot(q, kt) * scale
        if use_bias:
            upper = jnp.where(lax.broadcasted_iota(jnp.int32, (LANES, LANES), 0)
                              <= lax.broadcasted_iota(jnp.int32, (LANES, LANES), 1), 1.0, 0.0).astype(BF16)
            rows = HEADS * Q_ROWS
            n = len(lfs)
            within = _dot_sel(jnp.concatenate([head_rows(lf) for lf in lfs], axis=0), upper)
            incl = jnp.broadcast_to(within[:, LANES - 1:LANES], within.shape)
            d = 1
            while d < n:
                incl = incl + jnp.concatenate([jnp.zeros((d * rows, LANES), F32), incl[:(n - d) * rows]], axis=0)
                d *= 2
            carry = carry_ref[...]
            before = jnp.concatenate([jnp.zeros((rows, LANES), F32), incl[:(n - 1) * rows]], axis=0) if n > 1 else 0.0
            c_all = within + before
            carry_ref[...] = carry + incl[(n - 1) * rows:]
            s = s - (jnp.concatenate([c_all[i * rows:(i + 1) * rows] for i in range(n)], axis=1)
                     + jnp.concatenate([carry] * n, axis=1))
        if causal_new:
            t = jnp.bitwise_and(lax.broadcasted_iota(jnp.int32, s.shape, 0), Q_ROWS - 1)
            s = jnp.where(lax.broadcasted_iota(jnp.int32, s.shape, 1) <= t, s, NEG_BIG)
        m_prev = m_ref[...]
        m_new = jnp.maximum(m_prev, jnp.max(s, axis=1, keepdims=True))
        alpha = jnp.exp(m_prev - m_new)
        p = jnp.exp(s - m_new)
        l_ref[...] = alpha * l_ref[...] + jnp.sum(p, axis=1, keepdims=True)
        vt = jnp.concatenate(vts, axis=1).astype(BF16)
        acc_ref[...] = alpha * acc_ref[...] + _dot_nt(p.astype(BF16), vt)
        m_ref[...] = m_new

    attend([r[0, 0] for r in k_refs], [r[0, 0] for r in v_refs], [r[0, 0] for r in lf_refs] if use_bias else None, False)

    @pl.when(g == pl.num_programs(1) - 1)
    def _finish():
        attend([knew_ref[0]], [vnew_ref[0]], [lfnew_ref[0]] if use_bias else None, True)
        full = acc_ref[...] / l_ref[...]
        lane_head = _group_id((Q_ROWS, C), 1, HEAD_DIM)
        outs = []
        for mp in range(n_maps):
            o = jnp.zeros((Q_ROWS, C), F32)
            for h in range(HEADS):
                r0 = (h * n_maps + mp) * Q_ROWS
                o = o + jnp.where(lane_head == h, full[r0:r0 + Q_ROWS], 0.0)
            outs.append(o)
        if n_maps == 1:
            res = outs[0]
        else:
            d = outs[0] - lam_ref[...] * outs[1]
            head_sum = jnp.where(_group_id((C, C), 0, HEAD_DIM) == _group_id((C, C), 1, HEAD_DIM), 1.0, 0.0).astype(BF16)
            ms = _dot_sel(d * d, head_sum) * (1.0 / HEAD_DIM)
            res = d * lax.rsqrt(ms + NORM_EPS) * gsub_ref[...] * sub_scale
        o_ref[0] = res.astype(o_ref.dtype)


def _paged_attention(q_rows, k_cache_t, v_cache_t, knew_t, vnew_t, page_table, layer, *, pages_per_step, n_maps, scale,
                     lf_cache_t=None, lfnew_t=None, lam=None, gsub=None, sub_scale=1.0, name):
    B, R, C = q_rows.shape
    n_pages_total = page_table.shape[1]
    P = pages_per_step
    use_bias = lf_cache_t is not None
    per_seq = lambda shape: pl.BlockSpec((1,) + shape, lambda b, g, pt: (b,) + (0,) * len(shape))
    const = lambda shape: pl.BlockSpec(shape, lambda b, g, pt: (0,) * len(shape))

    def page_spec(i, tail):
        return pl.BlockSpec((1, 1) + tail, lambda b, g, pt: (layer, pt[b, g * P + i]) + (0,) * len(tail))

    operands = [q_rows]
    in_specs = [per_seq((R, C))]
    if not use_bias:
        operands += [lam.reshape(1, 1), jnp.tile(gsub, HEADS).reshape(1, C)]
        in_specs += [const((1, 1)), const((1, C))]
    operands += [knew_t, vnew_t]
    in_specs += [per_seq((C, LANES)), per_seq((C, LANES))]
    if use_bias:
        operands.append(lfnew_t)
        in_specs.append(per_seq((HEADS, LANES)))
    kv_tail = (C, LANES)
    operands += [k_cache_t] * P + [v_cache_t] * P
    in_specs += [page_spec(i, kv_tail) for i in range(P)] + [page_spec(i, kv_tail) for i in range(P)]
    scratch = [pltpu.VMEM((R, 1), F32), pltpu.VMEM((R, 1), F32), pltpu.VMEM((R, C), F32)]
    if use_bias:
        operands += [lf_cache_t] * P
        in_specs += [page_spec(i, (HEADS, LANES)) for i in range(P)]
        scratch.append(pltpu.VMEM((HEADS * Q_ROWS, LANES), F32))
    return pl.pallas_call(
        functools.partial(_paged_body, n_pages=P, n_maps=n_maps, use_bias=use_bias, scale=scale, sub_scale=sub_scale),
        grid_spec=pltpu.PrefetchScalarGridSpec(
            num_scalar_prefetch=1, grid=(B, n_pages_total // P), in_specs=in_specs,
            out_specs=pl.BlockSpec((1, Q_ROWS, C), lambda b, g, pt: (b, 0, 0)), scratch_shapes=scratch),
        out_shape=jax.ShapeDtypeStruct((B, Q_ROWS, C), F32),
        compiler_params=_params("arbitrary", "arbitrary"),
        name=name,
    )(page_table, *operands)


def _query_rows(q, n_maps):
    B, T, H, M, d = q.shape
    eye = jnp.eye(H * M, dtype=q.dtype).reshape(H, M, H, M)
    rows = jnp.einsum('bthmd,hmgn->bhmtgnd', q, eye)
    rows = jnp.pad(rows, ((0, 0), (0, 0), (0, 0), (0, Q_ROWS - T), (0, 0), (0, 0), (0, 0)))
    return rows.reshape(B, H * M * Q_ROWS, H * M * d).astype(BF16)


def _new_page(x):
    return jnp.pad(jnp.swapaxes(x, 1, 2), ((0, 0), (0, 0), (0, LANES - x.shape[1])))


def _gelu_tanh(x):
    return 0.5 * x * (1.0 + jnp.tanh(math.sqrt(2.0 / math.pi) * (x + 0.044715 * (x * x * x))))


def _lru_body(xc_ref, lg_ref, wa_ref, wx_ref, ba_ref, bx_ref, sp_ref, h0_ref, o_ref, ht_ref, carry_ref, *,
              rows, last_tile, last_row):
    t = pl.program_id(1)

    @pl.when(t == 0)
    def _init():
        carry_ref[...] = h0_ref[0]

    xc = xc_ref[...]
    xb = xc.astype(BF16)
    rg = _sigmoid(_dot(xb, wa_ref[...]) + ba_ref[...])
    ig = _sigmoid(_dot(xb, wx_ref[...]) + bx_ref[...])
    log_a = (-LRU_C) * rg * sp_ref[...]
    a = jnp.exp(log_a)
    b = jnp.sqrt(1.0 - jnp.exp(2.0 * log_a)) * (ig * xc)
    row = lax.broadcasted_iota(jnp.int32, a.shape, 0)
    d = 1
    while d < rows:
        a_prev = jnp.where(row >= d, pltpu.roll(a, d, 0), 1.0)
        b_prev = jnp.where(row >= d, pltpu.roll(b, d, 0), 0.0)
        b = a * b_prev + b
        a = a * a_prev
        d *= 2
    hs = b + a * carry_ref[...]
    carry_ref[...] = hs[rows - 1:rows]
    o_ref[...] = (hs * _gelu_tanh(lg_ref[...])).astype(o_ref.dtype)

    @pl.when(t == last_tile)
    def _state():
        ht_ref[0] = hs[last_row:last_row + 1]


def _lru(xc, lg, wa_bd, wx_bd, ba, bx, sp, h0, *, rows, t_valid, name):
    B, T, W = xc.shape
    nt = T // rows
    vec = lambda a: a.reshape(1, W)
    out, ht = pl.pallas_call(
        functools.partial(_lru_body, rows=rows, last_tile=(t_valid - 1) // rows, last_row=(t_valid - 1) % rows),
        grid=(B, nt),
        in_specs=[pl.BlockSpec((rows, W), lambda b, t: (b * nt + t, 0)),
                  pl.BlockSpec((rows, W), lambda b, t: (b * nt + t, 0)),
                  _resident((W, W)), _resident((W, W)), _resident((1, W)), _resident((1, W)), _resident((1, W)),
                  pl.BlockSpec((1, 1, W), lambda b, t: (b, 0, 0))],
        out_specs=[pl.BlockSpec((rows, W), lambda b, t: (b * nt + t, 0)),
                   pl.BlockSpec((1, 1, W), lambda b, t: (b, 0, 0))],
        out_shape=[jax.ShapeDtypeStruct((B * T, W), BF16), jax.ShapeDtypeStruct((B, 1, W), F32)],
        scratch_shapes=[pltpu.VMEM((1, W), F32)],
        compiler_params=_params("arbitrary", "arbitrary"),
        name=name,
    )(xc.reshape(B * T, W), lg.reshape(B * T, W), wa_bd, wx_bd, vec(ba), vec(bx), vec(sp), h0.reshape(B, 1, W))
    return out, ht.reshape(B, W)


def _gdn_chunk(q, k, v, gb, S, chunk):
    R = HEADS * chunk
    SD = HEADS * HEAD_DIM
    r0 = lax.broadcasted_iota(jnp.int32, (R, R), 0)
    r1 = lax.broadcasted_iota(jnp.int32, (R, R), 1)
    same = _group_id((R, R), 0, chunk) == _group_id((R, R), 1, chunk)
    low = same & (r0 >= r1)
    strict = same & (r0 > r1)
    eye = jnp.where(r0 == r1, 1.0, 0.0)

    colmat = jnp.concatenate([gb if h == 0 else pltpu.roll(gb, LANES - h, 1) for h in range(HEADS)], axis=0)
    cum_cols = _sel_dot(jnp.where(low, 1.0, 0.0).astype(BF16), colmat)
    tot_cols = _sel_dot(jnp.where(same, 1.0, 0.0).astype(BF16), colmat)
    gcum_col, beta_col, gtot_col = cum_cols[:, 0:1], colmat[:, HEADS:HEADS + 1], tot_cols[:, 0:1]
    c0, c1, c2 = _split3(cum_cols)
    eye_b = eye.astype(BF16)
    gcum_row = (_dot_tn(c0, eye_b) + (_dot_tn(c1, eye_b) + _dot_tn(c2, eye_b)))[0:1, :]

    decay = jnp.where(low, jnp.exp(jnp.where(low, gcum_col - gcum_row, 0.0)), 0.0)
    kb = k.astype(BF16)
    kk = _dot_nt(kb, kb)
    qk = _dot_nt(q.astype(BF16), kb)
    nmat = jnp.where(strict, kk * decay * beta_col, 0.0)
    inv = eye - nmat
    pw = nmat
    width = 2
    while width < chunk:
        ph, pl_ = _split2(pw)
        pw = _dot3(ph, pl_, ph, pl_)
        ih, il = _split2(inv)
        ph, pl_ = _split2(pw)
        inv = inv + _dot3(ih, il, ph, pl_)
        width *= 2
    ih, il = _split2(inv)
    u = _dot3(ih, il, *_split2(v * beta_col))
    w = _dot3(ih, il, *_split2(k * (beta_col * jnp.exp(gcum_col))))

    head_match = _group_id((R, SD), 0, chunk) == _group_id((R, SD), 1, HEAD_DIM)

    def spread(x):
        return jnp.where(head_match, jnp.concatenate([x] * HEADS, axis=1), 0.0).astype(BF16)

    sb = S.astype(BF16)
    v_new = u - _dot(spread(w), sb)
    o = _dot(spread(q * jnp.exp(gcum_col)), sb) + _dot((qk * decay).astype(BF16), v_new.astype(BF16))
    kg = spread(k * jnp.exp(gtot_col - gcum_col))
    tok = jnp.bitwise_and(lax.broadcasted_iota(jnp.int32, (R, SD), 0), chunk - 1)
    first = jnp.where(head_match & (tok == 0), 1.0, 0.0).astype(BF16)
    t0, t1, t2 = _split3(tot_cols)
    gtot_state = _dot_tn(first, t0) + (_dot_tn(first, t1) + _dot_tn(first, t2))
    s_new = S * jnp.exp(gtot_state[:, 0:1]) + _dot_tn(kg, v_new.astype(BF16))
    return o, s_new


def _gdn_body(q_ref, k_ref, v_ref, gb_ref, s0_ref, o_ref, sT_ref, s_ref, *, chunk, group):
    c = pl.program_id(1)

    @pl.when(c == 0)
    def _init():
        s_ref[...] = s0_ref[...]

    R = HEADS * chunk
    for g in range(group):
        o, s_new = _gdn_chunk(q_ref[g].reshape(R, HEAD_DIM), k_ref[g].reshape(R, HEAD_DIM),
                              v_ref[g].reshape(R, HEAD_DIM), gb_ref[g], s_ref[g], chunk)
        o_ref[g] = o.reshape(HEADS, chunk, HEAD_DIM)
        s_ref[g] = s_new

    @pl.when(c == pl.num_programs(1) - 1)
    def _state():
        sT_ref[...] = s_ref[...]


def _gdn(q, k, v, gb, S0, *, chunk, group, name):
    B, H, T, hd = q.shape
    nc = T // chunk
    SD = H * hd
    tok = pl.BlockSpec((group, H, chunk, hd), lambda b, c: (b, 0, c, 0))
    st = pl.BlockSpec((group, SD, hd), lambda b, c: (b, 0, 0))
    return pl.pallas_call(
        functools.partial(_gdn_body, chunk=chunk, group=group),
        grid=(B // group, nc),
        in_specs=[tok, tok, tok, pl.BlockSpec((group, chunk, LANES), lambda b, c: (b, c, 0)), st],
        out_specs=[tok, st],
        out_shape=[jax.ShapeDtypeStruct((B, H, T, hd), F32), jax.ShapeDtypeStruct((B, SD, hd), F32)],
        scratch_shapes=[pltpu.VMEM((group, SD, hd), F32)],
        compiler_params=_params("arbitrary", "arbitrary"),
        name=name,
    )(q, k, v, gb, S0)


def _log1p_exp_neg_abs(x):
    return jnp.log(1.0 + jnp.exp(-jnp.abs(x)))


def _conv_tap_sum(x, tail, w):
    rows = x.shape[0]
    first = lax.broadcasted_iota(jnp.int32, (8, x.shape[1]), 0)
    y = x * w[CONV_W - 1:CONV_W]
    for back in range(1, CONV_W):
        rolled = pltpu.roll(x, back, 0)
        top = jnp.where(first < back, pltpu.roll(tail, back, 0), rolled[0:8])
        y = y + jnp.concatenate([top, rolled[8:rows]], axis=0) * w[CONV_W - 1 - back:CONV_W - back]
    return y


def _prep_body(fq_ref, fk_ref, fv_ref, lx_ref, dq_ref, dk_ref, dv_ref, gq_ref, gk_ref, gv_ref, sm_ref,
               fqg_ref, fkg_ref, fb_ref, dqg_ref, dkg_ref, cos_ref, sin_ref, gw_ref, lw_ref, lb_ref, ga_ref, gdt_ref,
               fqt_o, fk_o, fvt_o, fkT_o, fvT_o, flT_o, ck_o, dqt_o, dk_o, dvt_o, dkT_o, dvT_o,
               gq_o, gk_o, gv_o, gb_o, xc_o, lst_o, gst_o, ltail, gtail, ccarry, *, rows):
    t = pl.program_id(1)
    C = HEADS * HEAD_DIM

    @pl.when(t == 0)
    def _init():
        ltail[...] = jnp.zeros(ltail.shape, F32)
        gtail[...] = jnp.zeros(gtail.shape, F32)
        ccarry[...] = jnp.zeros(ccarry.shape, F32)

    def group_sum(width):
        return jnp.where(_group_id((C, C), 0, width) == _group_id((C, C), 1, width), 1.0, 0.0).astype(BF16)

    sum64, sum32 = group_sum(HEAD_DIM), group_sum(DIFF_HD)

    def group_rms(x, sel, width):
        return x * lax.rsqrt(_dot_sel(x * x, sel) * (1.0 / width) + NORM_EPS)

    kn = group_rms(fk_ref[...], sum64, FOX_HD) * fkg_ref[...]
    qn = group_rms(fq_ref[...], sum64, FOX_HD) * fqg_ref[...] * (1.0 / math.sqrt(FOX_HD))
    fv = fv_ref[...]
    knT, fvT = kn.T, fv.T
    fqt_o[0] = qn.T.astype(BF16)
    fk_o[...] = kn.astype(BF16)
    fvt_o[0] = fvT.astype(BF16)
    fkT_o[0] = knT
    fvT_o[0] = fvT
    sm = sm_ref[...]
    xf = sm + fb_ref[...]
    logf = jnp.minimum(xf, 0.0) - _log1p_exp_neg_abs(xf)
    flT_o[0] = logf.T[0:8]
    lower = jnp.where(lax.broadcasted_iota(jnp.int32, (rows, rows), 0) >= lax.broadcasted_iota(jnp.int32, (rows, rows), 1),
                      1.0, 0.0).astype(BF16)
    c = _sel_dot(lower, logf) + ccarry[...]
    ccarry[...] = c[rows - 1:rows]
    ck_o[...] = c

    lane = lax.broadcasted_iota(jnp.int32, (rows, LANES), 1)
    first_half = jnp.bitwise_and(lane, DIFF_HD - 1) < DIFF_HD // 2

    def swap_halves(x):
        parts = []
        for c0 in range(0, C, LANES):
            h = x[:, c0:c0 + LANES]
            parts.append(jnp.where(first_half, pltpu.roll(h, LANES - DIFF_HD // 2, 1), pltpu.roll(h, DIFF_HD // 2, 1)))
        return jnp.concatenate(parts, axis=1)

    def rotary(x):
        return x * cos_ref[...] + swap_halves(x) * sin_ref[...]

    dkr = rotary(group_rms(dk_ref[...], sum32, DIFF_HD) * dkg_ref[...])
    dqr = rotary(group_rms(dq_ref[...], sum32, DIFF_HD) * dqg_ref[...]) * (1.0 / math.sqrt(DIFF_HD))
    dv = dv_ref[...]
    dkrT, dvT = dkr.T, dv.T
    dqt_o[0] = dqr.T.astype(BF16)
    dk_o[...] = dkr.astype(BF16)
    dvt_o[0] = dvT.astype(BF16)
    dkT_o[0] = dkrT
    dvT_o[0] = dvT

    gw = gw_ref[...]
    raws = (gq_ref[...], gk_ref[...], gv_ref[...])
    acts = []
    for i, raw in enumerate(raws):
        y = _conv_tap_sum(raw, gtail[:, i * C:(i + 1) * C], gw[:, i * C:(i + 1) * C])
        acts.append(y * _sigmoid(y))
    gqn = acts[0] * lax.rsqrt(_dot_sel(acts[0] * acts[0], sum64) + NORM_EPS) * (GDN_HD ** -0.5)
    gkn = acts[1] * lax.rsqrt(_dot_sel(acts[1] * acts[1], sum64) + NORM_EPS)
    pick_r = lax.broadcasted_iota(jnp.int32, (C, HEAD_DIM), 0)
    pick_c = lax.broadcasted_iota(jnp.int32, (C, HEAD_DIM), 1)
    for h in range(HEADS):
        pick = jnp.where(pick_r == pick_c + h * HEAD_DIM, 1.0, 0.0).astype(BF16)
        gq_o[0, h] = _dot_sel(gqn, pick)
        gk_o[0, h] = _dot_sel(gkn, pick)
        gv_o[0, h] = _dot_sel(acts[2], pick)
    sh = pltpu.roll(sm, LANES - HEADS, 1)
    xa = sh + gdt_ref[...]
    decay = -ga_ref[...] * (jnp.maximum(xa, 0.0) + _log1p_exp_neg_abs(xa))
    gb_o[...] = jnp.where(lane < HEADS, decay, _sigmoid(sh))

    lx = lx_ref[...]
    xc_o[...] = _conv_tap_sum(lx, ltail[...], lw_ref[...]) + lb_ref[...]

    ltail[...] = lx[rows - 8:rows]
    gtail[...] = jnp.concatenate([r[rows - 8:rows] for r in raws], axis=1)
    lst_o[0] = ltail[...]
    gst_o[0] = gtail[...]


def _prep(zm2, B, T, p, cos_t, sin_t, *, rows, name):
    M = B * T
    nt = T // rows
    C = HEADS * HEAD_DIM
    colblk = lambda j, w=C: pl.BlockSpec((rows, w), lambda b, t: (b * nt + t, j))
    vec = lambda a, w=C: jnp.pad(a.reshape(1, -1), ((0, 0), (0, w - a.size)))
    tile4 = lambda g: jnp.tile(g, C // g.size).reshape(1, C)
    in_specs = ([colblk(j) for j in (0, 1, 2, 3, 5, 6, 7, 8, 9, 10)] + [colblk(_C_SMALL // LANES, LANES)]
                + [_resident((1, C))] * 2 + [_resident((1, LANES))] + [_resident((1, C))] * 2
                + [pl.BlockSpec((rows, C), lambda b, t: (t, 0))] * 2
                + [_resident((CONV_W, 3 * C)), _resident((CONV_W, C)), _resident((1, C)), _resident((1, LANES)),
                   _resident((1, LANES))])
    chan = lambda dt: (jax.ShapeDtypeStruct((B, C, T), dt), pl.BlockSpec((1, C, rows), lambda b, t: (b, 0, t)))
    tokm = lambda w, dt: (jax.ShapeDtypeStruct((M, w), dt), pl.BlockSpec((rows, w), lambda b, t: (b * nt + t, 0)))
    headm = (jax.ShapeDtypeStruct((B, HEADS, T, HEAD_DIM), F32),
             pl.BlockSpec((1, HEADS, rows, HEAD_DIM), lambda b, t: (b, 0, t, 0)))
    state = lambda w: (jax.ShapeDtypeStruct((B, 8, w), F32), pl.BlockSpec((1, 8, w), lambda b, t: (b, 0, 0)))
    outs = [chan(BF16), tokm(C, BF16), chan(BF16), chan(F32), chan(F32),
            (jax.ShapeDtypeStruct((B, 8, T), F32), pl.BlockSpec((1, 8, rows), lambda b, t: (b, 0, t))), tokm(LANES, F32),
            chan(BF16), tokm(C, BF16), chan(BF16), chan(F32), chan(F32),
            headm, headm, headm, tokm(LANES, F32), tokm(C, F32), state(C), state(3 * C)]
    names = ['fqt', 'fk', 'fvt', 'fkT', 'fvT', 'flT', 'ck', 'dqt', 'dk', 'dvt', 'dkT', 'dvT', 'gq', 'gk', 'gv', 'gb', 'xc',
             'lru_tail', 'gdn_tail']
    res = pl.pallas_call(
        functools.partial(_prep_body, rows=rows),
        grid=(B, nt),
        in_specs=in_specs,
        out_specs=[o[1] for o in outs],
        out_shape=[o[0] for o in outs],
        scratch_shapes=[pltpu.VMEM((8, C), F32), pltpu.VMEM((8, 3 * C), F32), pltpu.VMEM((1, LANES), F32)],
        compiler_params=_params("arbitrary", "arbitrary"),
        name=name,
    )(*([zm2] * 11), tile4(p['fox_q_norm']), tile4(p['fox_k_norm']), vec(p['fox_b_f'], LANES),
      tile4(p['diff_q_norm']), tile4(p['diff_k_norm']), cos_t, sin_t, p['gdn_conv_w'], p['lru_conv_w'],
      p['lru_conv_b'].reshape(1, C), vec(jnp.exp(p['gdn_A_log']), LANES), vec(p['gdn_dt_bias'], LANES))
    return dict(zip(names, res))


def _rope_tables(pos):
    inv = ROPE_THETA ** (-jnp.arange(0, DIFF_HD, 2, dtype=F32) / DIFF_HD)
    ang = pos.astype(F32)[:, None] * inv[None, :]
    cos = jnp.tile(jnp.cos(ang), (1, 2 * BR_WIDTH // DIFF_HD))
    sin = jnp.tile(jnp.concatenate([-jnp.sin(ang), jnp.sin(ang)], axis=1), (1, BR_WIDTH // DIFF_HD))
    return cos, sin


def _merge_body(x_ref, b0_ref, b1_ref, b2_ref, b3_ref, gates_ref, wb_ref, wo_ref, o_ref):
    merged = None
    for n, b_ref in enumerate((b0_ref, b1_ref, b2_ref, b3_ref)):
        proj = _dot(b_ref[...], wb_ref[n])
        term = _sigmoid(gates_ref[:, n * D_MODEL:(n + 1) * D_MODEL]) * proj
        merged = term if merged is None else merged + term
    o_ref[...] = x_ref[...] + _dot(merged.astype(BF16), wo_ref[...])


def _merge(x2, branches, gates, wb, wo, *, row_tile, name):
    M, D = x2.shape
    row = lambda w: pl.BlockSpec((row_tile, w), lambda i: (i, 0))
    return pl.pallas_call(
        _merge_body,
        grid=(M // row_tile,),
        in_specs=[row(D)] + [row(BR_WIDTH)] * N_BRANCH + [row(N_BRANCH * D), _resident(wb.shape), _resident(wo.shape)],
        out_specs=row(D),
        out_shape=jax.ShapeDtypeStruct((M, D), F32),
        compiler_params=_params("arbitrary"),
        name=name,
    )(x2, *branches, gates, wb, wo)


def _mem_body(x_ref, g_ref, wq_ref, qg_ref, mk_ref, mv_ref, wo_ref, o_ref):
    x = x_ref[0]
    h = (x * _rms_scale(x) * g_ref[...]).astype(BF16)
    q = _dot(h, wq_ref[...])
    W = q.shape[1]
    head_sum = jnp.where(_group_id((W, W), 0, MEM_HD) == _group_id((W, W), 1, MEM_HD), 1.0, 0.0).astype(BF16)
    ms = _dot_sel(q * q, head_sum) * (1.0 / MEM_HD)
    qn = q * lax.rsqrt(ms + NORM_EPS) * qg_ref[...] * (1.0 / math.sqrt(MEM_HD))
    mk = mk_ref[0].astype(BF16)
    mv = mv_ref[0].astype(BF16)
    lane_head = _group_id(q.shape, 1, MEM_HD)
    o = jnp.zeros(q.shape, F32)
    for hh in range(MEM_HEADS):
        s = _dot_nt(jnp.where(lane_head == hh, qn, 0.0).astype(BF16), mk)
        p = jnp.exp(s - jnp.max(s, axis=1, keepdims=True))
        p = p / jnp.sum(p, axis=1, keepdims=True)
        o = o + jnp.where(lane_head == hh, _dot(p.astype(BF16), mv), 0.0)
    o_ref[0] = x + _dot(o.astype(BF16), wo_ref[...])


def _mem_sublayer(x, g, wq, qg, mk, mv, wo, *, row_tile, name):
    B, T, D = x.shape
    nt = T // row_tile
    kv = pl.BlockSpec((1,) + mk.shape[1:], lambda b, t: (b, 0, 0))
    return pl.pallas_call(
        _mem_body,
        grid=(B, nt),
        in_specs=[pl.BlockSpec((1, row_tile, D), lambda b, t: (b, t, 0)), _resident((1, D)), _resident(wq.shape),
                  _resident((1, MEM_WIDTH)), kv, kv, _resident(wo.shape)],
        out_specs=pl.BlockSpec((1, row_tile, D), lambda b, t: (b, t, 0)),
        out_shape=jax.ShapeDtypeStruct((B, T, D), F32),
        compiler_params=_params("arbitrary", "arbitrary"),
        name=name,
    )(x, g.reshape(1, D), wq, jnp.tile(qg, MEM_HEADS).reshape(1, MEM_WIDTH), mk, mv, wo)


def _mlp_body(x_ref, g_ref, wu_ref, wd_ref, o_ref, *, ff_chunk):
    x = x_ref[...]
    h = (x * _rms_scale(x) * g_ref[...]).astype(BF16)
    acc = x
    for c in range(D_FF // ff_chunk):
        u = _dot(h, wu_ref[:, c * ff_chunk:(c + 1) * ff_chunk])
        u = jnp.square(jnp.maximum(u, 0.0)).astype(BF16)
        acc = acc + _dot(u, wd_ref[c * ff_chunk:(c + 1) * ff_chunk, :])
    o_ref[...] = acc


def _mlp_sublayer(x2, g, w_up, w_down, *, row_tile, name):
    M, D = x2.shape
    return pl.pallas_call(
        functools.partial(_mlp_body, ff_chunk=1024),
        grid=(M // row_tile,),
        in_specs=[pl.BlockSpec((row_tile, D), lambda i: (i, 0)), _resident((1, D)), _resident(w_up.shape),
                  _resident(w_down.shape)],
        out_specs=pl.BlockSpec((row_tile, D), lambda i: (i, 0)),
        out_shape=jax.ShapeDtypeStruct((M, D), F32),
        compiler_params=_params("arbitrary"),
        name=name,
    )(x2, g.reshape(1, D), w_up, w_down)


def _rmsnorm(x, g):
    xf = x.astype(F32)
    return xf * lax.rsqrt(jnp.mean(xf * xf, axis=-1, keepdims=True) + NORM_EPS) * g.astype(F32)


def _l2norm(x):
    return x * lax.rsqrt(jnp.sum(x * x, axis=-1, keepdims=True) + NORM_EPS)


def _rope(x, pos):
    d = x.shape[-1]
    inv = ROPE_THETA ** (-jnp.arange(0, d, 2, dtype=F32) / d)
    ang = pos.astype(F32)[:, None] * inv[None, :]
    ang = ang.reshape((1, ang.shape[0]) + (1,) * (x.ndim - 3) + (d // 2,))
    cos, sin = jnp.cos(ang), jnp.sin(ang)
    x1, x2 = x[..., : d // 2], x[..., d // 2:]
    return jnp.concatenate([x1 * cos - x2 * sin, x2 * cos + x1 * sin], axis=-1)


def _causal_conv(x, buf, w, b=None):
    xp = jnp.concatenate([buf.astype(x.dtype), x], axis=1)
    T = x.shape[1]
    y = xp[:, 0:T] * w[0]
    for i in range(1, CONV_W):
        y = y + xp[:, i:i + T] * w[i]
    if b is not None:
        y = y + b
    return y, xp[:, -(CONV_W - 1):]


def _key_minor_pages(cache):
    nd = cache.ndim
    t = jnp.transpose(cache, (0, 1) + tuple(range(3, nd)) + (2,))
    return t.reshape(cache.shape[:2] + (-1, cache.shape[2]))


def _head_major(x):
    return jnp.swapaxes(x, 1, 2)


def _pad_time(x, t_pad, axis=1):
    pad = [(0, 0)] * x.ndim
    pad[axis] = (0, t_pad - x.shape[axis])
    return jnp.pad(x, pad)


def _pack_layer_weights(p):
    o = np.cumsum((0,) + (3 * BR_WIDTH, FOX_HEADS, LRU_WIDTH, LRU_WIDTH, 3 * BR_WIDTH, 3 * BR_WIDTH, GDN_HEADS,
                          GDN_HEADS, BR_WIDTH, N_BRANCH * D_MODEL)).tolist()
    wt = p['w_in'].T
    seg = lambda i: wt[o[i]:o[i + 1]]
    small = jnp.concatenate([seg(1), seg(6), seg(7), jnp.zeros((SMALL_COLS - 3 * HEADS, D_MODEL), wt.dtype)], axis=0)
    w_main = jnp.concatenate([seg(0), seg(2), seg(3), seg(4), seg(5), seg(8), small], axis=0).astype(BF16)
    blockdiag = lambda wb: jax.scipy.linalg.block_diag(*[wb[i] for i in range(LRU_BLOCKS)]).astype(BF16)
    return dict(w_main=w_main, w_gates=seg(9).astype(BF16), lru_wa=blockdiag(p['lru_w_a']),
                lru_wx=blockdiag(p['lru_w_x']), w_branch=p['w_branch'].astype(BF16), w_out=p['w_out'].astype(BF16),
                w_mem_q=p['w_mem_q'].astype(BF16), w_mem_kv=p['w_mem_kv'].astype(BF16),
                w_mem_o=p['w_mem_o'].astype(BF16), w_mlp_up=p['w_mlp_up'].astype(BF16),
                w_mlp_down=p['w_mlp_down'].astype(BF16))


_C_FOX, _C_LRUX, _C_LRUG, _C_DIFF, _C_GDN, _C_GDNZ, _C_SMALL = 0, 768, 1024, 1280, 2048, 2816, 3072


def _gdn_gate(go, z, g):
    B, T = go.shape[:2]
    out = _rmsnorm(go, g) * jax.nn.silu(z.reshape(B, T, GDN_HEADS, GDN_HD))
    return out.reshape(B * T, BR_WIDTH).astype(BF16)


def _mixer_sublayer_prompt(x, p, pw, lam_init, rope_tables, *, row_tile, attn_tile, attn_chunk, prep_rows, tag):
    B, T, D = x.shape
    M = B * T
    x2 = x.reshape(M, D)
    zm, gates = _norm_matmul(x2, p['norm_mix'], [pw['w_main'], pw['w_gates']], row_tile=row_tile, name=tag + "in_proj",
                             transposed=True)
    pre = _prep(zm, B, T, p, *rope_tables, rows=prep_rows, name=tag + "prep")
    fox_out = _flash_attention_t(pre['fqt'], pre['fk'], pre['fvt'], n_maps=1, tq=attn_tile, tk=attn_chunk, ck=pre['ck'],
                                 name=tag + "fox_attn")
    lp = p['diff_lambda']
    lam = jnp.exp(jnp.sum(lp[0] * lp[1])) - jnp.exp(jnp.sum(lp[2] * lp[3])) + lam_init
    diff_out = _flash_attention_t(pre['dqt'], pre['dk'], pre['dvt'], n_maps=2, tq=attn_tile, tk=attn_chunk, lam=lam,
                                  gsub=p['diff_sub_norm'], sub_scale=1.0 - lam_init, name=tag + "diff_attn")
    zm3 = zm.reshape(B, T, MAIN_COLS)
    lru_out, lru_hT = _lru(pre['xc'].reshape(B, T, LRU_WIDTH), zm3[..., _C_LRUG:_C_LRUG + LRU_WIDTH], pw['lru_wa'],
                           pw['lru_wx'], p['lru_b_a'], p['lru_b_x'], jax.nn.softplus(-p['lru_lambda']),
                           jnp.zeros((B, LRU_WIDTH), F32), rows=256, t_valid=T, name=tag + "lru")
    go, gS = _gdn(pre['gq'], pre['gk'], pre['gv'], pre['gb'].reshape(B, T, LANES),
                  jnp.zeros((B, GDN_HEADS * GDN_HD, GDN_HD), F32), chunk=64, group=math.gcd(B, 4), name=tag + "gdn")
    gdn_out = _gdn_gate(_head_major(go), zm3[..., _C_GDNZ:_C_GDNZ + BR_WIDTH], p['gdn_out_norm'])
    x2 = _merge(x2, (fox_out, lru_out, diff_out, gdn_out), gates, pw['w_branch'], pw['w_out'], row_tile=row_tile,
                name=tag + "merge")
    token_major = lambda a: jnp.swapaxes(a, 1, 2)
    state = (token_major(pre['fkT']).reshape(B, T, FOX_HEADS, FOX_HD),
             token_major(pre['fvT']).reshape(B, T, FOX_HEADS, FOX_HD),
             token_major(pre['flT'][:, :FOX_HEADS]),
             token_major(pre['dkT']).reshape(B, T, DIFF_HEADS, 2, DIFF_HD),
             token_major(pre['dvT']).reshape(B, T, DIFF_HEADS, DIFF_VD),
             lru_hT, pre['lru_tail'][:, 8 - (CONV_W - 1):], gS.reshape(B, GDN_HEADS, GDN_HD, GDN_HD),
             pre['gdn_tail'][:, 8 - (CONV_W - 1):])
    return x2.reshape(B, T, D), state


def _mixer_sublayer(x, pos, p, pw, lam_init, fox_past, diff_past, lru_h0, lru_buf, gdn_S0, gdn_buf, *, row_tile, tag):
    B, T, D = x.shape
    M = B * T
    x2 = x.reshape(M, D)
    zm, gates = _norm_matmul(x2, p['norm_mix'], [pw['w_main'], pw['w_gates']], row_tile=row_tile, name=tag + "in_proj",
                             transposed=True)
    zm = zm.reshape(B, T, MAIN_COLS)
    seg = lambda c0, w: zm[..., c0:c0 + w]

    r = seg(_C_FOX, 3 * BR_WIDTH).reshape(B, T, 3, FOX_HEADS, FOX_HD)
    fq = _rmsnorm(r[:, :, 0], p['fox_q_norm'])
    fk = _rmsnorm(r[:, :, 1], p['fox_k_norm'])
    fv = r[:, :, 2]
    flogf = jax.nn.log_sigmoid(seg(_C_SMALL, FOX_HEADS) + p['fox_b_f'])
    fo = _paged_attention(_query_rows(fq[:, :, :, None], 1), fox_past['k'], fox_past['v'],
                          _new_page(fk.reshape(B, T, BR_WIDTH)), _new_page(fv.reshape(B, T, BR_WIDTH)),
                          fox_past['page_table'], fox_past['layer'], pages_per_step=fox_past['pages_per_step'],
                          n_maps=1, scale=1.0 / math.sqrt(FOX_HD), lf_cache_t=fox_past['logf'],
                          lfnew_t=_new_page(flogf), name=tag + "fox_paged")
    fox_out = fo[:, :T].reshape(M, BR_WIDTH).astype(BF16)

    xc, lru_buf_new = _causal_conv(seg(_C_LRUX, LRU_WIDTH), lru_buf, p['lru_conv_w'], p['lru_conv_b'])
    lru_rows = min(256, T) if T % 8 == 0 else 8
    t_pad = -(-T // lru_rows) * lru_rows
    lru_out, lru_hT = _lru(_pad_time(xc, t_pad), _pad_time(seg(_C_LRUG, LRU_WIDTH), t_pad), pw['lru_wa'], pw['lru_wx'],
                           p['lru_b_a'], p['lru_b_x'], jax.nn.softplus(-p['lru_lambda']), lru_h0,
                           rows=lru_rows, t_valid=T, name=tag + "lru")
    if t_pad != T:
        lru_out = lru_out.reshape(B, t_pad, LRU_WIDTH)[:, :T].reshape(M, LRU_WIDTH)

    r = seg(_C_DIFF, 3 * BR_WIDTH).reshape(B, T, 3, DIFF_HEADS, DIFF_VD)
    dq = _rope(_rmsnorm(r[:, :, 0].reshape(B, T, DIFF_HEADS, 2, DIFF_HD), p['diff_q_norm']), pos)
    dk = _rope(_rmsnorm(r[:, :, 1].reshape(B, T, DIFF_HEADS, 2, DIFF_HD), p['diff_k_norm']), pos)
    dvv = r[:, :, 2]
    lp = p['diff_lambda']
    lam = jnp.exp(jnp.sum(lp[0] * lp[1])) - jnp.exp(jnp.sum(lp[2] * lp[3])) + lam_init
    do = _paged_attention(_query_rows(dq, 2), diff_past['k'], diff_past['v'],
                          _new_page(dk.reshape(B, T, BR_WIDTH)), _new_page(dvv.reshape(B, T, BR_WIDTH)),
                          diff_past['page_table'], diff_past['layer'], pages_per_step=diff_past['pages_per_step'],
                          n_maps=2, scale=1.0 / math.sqrt(DIFF_HD), lam=lam, gsub=p['diff_sub_norm'],
                          sub_scale=1.0 - lam_init, name=tag + "diff_paged")
    diff_out = do[:, :T].reshape(M, BR_WIDTH).astype(BF16)

    gc, gdn_buf_new = _causal_conv(seg(_C_GDN, 3 * BR_WIDTH), gdn_buf, p['gdn_conv_w'])
    gc = jax.nn.silu(gc).reshape(B, T, 3, GDN_HEADS, GDN_HD)
    gq = _l2norm(gc[:, :, 0]) * (GDN_HD ** -0.5)
    gk = _l2norm(gc[:, :, 1])
    gvv = gc[:, :, 2]
    gbeta = jax.nn.sigmoid(seg(_C_SMALL + 2 * HEADS, GDN_HEADS))
    gg = -jnp.exp(p['gdn_A_log']) * jax.nn.softplus(seg(_C_SMALL + HEADS, GDN_HEADS) + p['gdn_dt_bias'])
    chunk = 64 if T % 64 == 0 else 32
    tg = -(-T // chunk) * chunk
    gb = _pad_lanes(_pad_time(jnp.concatenate([gg, gbeta], axis=-1), tg))
    hm = lambda a: _head_major(_pad_time(a, tg))
    go, gS = _gdn(hm(gq), hm(gk), hm(gvv), gb, gdn_S0.reshape(B, GDN_HEADS * GDN_HD, GDN_HD), chunk=chunk,
                  group=math.gcd(B, 4), name=tag + "gdn")
    gdn_out = _gdn_gate(_head_major(go)[:, :T], seg(_C_GDNZ, BR_WIDTH), p['gdn_out_norm'])

    x2 = _merge(x2, (fox_out, lru_out, diff_out, gdn_out), gates, pw['w_branch'], pw['w_out'], row_tile=row_tile,
                name=tag + "merge")
    state = (fk, fv, flogf, dk, dvv, lru_hT, lru_buf_new, gS.reshape(B, GDN_HEADS, GDN_HD, GDN_HD), gdn_buf_new)
    return x2.reshape(B, T, D), state


def _memory_kv(mem, p, pw):
    B, Tm, D = mem.shape
    kv, = _norm_matmul(mem.reshape(B * Tm, D), p['norm_mem_src'], [pw['w_mem_kv']], row_tile=256, name="mem_kv")
    kv = kv.reshape(B, Tm, 2, MEM_HEADS, MEM_HD)
    return _rmsnorm(kv[:, :, 0], p['mem_k_norm']), kv[:, :, 1]


def _memory_sublayer(x, mk, mv, p, pw, *, row_tile, name):
    B, T, D = x.shape
    t_pad = -(-T // row_tile) * row_tile
    flat = lambda a: a.reshape(a.shape[0], a.shape[1], MEM_WIDTH)
    out = _mem_sublayer(_pad_time(x, t_pad), p['norm_mem'], pw['w_mem_q'], p['mem_q_norm'], flat(mk), flat(mv),
                        pw['w_mem_o'], row_tile=row_tile, name=name)
    return out[:, :T]


def kernel(x_prompt, x_sample, cache_fox_k, cache_fox_v, cache_fox_logf, cache_diff_k, cache_diff_v, cache_mem_k, cache_mem_v, state_lru_h, state_lru_conv, state_gdn_S, state_gdn_conv, page_table, mem_prompt, norm_mix, w_in, fox_b_f, fox_q_norm, fox_k_norm, lru_conv_w, lru_conv_b, lru_w_a, lru_b_a, lru_w_x, lru_b_x, lru_lambda, diff_q_norm, diff_k_norm, diff_lambda, diff_sub_norm, gdn_conv_w, gdn_A_log, gdn_dt_bias, gdn_out_norm, w_branch, w_out, norm_mem, norm_mem_src, w_mem_q, w_mem_kv, mem_q_norm, mem_k_norm, w_mem_o, norm_mlp, w_mlp_up, w_mlp_down):
    stacked = dict(norm_mix=norm_mix, w_in=w_in, fox_b_f=fox_b_f, fox_q_norm=fox_q_norm, fox_k_norm=fox_k_norm,
                   lru_conv_w=lru_conv_w, lru_conv_b=lru_conv_b, lru_w_a=lru_w_a, lru_b_a=lru_b_a,
                   lru_w_x=lru_w_x, lru_b_x=lru_b_x, lru_lambda=lru_lambda, diff_q_norm=diff_q_norm,
                   diff_k_norm=diff_k_norm, diff_lambda=diff_lambda, diff_sub_norm=diff_sub_norm,
                   gdn_conv_w=gdn_conv_w, gdn_A_log=gdn_A_log, gdn_dt_bias=gdn_dt_bias, gdn_out_norm=gdn_out_norm,
                   w_branch=w_branch, w_out=w_out, norm_mem=norm_mem, norm_mem_src=norm_mem_src, w_mem_q=w_mem_q,
                   w_mem_kv=w_mem_kv, mem_q_norm=mem_q_norm, mem_k_norm=mem_k_norm, w_mem_o=w_mem_o,
                   norm_mlp=norm_mlp, w_mlp_up=w_mlp_up, w_mlp_down=w_mlp_down)
    Bp, Tp, D = x_prompt.shape
    Bs, Ts, _ = x_sample.shape
    past_len = page_table.shape[1] * cache_fox_k.shape[2]
    rope_p = _rope_tables(jnp.arange(Tp))
    pos_s = past_len + jnp.arange(Ts)

    fox_k_t, fox_v_t, diff_k_t, diff_v_t = (_key_minor_pages(c) for c in (cache_fox_k, cache_fox_v, cache_diff_k,
                                                                      cache_diff_v))
    fox_logf_t = jnp.swapaxes(cache_fox_logf, 2, 3)

    xp, xs = x_prompt, x_sample
    pst = [[] for _ in range(11)]
    sst = [[] for _ in range(9)]
    for l in range(DEPTH):
        p = {name: arr[l] for name, arr in stacked.items()}
        pw = _pack_layer_weights(p)
        lam_init = 0.8 - 0.6 * math.exp(-0.3 * l)
        xp, st_p = _mixer_sublayer_prompt(xp, p, pw, lam_init, rope_p, row_tile=256, attn_tile=512, attn_chunk=512,
                                          prep_rows=256, tag="p_")
        mk, mv = _memory_kv(mem_prompt, p, pw)
        xp = _memory_sublayer(xp, mk, mv, p, pw, row_tile=512, name="p_mem")
        xp = _mlp_sublayer(xp.reshape(Bp * Tp, D), p['norm_mlp'], pw['w_mlp_up'], pw['w_mlp_down'], row_tile=512,
                           name="p_mlp").reshape(Bp, Tp, D)
        for lst, val in zip(pst, (st_p[0], st_p[1], st_p[2], st_p[3], st_p[4], mk, mv,
                                  st_p[5], st_p[6], st_p[7], st_p[8])):
            lst.append(val)
        fox_past = dict(k=fox_k_t, v=fox_v_t, logf=fox_logf_t, page_table=page_table, layer=l, pages_per_step=16)
        diff_past = dict(k=diff_k_t, v=diff_v_t, page_table=page_table, layer=l, pages_per_step=16)
        xs, st_s = _mixer_sublayer(xs, pos_s, p, pw, lam_init, fox_past, diff_past, state_lru_h[l], state_lru_conv[l],
                                   state_gdn_S[l], state_gdn_conv[l], row_tile=Bs * Ts, tag="s_")
        xs = _memory_sublayer(xs, cache_mem_k[l], cache_mem_v[l], p, pw, row_tile=8, name="s_mem")
        xs = _mlp_sublayer(xs.reshape(Bs * Ts, D), p['norm_mlp'], pw['w_mlp_up'], pw['w_mlp_down'], row_tile=Bs * Ts,
                           name="s_mlp").reshape(Bs, Ts, D)
        for lst, val in zip(sst, st_s):
            lst.append(val)

    return tuple([xp, xs] + [jnp.stack(v_, axis=0) for v_ in pst] + [jnp.stack(v_, axis=0) for v_ in sst])
```

```python
import functools
import math

import jax
import jax.numpy as jnp
import numpy as np
from jax import lax
from jax.experimental import pallas as pl
from jax.experimental.pallas import tpu as pltpu

D_MODEL = 1024
DEPTH = 2
N_BRANCH = 4
BR_WIDTH = D_MODEL // 4
FOX_HEADS = 4
FOX_HD = BR_WIDTH // FOX_HEADS
LRU_WIDTH = BR_WIDTH
LRU_BLOCKS = 4
LRU_BD = LRU_WIDTH // LRU_BLOCKS
LRU_C = 8.0
CONV_W = 4
DIFF_HEADS = 4
DIFF_VD = BR_WIDTH // DIFF_HEADS
DIFF_HD = DIFF_VD // 2
GDN_HEADS = 4
GDN_HD = BR_WIDTH // GDN_HEADS
MEM_HEADS = 4
MEM_HD = 64
MEM_WIDTH = MEM_HEADS * MEM_HD
D_FF = 4 * D_MODEL
ROPE_THETA = 10000.0
Q_BLOCK = 128
NORM_EPS = 1e-6
HEADS = 4
HEAD_DIM = 64
SMALL_COLS = 128
MAIN_COLS = 12 * BR_WIDTH + SMALL_COLS

F32 = jnp.float32
BF16 = jnp.bfloat16
NEG_BIG = -1e30
LANES = 128
VMEM_LIMIT_BYTES = 56 * 1024 * 1024


def _params(*sem):
    return pltpu.CompilerParams(dimension_semantics=sem, vmem_limit_bytes=VMEM_LIMIT_BYTES)


def _resident(shape):
    nd = len(shape)
    return pl.BlockSpec(shape, lambda *_: (0,) * nd, pipeline_mode=pl.Buffered(1))


def _dot(a, b):
    return jnp.dot(a, b, preferred_element_type=F32)


def _dot_nt(a, b):
    return lax.dot_general(a, b, (((1,), (1,)), ((), ())), preferred_element_type=F32)


def _dot_tn(a, b):
    return lax.dot_general(a, b, (((0,), (0,)), ((), ())), preferred_element_type=F32)


def _split3(x):
    p0 = x.astype(BF16)
    r = x - p0.astype(F32)
    p1 = r.astype(BF16)
    p2 = (r - p1.astype(F32)).astype(BF16)
    return p0, p1, p2


def _split3_trunc(x):
    def top(v):
        return lax.bitcast_convert_type(lax.bitcast_convert_type(v, jnp.uint32) & jnp.uint32(0xFFFF0000), F32)
    p0 = top(x)
    r = x - p0
    p1 = top(r)
    p2 = top(r - p1)
    return p0.astype(BF16), p1.astype(BF16), p2.astype(BF16)


def _dot_sel(x, sel_bf16):
    p0, p1, p2 = _split3(x)
    return _dot(p0, sel_bf16) + (_dot(p1, sel_bf16) + _dot(p2, sel_bf16))


def _sel_dot_nt(sel_bf16, x):
    p0, p1, p2 = _split3(x)
    return _dot_nt(sel_bf16, p0) + (_dot_nt(sel_bf16, p1) + _dot_nt(sel_bf16, p2))


def _sel_dot(sel_bf16, x):
    p0, p1, p2 = _split3(x)
    return _dot(sel_bf16, p0) + (_dot(sel_bf16, p1) + _dot(sel_bf16, p2))


def _split2(x):
    hi = x.astype(BF16)
    return hi, (x - hi.astype(F32)).astype(BF16)


def _dot3(ah, al, bh, bl):
    return _dot(ah, bh) + (_dot(al, bh) + _dot(ah, bl))


def _group_id(shape, axis, size):
    return lax.shift_right_logical(lax.broadcasted_iota(jnp.int32, shape, axis), int(math.log2(size)))


def _rms_scale(x):
    return lax.rsqrt(jnp.mean(x * x, axis=-1, keepdims=True) + NORM_EPS)


def _sigmoid(x):
    return 1.0 / (1.0 + jnp.exp(-x))


def _norm_matmul_body(x_ref, g_ref, *refs, n_out, col_chunk, transposed):
    w_refs, o_refs = refs[:n_out], refs[n_out:]
    x = x_ref[...]
    h = (x * _rms_scale(x) * g_ref[...]).astype(BF16)
    for w_ref, o_ref in zip(w_refs, o_refs):
        n = w_ref.shape[0 if transposed else 1]
        for c0 in range(0, n, col_chunk):
            c1 = min(n, c0 + col_chunk)
            o_ref[:, c0:c1] = _dot_nt(h, w_ref[c0:c1, :]) if transposed else _dot(h, w_ref[:, c0:c1])


def _norm_matmul(x2, g, ws, *, row_tile, name, transposed=False):
    M, D = x2.shape
    widths = [w.shape[0 if transposed else 1] for w in ws]
    return pl.pallas_call(
        functools.partial(_norm_matmul_body, n_out=len(ws), col_chunk=1024, transposed=transposed),
        grid=(M // row_tile,),
        in_specs=[pl.BlockSpec((row_tile, D), lambda i: (i, 0)), _resident((1, D))] + [_resident(w.shape) for w in ws],
        out_specs=[pl.BlockSpec((row_tile, n), lambda i: (i, 0)) for n in widths],
        out_shape=[jax.ShapeDtypeStruct((M, n), F32) for n in widths],
        compiler_params=_params("arbitrary"),
        name=name,
    )(x2, g.reshape(1, D), *ws)


def _flash_t_body(*refs, n_maps, use_bias, tq, tk, group, sub_scale):
    refs = list(refs)
    qt_ref, k_ref, vt_ref = refs[:3]
    ck_ref = refs[3] if use_bias else None
    lam_ref, gsub_ref = (refs[-3], refs[-2]) if n_maps > 1 else (None, None)
    o_ref = refs[-1]
    i = pl.program_id(1)
    q0 = i * tq
    n_full = q0 // tk
    n_diag = tq // tk
    C = HEADS * HEAD_DIM
    width = HEAD_DIM // n_maps

    def step(pairs, qts, j, carries, masked):
        start = pl.multiple_of(j * tk, tk)
        kj = k_ref[pl.ds(start, tk), :]
        if masked:
            kpos = start + lax.broadcasted_iota(jnp.int32, (tk, tq), 0)
            qpos = q0 + lax.broadcasted_iota(jnp.int32, (tk, tq), 1)
            keep = kpos <= qpos
        if use_bias:
            ck = ck_ref[pl.ds(start, tk), :]
        sts = [_dot(kj, qt) for qt in qts]
        if use_bias:
            sts = [st - ck[:, h:h + 1] for st, (h, _) in zip(sts, pairs)]
        if masked:
            sts = [jnp.where(keep, st, NEG_BIG) for st in sts]
        m_news = [jnp.maximum(m, jnp.max(st, axis=0, keepdims=True)) for st, (m, _, _) in zip(sts, carries)]
        ps = [jnp.exp(st - m_new) for st, m_new in zip(sts, m_news)]
        alphas = [jnp.exp(m - m_new) for (m, _, _), m_new in zip(carries, m_news)]
        ls = [a * l + jnp.sum(p, axis=0, keepdims=True) for a, (_, l, _), p in zip(alphas, carries, ps)]
        pvs = [_dot(vt_ref[0, h * HEAD_DIM:(h + 1) * HEAD_DIM, pl.ds(start, tk)], p.astype(BF16))
               for (h, _), p in zip(pairs, ps)]
        accs = [a * acc + pv for a, (_, _, acc), pv in zip(alphas, carries, pvs)]
        return tuple(zip(m_news, ls, accs))

    all_pairs = [(h, mp) for h in range(HEADS) for mp in range(n_maps)]
    qt_all = qt_ref[0]
    row_chain = _group_id((C, tq), 0, width)
    normalized = {}
    for g0 in range(0, len(all_pairs), group):
        pairs = all_pairs[g0:g0 + group]
        qts = [jnp.where(row_chain == h * n_maps + mp, qt_all, jnp.zeros_like(qt_all)) for h, mp in pairs]
        init = tuple((jnp.full((1, tq), NEG_BIG, F32), jnp.zeros((1, tq), F32), jnp.zeros((HEAD_DIM, tq), F32))
                     for _ in pairs)
        carries = lax.fori_loop(0, n_full, lambda j, c: step(pairs, qts, j, c, False), init)
        for dj in range(n_diag):
            carries = step(pairs, qts, n_full + dj, carries, True)
        for pair, (_, l, acc) in zip(pairs, carries):
            normalized[pair] = acc / l
    outs = []
    for h in range(HEADS):
        if n_maps == 1:
            outs.append(normalized[(h, 0)])
        else:
            d = normalized[(h, 0)] - lam_ref[...] * normalized[(h, 1)]
            ms = jnp.mean(d * d, axis=0, keepdims=True)
            outs.append(d * lax.rsqrt(ms + NORM_EPS) * gsub_ref[...] * sub_scale)
    o_ref[...] = jnp.concatenate(outs, axis=0).T.astype(o_ref.dtype)


def _flash_attention_t(qt, k, vt, *, n_maps, tq, tk, group=4, ck=None, lam=None, gsub=None, sub_scale=1.0, name):
    B, C, T = qt.shape
    nq = T // tq
    operands = [qt, k, vt]
    in_specs = [pl.BlockSpec((1, C, tq), lambda b, i: (b, 0, i)),
                pl.BlockSpec((T, C), lambda b, i: (b, 0)),
                pl.BlockSpec((1, C, T), lambda b, i: (b, 0, 0))]
    if ck is not None:
        operands.append(ck)
        in_specs.append(pl.BlockSpec((T, LANES), lambda b, i: (b, 0)))
    if n_maps > 1:
        operands += [lam.reshape(1, 1), gsub.reshape(HEAD_DIM, 1)]
        in_specs += [_resident((1, 1)), _resident((HEAD_DIM, 1))]
    return pl.pallas_call(
        functools.partial(_flash_t_body, n_maps=n_maps, use_bias=ck is not None, tq=tq, tk=tk, group=group,
                          sub_scale=sub_scale),
        grid=(B, nq),
        in_specs=in_specs,
        out_specs=pl.BlockSpec((tq, C), lambda b, i: (b * nq + i, 0)),
        out_shape=jax.ShapeDtypeStruct((B * T, C), BF16),
        compiler_params=_params("arbitrary", "arbitrary"),
        name=name,
    )(*operands)


def _pad_lanes(x, width=LANES):
    return jnp.pad(x, [(0, 0)] * (x.ndim - 1) + [(0, width - x.shape[-1])])


Q_ROWS = 8


def _paged_body(pt_ref, *refs, n_pages, n_maps, use_bias, scale, sub_scale, chains):
    del pt_ref
    refs = list(refs)
    q_ref = refs.pop(0)
    lam_ref, gsub_ref = (None, None) if use_bias else (refs.pop(0), refs.pop(0))
    knew_ref, vnew_ref = refs.pop(0), refs.pop(0)
    lfnew_ref = refs.pop(0) if use_bias else None
    k_refs, refs = refs[:n_pages], refs[n_pages:]
    v_refs, refs = refs[:n_pages], refs[n_pages:]
    lf_refs, refs = (refs[:n_pages], refs[n_pages:]) if use_bias else (None, refs)
    o_ref, m_ref, l_ref, acc_ref = refs[:4]
    carry_ref = refs[4] if use_bias else None
    g = pl.program_id(1)
    R = HEADS * n_maps * Q_ROWS
    C = HEADS * HEAD_DIM

    @pl.when(g == 0)
    def _init():
        m_ref[...] = jnp.full(m_ref.shape, NEG_BIG, F32)
        l_ref[...] = jnp.zeros(l_ref.shape, F32)
        acc_ref[...] = jnp.zeros(acc_ref.shape, F32)
        if use_bias:
            carry_ref[...] = jnp.zeros(carry_ref.shape, F32)

    q = q_ref[0]

    def head_rows(x):
        return jnp.concatenate([jnp.broadcast_to(x[h:h + 1], (Q_ROWS, x.shape[1])) for h in range(HEADS)], axis=0)

    def attend(kts, vts, lfs, causal_new, splits=1):
        n_all = len(kts)
        bias = None
        if use_bias:
            upper = jnp.where(lax.broadcasted_iota(jnp.int32, (LANES, LANES), 0)
                              <= lax.broadcasted_iota(jnp.int32, (LANES, LANES), 1), 1.0, 0.0).astype(BF16)
            rows = HEADS * Q_ROWS
            n = len(lfs)
            within = _dot_sel(jnp.concatenate([head_rows(lf) for lf in lfs], axis=0), upper)
            incl = jnp.broadcast_to(within[:, LANES - 1:LANES], within.shape)
            d = 1
            while d < n:
                incl = incl + jnp.concatenate([jnp.zeros((d * rows, LANES), F32), incl[:(n - d) * rows]], axis=0)
                d *= 2
            carry = carry_ref[...]
            before = jnp.concatenate([jnp.zeros((rows, LANES), F32), incl[:(n - 1) * rows]], axis=0) if n > 1 else 0.0
            c_all = within + before
            carry_ref[...] = carry + incl[(n - 1) * rows:]
            bias = [c_all[i * rows:(i + 1) * rows] + carry for i in range(n)]
        per = n_all // splits
        parts = []
        for c in range(splits):
            sl = slice(c * per, (c + 1) * per)
            kt = jnp.concatenate(kts[sl], axis=1).astype(BF16)
            s = _dot(q, kt) * scale
            if use_bias:
                s = s - jnp.concatenate(bias[sl], axis=1)
            if causal_new:
                t = jnp.bitwise_and(lax.broadcasted_iota(jnp.int32, s.shape, 0), Q_ROWS - 1)
                s = jnp.where(lax.broadcasted_iota(jnp.int32, s.shape, 1) <= t, s, NEG_BIG)
            m_c = jnp.max(s, axis=1, keepdims=True)
            p = jnp.exp(s - m_c)
            vt = jnp.concatenate(vts[sl], axis=1).astype(BF16)
            parts.append((m_c, jnp.sum(p, axis=1, keepdims=True), _dot_nt(p.astype(BF16), vt)))
        m_prev = m_ref[...]
        m_new = m_prev
        for m_c, _, _ in parts:
            m_new = jnp.maximum(m_new, m_c)
        alpha = jnp.exp(m_prev - m_new)
        l_new = alpha * l_ref[...]
        acc_new = alpha * acc_ref[...]
        for m_c, l_c, pv_c in parts:
            w_c = jnp.exp(m_c - m_new)
            l_new = l_new + w_c * l_c
            acc_new = acc_new + w_c * pv_c
        l_ref[...] = l_new
        acc_ref[...] = acc_new
        m_ref[...] = m_new

    attend([r[0, 0] for r in k_refs], [r[0, 0] for r in v_refs], [r[0, 0] for r in lf_refs] if use_bias else None, False,
           splits=chains)

    @pl.when(g == pl.num_programs(1) - 1)
    def _finish():
        attend([knew_ref[0]], [vnew_ref[0]], [lfnew_ref[0]] if use_bias else None, True)
        full = acc_ref[...] / l_ref[...]
        lane_head = _group_id((Q_ROWS, C), 1, HEAD_DIM)
        outs = []
        for mp in range(n_maps):
            o = jnp.zeros((Q_ROWS, C), F32)
            for h in range(HEADS):
                r0 = (h * n_maps + mp) * Q_ROWS
                o = o + jnp.where(lane_head == h, full[r0:r0 + Q_ROWS], 0.0)
            outs.append(o)
        if n_maps == 1:
            res = outs[0]
        else:
            d = outs[0] - lam_ref[...] * outs[1]
            head_sum = jnp.where(_group_id((C, C), 0, HEAD_DIM) == _group_id((C, C), 1, HEAD_DIM), 1.0, 0.0).astype(BF16)
            ms = _dot_sel(d * d, head_sum) * (1.0 / HEAD_DIM)
            res = d * lax.rsqrt(ms + NORM_EPS) * gsub_ref[...] * sub_scale
        o_ref[0] = res.astype(o_ref.dtype)


def _paged_attention(q_rows, k_cache_t, v_cache_t, knew_t, vnew_t, page_table, layer, *, pages_per_step, n_maps, scale,
                     lf_cache_t=None, lfnew_t=None, lam=None, gsub=None, sub_scale=1.0, name):
    B, R, C = q_rows.shape
    n_pages_total = page_table.shape[1]
    P = pages_per_step
    use_bias = lf_cache_t is not None
    per_seq = lambda shape: pl.BlockSpec((1,) + shape, lambda b, g, pt: (b,) + (0,) * len(shape))
    const = lambda shape: pl.BlockSpec(shape, lambda b, g, pt: (0,) * len(shape))

    def page_spec(i, tail):
        return pl.BlockSpec((1, 1) + tail, lambda b, g, pt: (layer, pt[b, g * P + i]) + (0,) * len(tail))

    operands = [q_rows]
    in_specs = [per_seq((R, C))]
    if not use_bias:
        operands += [lam.reshape(1, 1), jnp.tile(gsub, HEADS).reshape(1, C)]
        in_specs += [const((1, 1)), const((1, C))]
    operands += [knew_t, vnew_t]
    in_specs += [per_seq((C, LANES)), per_seq((C, LANES))]
    if use_bias:
        operands.append(lfnew_t)
        in_specs.append(per_seq((HEADS, LANES)))
    kv_tail = (C, LANES)
    operands += [k_cache_t] * P + [v_cache_t] * P
    in_specs += [page_spec(i, kv_tail) for i in range(P)] + [page_spec(i, kv_tail) for i in range(P)]
    scratch = [pltpu.VMEM((R, 1), F32), pltpu.VMEM((R, 1), F32), pltpu.VMEM((R, C), F32)]
    if use_bias:
        operands += [lf_cache_t] * P
        in_specs += [page_spec(i, (HEADS, LANES)) for i in range(P)]
        scratch.append(pltpu.VMEM((HEADS * Q_ROWS, LANES), F32))
    return pl.pallas_call(
        functools.partial(_paged_body, n_pages=P, n_maps=n_maps, use_bias=use_bias, scale=scale, sub_scale=sub_scale,
                          chains=1),
        grid_spec=pltpu.PrefetchScalarGridSpec(
            num_scalar_prefetch=1, grid=(B, n_pages_total // P), in_specs=in_specs,
            out_specs=pl.BlockSpec((1, Q_ROWS, C), lambda b, g, pt: (b, 0, 0)), scratch_shapes=scratch),
        out_shape=jax.ShapeDtypeStruct((B, Q_ROWS, C), F32),
        compiler_params=_params("arbitrary", "arbitrary"),
        name=name,
    )(page_table, *operands)


def _query_rows(q, n_maps):
    B, T, H, M, d = q.shape
    eye = jnp.eye(H * M, dtype=q.dtype).reshape(H, M, H, M)
    rows = jnp.einsum('bthmd,hmgn->bhmtgnd', q, eye)
    rows = jnp.pad(rows, ((0, 0), (0, 0), (0, 0), (0, Q_ROWS - T), (0, 0), (0, 0), (0, 0)))
    return rows.reshape(B, H * M * Q_ROWS, H * M * d).astype(BF16)


def _new_page(x):
    return jnp.pad(jnp.swapaxes(x, 1, 2), ((0, 0), (0, 0), (0, LANES - x.shape[1])))


def _gelu_tanh(x):
    return 0.5 * x * (1.0 + jnp.tanh(math.sqrt(2.0 / math.pi) * (x + 0.044715 * (x * x * x))))


def _lru_body(xc_ref, lg_ref, wa_ref, wx_ref, ba_ref, bx_ref, sp_ref, h0_ref, o_ref, ht_ref, carry_ref, *,
              rows, last_tile, last_row):
    t = pl.program_id(1)

    @pl.when(t == 0)
    def _init():
        carry_ref[...] = h0_ref[0]

    xc = xc_ref[...]
    xb = xc.astype(BF16)
    rg = _sigmoid(_dot(xb, wa_ref[...]) + ba_ref[...])
    ig = _sigmoid(_dot(xb, wx_ref[...]) + bx_ref[...])
    log_a = (-LRU_C) * rg * sp_ref[...]
    a = jnp.exp(log_a)
    b = jnp.sqrt(1.0 - jnp.exp(2.0 * log_a)) * (ig * xc)
    row = lax.broadcasted_iota(jnp.int32, a.shape, 0)
    d = 1
    while d < rows:
        a_prev = jnp.where(row >= d, pltpu.roll(a, d, 0), 1.0)
        b_prev = jnp.where(row >= d, pltpu.roll(b, d, 0), 0.0)
        b = a * b_prev + b
        a = a * a_prev
        d *= 2
    hs = b + a * carry_ref[...]
    carry_ref[...] = hs[rows - 1:rows]
    o_ref[...] = (hs * _gelu_tanh(lg_ref[...])).astype(o_ref.dtype)

    @pl.when(t == last_tile)
    def _state():
        ht_ref[0] = hs[last_row:last_row + 1]


def _lru(xc, lg, wa_bd, wx_bd, ba, bx, sp, h0, *, rows, t_valid, name):
    B, T, W = xc.shape
    nt = T // rows
    vec = lambda a: a.reshape(1, W)
    out, ht = pl.pallas_call(
        functools.partial(_lru_body, rows=rows, last_tile=(t_valid - 1) // rows, last_row=(t_valid - 1) % rows),
        grid=(B, nt),
        in_specs=[pl.BlockSpec((rows, W), lambda b, t: (b * nt + t, 0)),
                  pl.BlockSpec((rows, W), lambda b, t: (b * nt + t, 0)),
                  _resident((W, W)), _resident((W, W)), _resident((1, W)), _resident((1, W)), _resident((1, W)),
                  pl.BlockSpec((1, 1, W), lambda b, t: (b, 0, 0))],
        out_specs=[pl.BlockSpec((rows, W), lambda b, t: (b * nt + t, 0)),
                   pl.BlockSpec((1, 1, W), lambda b, t: (b, 0, 0))],
        out_shape=[jax.ShapeDtypeStruct((B * T, W), BF16), jax.ShapeDtypeStruct((B, 1, W), F32)],
        scratch_shapes=[pltpu.VMEM((1, W), F32)],
        compiler_params=_params("arbitrary", "arbitrary"),
        name=name,
    )(xc.reshape(B * T, W), lg.reshape(B * T, W), wa_bd, wx_bd, vec(ba), vec(bx), vec(sp), h0.reshape(B, 1, W))
    return out, ht.reshape(B, W)


def _gdn_chunk(qs, ks, vs, gbs, Ss, chunk, heads, head0s):
    each = lambda f, *ls: [f(*a) for a in zip(*ls)]
    R = heads * chunk
    SD = heads * HEAD_DIM
    r0 = lax.broadcasted_iota(jnp.int32, (R, R), 0)
    r1 = lax.broadcasted_iota(jnp.int32, (R, R), 1)
    same = _group_id((R, R), 0, chunk) == _group_id((R, R), 1, chunk)
    low = same & (r0 >= r1)
    strict = same & (r0 > r1)
    low_b = jnp.where(low, 1.0, 0.0).astype(BF16)
    same_b = jnp.where(same, 1.0, 0.0).astype(BF16)
    eye_b = jnp.where(r0 == r1, 1.0, 0.0).astype(BF16)

    colmat = each(lambda gb, h0: jnp.concatenate(
        [gb if h == 0 else pltpu.roll(gb, LANES - h, 1) for h in range(h0, h0 + heads)], axis=0), gbs, head0s)
    cparts = each(_split3, colmat)
    sel3 = lambda sel, p: _dot(sel, p[0]) + (_dot(sel, p[1]) + _dot(sel, p[2]))
    cum_cols = each(lambda p: sel3(low_b, p), cparts)
    tot_cols = each(lambda p: sel3(same_b, p), cparts)
    gcum_col = each(lambda c: c[:, 0:1], cum_cols)
    beta_col = each(lambda c: c[:, HEADS:HEADS + 1], colmat)
    gtot_col = each(lambda c: c[:, 0:1], tot_cols)
    cum_parts = each(_split3, cum_cols)
    gcum_row = each(lambda p: (_dot_tn(p[0], eye_b) + (_dot_tn(p[1], eye_b) + _dot_tn(p[2], eye_b)))[0:1, :], cum_parts)

    decay = each(lambda c, r: jnp.where(low, jnp.exp(jnp.where(low, c - r, 0.0)), 0.0), gcum_col, gcum_row)
    kb = each(lambda k: k.astype(BF16), ks)
    kk = each(lambda b: _dot_nt(b, b), kb)
    qk = each(lambda q, b: _dot_nt(q.astype(BF16), b), qs, kb)
    nmat = each(lambda a, d, b: jnp.where(strict, a * d * b, 0.0), kk, decay, beta_col)
    xs = each(lambda v, k, b, c: jnp.concatenate([v * b, k * (b * jnp.exp(c))], axis=1), vs, ks, beta_col, gcum_col)
    pw = each(_split2, nmat)
    xs = each(lambda x, p: x - _dot3(p[0], p[1], *_split2(x)), xs, pw)
    width = 2
    while width < chunk:
        pw = each(lambda p: _split2(_dot3(p[0], p[1], p[0], p[1])), pw)
        xs = each(lambda x, p: x + _dot3(p[0], p[1], *_split2(x)), xs, pw)
        width *= 2
    u = each(lambda x: x[:, 0:HEAD_DIM], xs)
    w = each(lambda x: pltpu.roll(x, HEAD_DIM, 1)[:, 0:HEAD_DIM], xs)

    head_match = _group_id((R, SD), 0, chunk) == _group_id((R, SD), 1, HEAD_DIM)

    def spread(x):
        return jnp.where(head_match, jnp.concatenate([x] * heads, axis=1), 0.0).astype(BF16)

    sb = each(lambda S: S.astype(BF16), Ss)
    v_new = each(lambda u_, w_, s: u_ - _dot(spread(w_), s), u, w, sb)
    vb = each(lambda x: x.astype(BF16), v_new)
    o = each(lambda q, c, s, a, d, x: _dot(spread(q * jnp.exp(c)), s) + _dot((a * d).astype(BF16), x),
             qs, gcum_col, sb, qk, decay, vb)
    kg = each(lambda k, t, c: spread(k * jnp.exp(t - c)), ks, gtot_col, gcum_col)
    tok = jnp.bitwise_and(lax.broadcasted_iota(jnp.int32, (R, SD), 0), chunk - 1)
    first = jnp.where(head_match & (tok == 0), 1.0, 0.0).astype(BF16)
    gtot_state = each(lambda p: _dot_tn(first, p[0]) + (_dot_tn(first, p[1]) + _dot_tn(first, p[2])),
                      each(_split3, tot_cols))
    s_new = each(lambda S, t, g, x: S * jnp.exp(t[:, 0:1]) + _dot_tn(g, x), Ss, gtot_state, kg, vb)
    return o, s_new


def _gdn_body(q_ref, k_ref, v_ref, gb_ref, s0_ref, o_ref, sT_ref, s_ref, *, chunk, group):
    c = pl.program_id(1)

    @pl.when(c == 0)
    def _init():
        s_ref[...] = s0_ref[...]

    hpg = min(HEADS, LANES // chunk)
    R = hpg * chunk
    probs = [(g, h0) for g in range(group) for h0 in range(0, HEADS, hpg)]
    hsl = lambda h0: slice(h0, h0 + hpg)
    rsl = lambda h0: slice(h0 * HEAD_DIM, (h0 + hpg) * HEAD_DIM)
    o, s_new = _gdn_chunk([q_ref[g, hsl(h0)].reshape(R, HEAD_DIM) for g, h0 in probs],
                          [k_ref[g, hsl(h0)].reshape(R, HEAD_DIM) for g, h0 in probs],
                          [v_ref[g, hsl(h0)].reshape(R, HEAD_DIM) for g, h0 in probs],
                          [gb_ref[g] for g, _ in probs], [s_ref[g, rsl(h0)] for g, h0 in probs], chunk, hpg,
                          [h0 for _, h0 in probs])
    for i, (g, h0) in enumerate(probs):
        o_ref[g, hsl(h0)] = o[i].reshape(hpg, chunk, HEAD_DIM)
        s_ref[g, rsl(h0)] = s_new[i]

    @pl.when(c == pl.num_programs(1) - 1)
    def _state():
        sT_ref[...] = s_ref[...]


def _gdn(q, k, v, gb, S0, *, chunk, group, name):
    B, H, T, hd = q.shape
    nc = T // chunk
    SD = H * hd
    tok = pl.BlockSpec((group, H, chunk, hd), lambda b, c: (b, 0, c, 0))
    st = pl.BlockSpec((group, SD, hd), lambda b, c: (b, 0, 0))
    return pl.pallas_call(
        functools.partial(_gdn_body, chunk=chunk, group=group),
        grid=(B // group, nc),
        in_specs=[tok, tok, tok, pl.BlockSpec((group, chunk, LANES), lambda b, c: (b, c, 0)), st],
        out_specs=[tok, st],
        out_shape=[jax.ShapeDtypeStruct((B, H, T, hd), F32), jax.ShapeDtypeStruct((B, SD, hd), F32)],
        scratch_shapes=[pltpu.VMEM((group, SD, hd), F32)],
        compiler_params=_params("arbitrary", "arbitrary"),
        name=name,
    )(q, k, v, gb, S0)


def _log1p_exp_neg_abs(x):
    return jnp.log(1.0 + jnp.exp(-jnp.abs(x)))


def _conv_tap_sum(x, tail, w):
    rows = x.shape[0]
    first = lax.broadcasted_iota(jnp.int32, (8, x.shape[1]), 0)
    y = x * w[CONV_W - 1:CONV_W]
    for back in range(1, CONV_W):
        rolled = pltpu.roll(x, back, 0)
        top = jnp.where(first < back, pltpu.roll(tail, back, 0), rolled[0:8])
        y = y + jnp.concatenate([top, rolled[8:rows]], axis=0) * w[CONV_W - 1 - back:CONV_W - back]
    return y


def _prep_body(fq_ref, fk_ref, fv_ref, lx_ref, dq_ref, dk_ref, dv_ref, gq_ref, gk_ref, gv_ref, sm_ref,
               fqg_ref, fkg_ref, fb_ref, dqg_ref, dkg_ref, cos_ref, sin_ref, gw_ref, lw_ref, lb_ref, ga_ref, gdt_ref,
               fqt_o, fk_o, fvt_o, fkT_o, fvT_o, flT_o, ck_o, dqt_o, dk_o, dvt_o, dkT_o, dvT_o,
               gq_o, gk_o, gv_o, gb_o, xc_o, lst_o, gst_o, ltail, gtail, ccarry, *, rows):
    t = pl.program_id(1)
    C = HEADS * HEAD_DIM

    @pl.when(t == 0)
    def _init():
        ltail[...] = jnp.zeros(ltail.shape, F32)
        gtail[...] = jnp.zeros(gtail.shape, F32)
        ccarry[...] = jnp.zeros(ccarry.shape, F32)

    def group_sum(width):
        return jnp.where(_group_id((C, C), 0, width) == _group_id((C, C), 1, width), 1.0, 0.0).astype(BF16)

    sum64, sum32 = group_sum(HEAD_DIM), group_sum(DIFF_HD)

    def group_rms(x, sel, width):
        return x * lax.rsqrt(_dot_sel(x * x, sel) * (1.0 / width) + NORM_EPS)

    kn = group_rms(fk_ref[...], sum64, FOX_HD) * fkg_ref[...]
    qn = group_rms(fq_ref[...], sum64, FOX_HD) * fqg_ref[...] * (1.0 / math.sqrt(FOX_HD))
    fv = fv_ref[...]
    knT, fvT = kn.T, fv.T
    fqt_o[0] = qn.T.astype(BF16)
    fk_o[...] = kn.astype(BF16)
    fvt_o[0] = fvT.astype(BF16)
    fkT_o[0] = knT
    fvT_o[0] = fvT
    sm = sm_ref[...]
    xf = sm + fb_ref[...]
    logf = jnp.minimum(xf, 0.0) - _log1p_exp_neg_abs(xf)
    flT_o[0] = logf.T[0:8]
    lower = jnp.where(lax.broadcasted_iota(jnp.int32, (rows, rows), 0) >= lax.broadcasted_iota(jnp.int32, (rows, rows), 1),
                      1.0, 0.0).astype(BF16)
    c = _sel_dot(lower, logf) + ccarry[...]
    ccarry[...] = c[rows - 1:rows]
    ck_o[...] = c

    lane = lax.broadcasted_iota(jnp.int32, (rows, LANES), 1)
    first_half = jnp.bitwise_and(lane, DIFF_HD - 1) < DIFF_HD // 2

    def swap_halves(x):
        parts = []
        for c0 in range(0, C, LANES):
            h = x[:, c0:c0 + LANES]
            parts.append(jnp.where(first_half, pltpu.roll(h, LANES - DIFF_HD // 2, 1), pltpu.roll(h, DIFF_HD // 2, 1)))
        return jnp.concatenate(parts, axis=1)

    def rotary(x):
        return x * cos_ref[...] + swap_halves(x) * sin_ref[...]

    dkr = rotary(group_rms(dk_ref[...], sum32, DIFF_HD) * dkg_ref[...])
    dqr = rotary(group_rms(dq_ref[...], sum32, DIFF_HD) * dqg_ref[...]) * (1.0 / math.sqrt(DIFF_HD))
    dv = dv_ref[...]
    dkrT, dvT = dkr.T, dv.T
    dqt_o[0] = dqr.T.astype(BF16)
    dk_o[...] = dkr.astype(BF16)
    dvt_o[0] = dvT.astype(BF16)
    dkT_o[0] = dkrT
    dvT_o[0] = dvT

    gw = gw_ref[...]
    raws = (gq_ref[...], gk_ref[...], gv_ref[...])
    acts = []
    for i, raw in enumerate(raws):
        y = _conv_tap_sum(raw, gtail[:, i * C:(i + 1) * C], gw[:, i * C:(i + 1) * C])
        acts.append(y * _sigmoid(y))
    gqn = acts[0] * lax.rsqrt(_dot_sel(acts[0] * acts[0], sum64) + NORM_EPS) * (GDN_HD ** -0.5)
    gkn = acts[1] * lax.rsqrt(_dot_sel(acts[1] * acts[1], sum64) + NORM_EPS)
    pick_r = lax.broadcasted_iota(jnp.int32, (C, HEAD_DIM), 0)
    pick_c = lax.broadcasted_iota(jnp.int32, (C, HEAD_DIM), 1)
    for h in range(HEADS):
        pick = jnp.where(pick_r == pick_c + h * HEAD_DIM, 1.0, 0.0).astype(BF16)
        gq_o[0, h] = _dot_sel(gqn, pick)
        gk_o[0, h] = _dot_sel(gkn, pick)
        gv_o[0, h] = _dot_sel(acts[2], pick)
    sh = pltpu.roll(sm, LANES - HEADS, 1)
    xa = sh + gdt_ref[...]
    decay = -ga_ref[...] * (jnp.maximum(xa, 0.0) + _log1p_exp_neg_abs(xa))
    gb_o[...] = jnp.where(lane < HEADS, decay, _sigmoid(sh))

    lx = lx_ref[...]
    xc_o[...] = _conv_tap_sum(lx, ltail[...], lw_ref[...]) + lb_ref[...]

    ltail[...] = lx[rows - 8:rows]
    gtail[...] = jnp.concatenate([r[rows - 8:rows] for r in raws], axis=1)
    lst_o[0] = ltail[...]
    gst_o[0] = gtail[...]


def _prep(zm2, B, T, p, cos_t, sin_t, *, rows, name):
    M = B * T
    nt = T // rows
    C = HEADS * HEAD_DIM
    colblk = lambda j, w=C: pl.BlockSpec((rows, w), lambda b, t: (b * nt + t, j))
    vec = lambda a, w=C: jnp.pad(a.reshape(1, -1), ((0, 0), (0, w - a.size)))
    tile4 = lambda g: jnp.tile(g, C // g.size).reshape(1, C)
    in_specs = ([colblk(j) for j in (0, 1, 2, 3, 5, 6, 7, 8, 9, 10)] + [colblk(_C_SMALL // LANES, LANES)]
                + [_resident((1, C))] * 2 + [_resident((1, LANES))] + [_resident((1, C))] * 2
                + [pl.BlockSpec((rows, C), lambda b, t: (t, 0))] * 2
                + [_resident((CONV_W, 3 * C)), _resident((CONV_W, C)), _resident((1, C)), _resident((1, LANES)),
                   _resident((1, LANES))])
    chan = lambda dt: (jax.ShapeDtypeStruct((B, C, T), dt), pl.BlockSpec((1, C, rows), lambda b, t: (b, 0, t)))
    tokm = lambda w, dt: (jax.ShapeDtypeStruct((M, w), dt), pl.BlockSpec((rows, w), lambda b, t: (b * nt + t, 0)))
    headm = (jax.ShapeDtypeStruct((B, HEADS, T, HEAD_DIM), F32),
             pl.BlockSpec((1, HEADS, rows, HEAD_DIM), lambda b, t: (b, 0, t, 0)))
    state = lambda w: (jax.ShapeDtypeStruct((B, 8, w), F32), pl.BlockSpec((1, 8, w), lambda b, t: (b, 0, 0)))
    outs = [chan(BF16), tokm(C, BF16), chan(BF16), chan(F32), chan(F32),
            (jax.ShapeDtypeStruct((B, 8, T), F32), pl.BlockSpec((1, 8, rows), lambda b, t: (b, 0, t))), tokm(LANES, F32),
            chan(BF16), tokm(C, BF16), chan(BF16), chan(F32), chan(F32),
            headm, headm, headm, tokm(LANES, F32), tokm(C, F32), state(C), state(3 * C)]
    names = ['fqt', 'fk', 'fvt', 'fkT', 'fvT', 'flT', 'ck', 'dqt', 'dk', 'dvt', 'dkT', 'dvT', 'gq', 'gk', 'gv', 'gb', 'xc',
             'lru_tail', 'gdn_tail']
    res = pl.pallas_call(
        functools.partial(_prep_body, rows=rows),
        grid=(B, nt),
        in_specs=in_specs,
        out_specs=[o[1] for o in outs],
        out_shape=[o[0] for o in outs],
        scratch_shapes=[pltpu.VMEM((8, C), F32), pltpu.VMEM((8, 3 * C), F32), pltpu.VMEM((1, LANES), F32)],
        compiler_params=_params("arbitrary", "arbitrary"),
        name=name,
    )(*([zm2] * 11), tile4(p['fox_q_norm']), tile4(p['fox_k_norm']), vec(p['fox_b_f'], LANES),
      tile4(p['diff_q_norm']), tile4(p['diff_k_norm']), cos_t, sin_t, p['gdn_conv_w'], p['lru_conv_w'],
      p['lru_conv_b'].reshape(1, C), vec(jnp.exp(p['gdn_A_log']), LANES), vec(p['gdn_dt_bias'], LANES))
    return dict(zip(names, res))


def _rope_tables(pos):
    inv = ROPE_THETA ** (-jnp.arange(0, DIFF_HD, 2, dtype=F32) / DIFF_HD)
    ang = pos.astype(F32)[:, None] * inv[None, :]
    cos = jnp.tile(jnp.cos(ang), (1, 2 * BR_WIDTH // DIFF_HD))
    sin = jnp.tile(jnp.concatenate([-jnp.sin(ang), jnp.sin(ang)], axis=1), (1, BR_WIDTH // DIFF_HD))
    return cos, sin


def _merge_body(x_ref, b0_ref, b1_ref, b2_ref, b3_ref, gates_ref, wb_ref, wo_ref, o_ref):
    merged = None
    for n, b_ref in enumerate((b0_ref, b1_ref, b2_ref, b3_ref)):
        proj = _dot(b_ref[...], wb_ref[n])
        term = _sigmoid(gates_ref[:, n * D_MODEL:(n + 1) * D_MODEL]) * proj
        merged = term if merged is None else merged + term
    o_ref[...] = x_ref[...] + _dot(merged.astype(BF16), wo_ref[...])


def _merge(x2, branches, gates, wb, wo, *, row_tile, name):
    M, D = x2.shape
    row = lambda w: pl.BlockSpec((row_tile, w), lambda i: (i, 0))
    return pl.pallas_call(
        _merge_body,
        grid=(M // row_tile,),
        in_specs=[row(D)] + [row(BR_WIDTH)] * N_BRANCH + [row(N_BRANCH * D), _resident(wb.shape), _resident(wo.shape)],
        out_specs=row(D),
        out_shape=jax.ShapeDtypeStruct((M, D), F32),
        compiler_params=_params("arbitrary"),
        name=name,
    )(x2, *branches, gates, wb, wo)


def _mem_body(x_ref, g_ref, wq_ref, qg_ref, mk_ref, mv_ref, wo_ref, o_ref):
    x = x_ref[0]
    h = (x * _rms_scale(x) * g_ref[...]).astype(BF16)
    q = _dot(h, wq_ref[...])
    W = q.shape[1]
    head_sum = jnp.where(_group_id((W, W), 0, MEM_HD) == _group_id((W, W), 1, MEM_HD), 1.0, 0.0).astype(BF16)
    ms = _dot_sel(q * q, head_sum) * (1.0 / MEM_HD)
    qn = q * lax.rsqrt(ms + NORM_EPS) * qg_ref[...] * (1.0 / math.sqrt(MEM_HD))
    mk = mk_ref[0].astype(BF16)
    mv = mv_ref[0].astype(BF16)
    lane_head = _group_id(q.shape, 1, MEM_HD)
    o = jnp.zeros(q.shape, F32)
    for hh in range(MEM_HEADS):
        s = _dot_nt(jnp.where(lane_head == hh, qn, 0.0).astype(BF16), mk)
        p = jnp.exp(s - jnp.max(s, axis=1, keepdims=True))
        p = p / jnp.sum(p, axis=1, keepdims=True)
        o = o + jnp.where(lane_head == hh, _dot(p.astype(BF16), mv), 0.0)
    o_ref[0] = x + _dot(o.astype(BF16), wo_ref[...])


def _mem_sublayer(x, g, wq, qg, mk, mv, wo, *, row_tile, name):
    B, T, D = x.shape
    nt = T // row_tile
    kv = pl.BlockSpec((1,) + mk.shape[1:], lambda b, t: (b, 0, 0))
    return pl.pallas_call(
        _mem_body,
        grid=(B, nt),
        in_specs=[pl.BlockSpec((1, row_tile, D), lambda b, t: (b, t, 0)), _resident((1, D)), _resident(wq.shape),
                  _resident((1, MEM_WIDTH)), kv, kv, _resident(wo.shape)],
        out_specs=pl.BlockSpec((1, row_tile, D), lambda b, t: (b, t, 0)),
        out_shape=jax.ShapeDtypeStruct((B, T, D), F32),
        compiler_params=_params("arbitrary", "arbitrary"),
        name=name,
    )(x, g.reshape(1, D), wq, jnp.tile(qg, MEM_HEADS).reshape(1, MEM_WIDTH), mk, mv, wo)


def _mlp_body(x_ref, g_ref, wu_ref, wd_ref, o_ref, *, ff_chunk):
    x = x_ref[...]
    h = (x * _rms_scale(x) * g_ref[...]).astype(BF16)
    acc = x
    for c in range(D_FF // ff_chunk):
        u = _dot(h, wu_ref[:, c * ff_chunk:(c + 1) * ff_chunk])
        u = jnp.square(jnp.maximum(u, 0.0)).astype(BF16)
        acc = acc + _dot(u, wd_ref[c * ff_chunk:(c + 1) * ff_chunk, :])
    o_ref[...] = acc


def _mlp_sublayer(x2, g, w_up, w_down, *, row_tile, name):
    M, D = x2.shape
    return pl.pallas_call(
        functools.partial(_mlp_body, ff_chunk=1024),
        grid=(M // row_tile,),
        in_specs=[pl.BlockSpec((row_tile, D), lambda i: (i, 0)), _resident((1, D)), _resident(w_up.shape),
                  _resident(w_down.shape)],
        out_specs=pl.BlockSpec((row_tile, D), lambda i: (i, 0)),
        out_shape=jax.ShapeDtypeStruct((M, D), F32),
        compiler_params=_params("arbitrary"),
        name=name,
    )(x2, g.reshape(1, D), w_up, w_down)


def _rmsnorm(x, g):
    xf = x.astype(F32)
    return xf * lax.rsqrt(jnp.mean(xf * xf, axis=-1, keepdims=True) + NORM_EPS) * g.astype(F32)


def _l2norm(x):
    return x * lax.rsqrt(jnp.sum(x * x, axis=-1, keepdims=True) + NORM_EPS)


def _rope(x, pos):
    d = x.shape[-1]
    inv = ROPE_THETA ** (-jnp.arange(0, d, 2, dtype=F32) / d)
    ang = pos.astype(F32)[:, None] * inv[None, :]
    ang = ang.reshape((1, ang.shape[0]) + (1,) * (x.ndim - 3) + (d // 2,))
    cos, sin = jnp.cos(ang), jnp.sin(ang)
    x1, x2 = x[..., : d // 2], x[..., d // 2:]
    return jnp.concatenate([x1 * cos - x2 * sin, x2 * cos + x1 * sin], axis=-1)


def _causal_conv(x, buf, w, b=None):
    xp = jnp.concatenate([buf.astype(x.dtype), x], axis=1)
    T = x.shape[1]
    y = xp[:, 0:T] * w[0]
    for i in range(1, CONV_W):
        y = y + xp[:, i:i + T] * w[i]
    if b is not None:
        y = y + b
    return y, xp[:, -(CONV_W - 1):]


def _key_minor_pages(cache):
    nd = cache.ndim
    t = jnp.transpose(cache, (0, 1) + tuple(range(3, nd)) + (2,))
    return t.reshape(cache.shape[:2] + (-1, cache.shape[2]))


def _head_major(x):
    return jnp.swapaxes(x, 1, 2)


def _pad_time(x, t_pad, axis=1):
    pad = [(0, 0)] * x.ndim
    pad[axis] = (0, t_pad - x.shape[axis])
    return jnp.pad(x, pad)


def _pack_layer_weights(p):
    o = np.cumsum((0,) + (3 * BR_WIDTH, FOX_HEADS, LRU_WIDTH, LRU_WIDTH, 3 * BR_WIDTH, 3 * BR_WIDTH, GDN_HEADS,
                          GDN_HEADS, BR_WIDTH, N_BRANCH * D_MODEL)).tolist()
    wt = p['w_in'].T
    seg = lambda i: wt[o[i]:o[i + 1]]
    small = jnp.concatenate([seg(1), seg(6), seg(7), jnp.zeros((SMALL_COLS - 3 * HEADS, D_MODEL), wt.dtype)], axis=0)
    w_main = jnp.concatenate([seg(0), seg(2), seg(3), seg(4), seg(5), seg(8), small], axis=0).astype(BF16)
    blockdiag = lambda wb: jax.scipy.linalg.block_diag(*[wb[i] for i in range(LRU_BLOCKS)]).astype(BF16)
    return dict(w_main=w_main, w_gates=seg(9).astype(BF16), lru_wa=blockdiag(p['lru_w_a']),
                lru_wx=blockdiag(p['lru_w_x']), w_branch=p['w_branch'].astype(BF16), w_out=p['w_out'].astype(BF16),
                w_mem_q=p['w_mem_q'].astype(BF16), w_mem_kv=p['w_mem_kv'].astype(BF16),
                w_mem_o=p['w_mem_o'].astype(BF16), w_mlp_up=p['w_mlp_up'].astype(BF16),
                w_mlp_down=p['w_mlp_down'].astype(BF16))


_C_FOX, _C_LRUX, _C_LRUG, _C_DIFF, _C_GDN, _C_GDNZ, _C_SMALL = 0, 768, 1024, 1280, 2048, 2816, 3072


def _gdn_gate(go, z, g):
    B, T = go.shape[:2]
    out = _rmsnorm(go, g) * jax.nn.silu(z.reshape(B, T, GDN_HEADS, GDN_HD))
    return out.reshape(B * T, BR_WIDTH).astype(BF16)


def _mixer_sublayer_prompt(x, p, pw, lam_init, rope_tables, *, row_tile, attn_tile, attn_chunk, prep_rows, tag):
    B, T, D = x.shape
    M = B * T
    x2 = x.reshape(M, D)
    zm, gates = _norm_matmul(x2, p['norm_mix'], [pw['w_main'], pw['w_gates']], row_tile=row_tile, name=tag + "in_proj",
                             transposed=True)
    pre = _prep(zm, B, T, p, *rope_tables, rows=prep_rows, name=tag + "prep")
    fox_out = _flash_attention_t(pre['fqt'], pre['fk'], pre['fvt'], n_maps=1, tq=attn_tile, tk=attn_chunk, ck=pre['ck'],
                                 name=tag + "fox_attn")
    lp = p['diff_lambda']
    lam = jnp.exp(jnp.sum(lp[0] * lp[1])) - jnp.exp(jnp.sum(lp[2] * lp[3])) + lam_init
    diff_out = _flash_attention_t(pre['dqt'], pre['dk'], pre['dvt'], n_maps=2, tq=attn_tile, tk=attn_chunk, lam=lam,
                                  gsub=p['diff_sub_norm'], sub_scale=1.0 - lam_init, name=tag + "diff_attn")
    zm3 = zm.reshape(B, T, MAIN_COLS)
    lru_out, lru_hT = _lru(pre['xc'].reshape(B, T, LRU_WIDTH), zm3[..., _C_LRUG:_C_LRUG + LRU_WIDTH], pw['lru_wa'],
                           pw['lru_wx'], p['lru_b_a'], p['lru_b_x'], jax.nn.softplus(-p['lru_lambda']),
                           jnp.zeros((B, LRU_WIDTH), F32), rows=256, t_valid=T, name=tag + "lru")
    go, gS = _gdn(pre['gq'], pre['gk'], pre['gv'], pre['gb'].reshape(B, T, LANES),
                  jnp.zeros((B, GDN_HEADS * GDN_HD, GDN_HD), F32), chunk=64, group=math.gcd(B, 4), name=tag + "gdn")
    gdn_out = _gdn_gate(_head_major(go), zm3[..., _C_GDNZ:_C_GDNZ + BR_WIDTH], p['gdn_out_norm'])
    x2 = _merge(x2, (fox_out, lru_out, diff_out, gdn_out), gates, pw['w_branch'], pw['w_out'], row_tile=row_tile,
                name=tag + "merge")
    token_major = lambda a: jnp.swapaxes(a, 1, 2)
    state = (token_major(pre['fkT']).reshape(B, T, FOX_HEADS, FOX_HD),
             token_major(pre['fvT']).reshape(B, T, FOX_HEADS, FOX_HD),
             token_major(pre['flT'][:, :FOX_HEADS]),
             token_major(pre['dkT']).reshape(B, T, DIFF_HEADS, 2, DIFF_HD),
             token_major(pre['dvT']).reshape(B, T, DIFF_HEADS, DIFF_VD),
             lru_hT, pre['lru_tail'][:, 8 - (CONV_W - 1):], gS.reshape(B, GDN_HEADS, GDN_HD, GDN_HD),
             pre['gdn_tail'][:, 8 - (CONV_W - 1):])
    return x2.reshape(B, T, D), state


def _mixer_sublayer(x, pos, p, pw, lam_init, fox_past, diff_past, lru_h0, lru_buf, gdn_S0, gdn_buf, *, row_tile, tag):
    B, T, D = x.shape
    M = B * T
    x2 = x.reshape(M, D)
    zm, gates = _norm_matmul(x2, p['norm_mix'], [pw['w_main'], pw['w_gates']], row_tile=row_tile, name=tag + "in_proj",
                             transposed=True)
    zm = zm.reshape(B, T, MAIN_COLS)
    seg = lambda c0, w: zm[..., c0:c0 + w]

    r = seg(_C_FOX, 3 * BR_WIDTH).reshape(B, T, 3, FOX_HEADS, FOX_HD)
    fq = _rmsnorm(r[:, :, 0], p['fox_q_norm'])
    fk = _rmsnorm(r[:, :, 1], p['fox_k_norm'])
    fv = r[:, :, 2]
    flogf = jax.nn.log_sigmoid(seg(_C_SMALL, FOX_HEADS) + p['fox_b_f'])
    fo = _paged_attention(_query_rows(fq[:, :, :, None], 1), fox_past['k'], fox_past['v'],
                          _new_page(fk.reshape(B, T, BR_WIDTH)), _new_page(fv.reshape(B, T, BR_WIDTH)),
                          fox_past['page_table'], fox_past['layer'], pages_per_step=fox_past['pages_per_step'],
                          n_maps=1, scale=1.0 / math.sqrt(FOX_HD), lf_cache_t=fox_past['logf'],
                          lfnew_t=_new_page(flogf), name=tag + "fox_paged")
    fox_out = fo[:, :T].reshape(M, BR_WIDTH).astype(BF16)

    xc, lru_buf_new = _causal_conv(seg(_C_LRUX, LRU_WIDTH), lru_buf, p['lru_conv_w'], p['lru_conv_b'])
    lru_rows = min(256, T) if T % 8 == 0 else 8
    t_pad = -(-T // lru_rows) * lru_rows
    lru_out, lru_hT = _lru(_pad_time(xc, t_pad), _pad_time(seg(_C_LRUG, LRU_WIDTH), t_pad), pw['lru_wa'], pw['lru_wx'],
                           p['lru_b_a'], p['lru_b_x'], jax.nn.softplus(-p['lru_lambda']), lru_h0,
                           rows=lru_rows, t_valid=T, name=tag + "lru")
    if t_pad != T:
        lru_out = lru_out.reshape(B, t_pad, LRU_WIDTH)[:, :T].reshape(M, LRU_WIDTH)

    r = seg(_C_DIFF, 3 * BR_WIDTH).reshape(B, T, 3, DIFF_HEADS, DIFF_VD)
    dq = _rope(_rmsnorm(r[:, :, 0].reshape(B, T, DIFF_HEADS, 2, DIFF_HD), p['diff_q_norm']), pos)
    dk = _rope(_rmsnorm(r[:, :, 1].reshape(B, T, DIFF_HEADS, 2, DIFF_HD), p['diff_k_norm']), pos)
    dvv = r[:, :, 2]
    lp = p['diff_lambda']
    lam = jnp.exp(jnp.sum(lp[0] * lp[1])) - jnp.exp(jnp.sum(lp[2] * lp[3])) + lam_init
    do = _paged_attention(_query_rows(dq, 2), diff_past['k'], diff_past['v'],
                          _new_page(dk.reshape(B, T, BR_WIDTH)), _new_page(dvv.reshape(B, T, BR_WIDTH)),
                          diff_past['page_table'], diff_past['layer'], pages_per_step=diff_past['pages_per_step'],
                          n_maps=2, scale=1.0 / math.sqrt(DIFF_HD), lam=lam, gsub=p['diff_sub_norm'],
                          sub_scale=1.0 - lam_init, name=tag + "diff_paged")
    diff_out = do[:, :T].reshape(M, BR_WIDTH).astype(BF16)

    gc, gdn_buf_new = _causal_conv(seg(_C_GDN, 3 * BR_WIDTH), gdn_buf, p['gdn_conv_w'])
    gc = jax.nn.silu(gc).reshape(B, T, 3, GDN_HEADS, GDN_HD)
    gq = _l2norm(gc[:, :, 0]) * (GDN_HD ** -0.5)
    gk = _l2norm(gc[:, :, 1])
    gvv = gc[:, :, 2]
    gbeta = jax.nn.sigmoid(seg(_C_SMALL + 2 * HEADS, GDN_HEADS))
    gg = -jnp.exp(p['gdn_A_log']) * jax.nn.softplus(seg(_C_SMALL + HEADS, GDN_HEADS) + p['gdn_dt_bias'])
    chunk = 64 if T % 64 == 0 else 32
    tg = -(-T // chunk) * chunk
    gb = _pad_lanes(_pad_time(jnp.concatenate([gg, gbeta], axis=-1), tg))
    hm = lambda a: _head_major(_pad_time(a, tg))
    go, gS = _gdn(hm(gq), hm(gk), hm(gvv), gb, gdn_S0.reshape(B, GDN_HEADS * GDN_HD, GDN_HD), chunk=chunk,
                  group=math.gcd(B, 4), name=tag + "gdn")
    gdn_out = _gdn_gate(_head_major(go)[:, :T], seg(_C_GDNZ, BR_WIDTH), p['gdn_out_norm'])

    x2 = _merge(x2, (fox_out, lru_out, diff_out, gdn_out), gates, pw['w_branch'], pw['w_out'], row_tile=row_tile,
                name=tag + "merge")
    state = (fk, fv, flogf, dk, dvv, lru_hT, lru_buf_new, gS.reshape(B, GDN_HEADS, GDN_HD, GDN_HD), gdn_buf_new)
    return x2.reshape(B, T, D), state


def _memory_kv(mem, p, pw):
    B, Tm, D = mem.shape
    kv, = _norm_matmul(mem.reshape(B * Tm, D), p['norm_mem_src'], [pw['w_mem_kv']], row_tile=256, name="mem_kv")
    kv = kv.reshape(B, Tm, 2, MEM_HEADS, MEM_HD)
    return _rmsnorm(kv[:, :, 0], p['mem_k_norm']), kv[:, :, 1]


def _memory_sublayer(x, mk, mv, p, pw, *, row_tile, name):
    B, T, D = x.shape
    t_pad = -(-T // row_tile) * row_tile
    flat = lambda a: a.reshape(a.shape[0], a.shape[1], MEM_WIDTH)
    out = _mem_sublayer(_pad_time(x, t_pad), p['norm_mem'], pw['w_mem_q'], p['mem_q_norm'], flat(mk), flat(mv),
                        pw['w_mem_o'], row_tile=row_tile, name=name)
    return out[:, :T]


def kernel(x_prompt, x_sample, cache_fox_k, cache_fox_v, cache_fox_logf, cache_diff_k, cache_diff_v, cache_mem_k, cache_mem_v, state_lru_h, state_lru_conv, state_gdn_S, state_gdn_conv, page_table, mem_prompt, norm_mix, w_in, fox_b_f, fox_q_norm, fox_k_norm, lru_conv_w, lru_conv_b, lru_w_a, lru_b_a, lru_w_x, lru_b_x, lru_lambda, diff_q_norm, diff_k_norm, diff_lambda, diff_sub_norm, gdn_conv_w, gdn_A_log, gdn_dt_bias, gdn_out_norm, w_branch, w_out, norm_mem, norm_mem_src, w_mem_q, w_mem_kv, mem_q_norm, mem_k_norm, w_mem_o, norm_mlp, w_mlp_up, w_mlp_down):
    stacked = dict(norm_mix=norm_mix, w_in=w_in, fox_b_f=fox_b_f, fox_q_norm=fox_q_norm, fox_k_norm=fox_k_norm,
                   lru_conv_w=lru_conv_w, lru_conv_b=lru_conv_b, lru_w_a=lru_w_a, lru_b_a=lru_b_a,
                   lru_w_x=lru_w_x, lru_b_x=lru_b_x, lru_lambda=lru_lambda, diff_q_norm=diff_q_norm,
                   diff_k_norm=diff_k_norm, diff_lambda=diff_lambda, diff_sub_norm=diff_sub_norm,
                   gdn_conv_w=gdn_conv_w, gdn_A_log=gdn_A_log, gdn_dt_bias=gdn_dt_bias, gdn_out_norm=gdn_out_norm,
                   w_branch=w_branch, w_out=w_out, norm_mem=norm_mem, norm_mem_src=norm_mem_src, w_mem_q=w_mem_q,
                   w_mem_kv=w_mem_kv, mem_q_norm=mem_q_norm, mem_k_norm=mem_k_norm, w_mem_o=w_mem_o,
                   norm_mlp=norm_mlp, w_mlp_up=w_mlp_up, w_mlp_down=w_mlp_down)
    Bp, Tp, D = x_prompt.shape
    Bs, Ts, _ = x_sample.shape
    past_len = page_table.shape[1] * cache_fox_k.shape[2]
    rope_p = _rope_tables(jnp.arange(Tp))
    pos_s = past_len + jnp.arange(Ts)

    fox_k_t, fox_v_t, diff_k_t, diff_v_t = (_key_minor_pages(c) for c in (cache_fox_k, cache_fox_v, cache_diff_k,
                                                                      cache_diff_v))
    fox_logf_t = jnp.swapaxes(cache_fox_logf, 2, 3)

    xp, xs = x_prompt, x_sample
    pst = [[] for _ in range(11)]
    sst = [[] for _ in range(9)]
    for l in range(DEPTH):
        p = {name: arr[l] for name, arr in stacked.items()}
        pw = _pack_layer_weights(p)
        lam_init = 0.8 - 0.6 * math.exp(-0.3 * l)
        xp, st_p = _mixer_sublayer_prompt(xp, p, pw, lam_init, rope_p, row_tile=256, attn_tile=512, attn_chunk=512,
                                          prep_rows=256, tag="p_")
        mk, mv = _memory_kv(mem_prompt, p, pw)
        xp = _memory_sublayer(xp, mk, mv, p, pw, row_tile=512, name="p_mem")
        xp = _mlp_sublayer(xp.reshape(Bp * Tp, D), p['norm_mlp'], pw['w_mlp_up'], pw['w_mlp_down'], row_tile=512,
                           name="p_mlp").reshape(Bp, Tp, D)
        for lst, val in zip(pst, (st_p[0], st_p[1], st_p[2], st_p[3], st_p[4], mk, mv,
                                  st_p[5], st_p[6], st_p[7], st_p[8])):
            lst.append(val)
        fox_past = dict(k=fox_k_t, v=fox_v_t, logf=fox_logf_t, page_table=page_table, layer=l, pages_per_step=16)
        diff_past = dict(k=diff_k_t, v=diff_v_t, page_table=page_table, layer=l, pages_per_step=16)
        xs, st_s = _mixer_sublayer(xs, pos_s, p, pw, lam_init, fox_past, diff_past, state_lru_h[l], state_lru_conv[l],
                                   state_gdn_S[l], state_gdn_conv[l], row_tile=Bs * Ts, tag="s_")
        xs = _memory_sublayer(xs, cache_mem_k[l], cache_mem_v[l], p, pw, row_tile=8, name="s_mem")
        xs = _mlp_sublayer(xs.reshape(Bs * Ts, D), p['norm_mlp'], pw['w_mlp_up'], pw['w_mlp_down'], row_tile=Bs * Ts,
                           name="s_mlp").reshape(Bs, Ts, D)
        for lst, val in zip(sst, st_s):
            lst.append(val)

    return tuple([xp, xs] + [jnp.stack(v_, axis=0) for v_ in pst] + [jnp.stack(v_, axis=0) for v_ in sst])
```

```python
import functools
import math

import jax
import jax.numpy as jnp
import numpy as np
from jax import lax
from jax.experimental import pallas as pl
from jax.experimental.pallas import tpu as pltpu

D_MODEL = 1024
DEPTH = 2
N_BRANCH = 4
BR_WIDTH = D_MODEL // 4
FOX_HEADS = 4
FOX_HD = BR_WIDTH // FOX_HEADS
LRU_WIDTH = BR_WIDTH
LRU_BLOCKS = 4
LRU_BD = LRU_WIDTH // LRU_BLOCKS
LRU_C = 8.0
CONV_W = 4
DIFF_HEADS = 4
DIFF_VD = BR_WIDTH // DIFF_HEADS
DIFF_HD = DIFF_VD // 2
GDN_HEADS = 4
GDN_HD = BR_WIDTH // GDN_HEADS
MEM_HEADS = 4
MEM_HD = 64
MEM_WIDTH = MEM_HEADS * MEM_HD
D_FF = 4 * D_MODEL
ROPE_THETA = 10000.0
Q_BLOCK = 128
NORM_EPS = 1e-6
HEADS = 4
HEAD_DIM = 64
SMALL_COLS = 128
MAIN_COLS = 12 * BR_WIDTH + SMALL_COLS

F32 = jnp.float32
BF16 = jnp.bfloat16
NEG_BIG = -1e30
LANES = 128
VMEM_LIMIT_BYTES = 56 * 1024 * 1024


def _params(*sem):
    return pltpu.CompilerParams(dimension_semantics=sem, vmem_limit_bytes=VMEM_LIMIT_BYTES)


def _resident(shape):
    nd = len(shape)
    return pl.BlockSpec(shape, lambda *_: (0,) * nd, pipeline_mode=pl.Buffered(1))


def _dot(a, b):
    return jnp.dot(a, b, preferred_element_type=F32)


def _dot_nt(a, b):
    return lax.dot_general(a, b, (((1,), (1,)), ((), ())), preferred_element_type=F32)


def _dot_tn(a, b):
    return lax.dot_general(a, b, (((0,), (0,)), ((), ())), preferred_element_type=F32)


def _split3(x):
    p0 = x.astype(BF16)
    r = x - p0.astype(F32)
    p1 = r.astype(BF16)
    p2 = (r - p1.astype(F32)).astype(BF16)
    return p0, p1, p2


def _split3_trunc(x):
    def top(v):
        return lax.bitcast_convert_type(lax.bitcast_convert_type(v, jnp.uint32) & jnp.uint32(0xFFFF0000), F32)
    p0 = top(x)
    r = x - p0
    p1 = top(r)
    p2 = top(r - p1)
    return p0.astype(BF16), p1.astype(BF16), p2.astype(BF16)


def _dot_sel(x, sel_bf16):
    p0, p1, p2 = _split3(x)
    return _dot(p0, sel_bf16) + (_dot(p1, sel_bf16) + _dot(p2, sel_bf16))


def _sel_dot_nt(sel_bf16, x):
    p0, p1, p2 = _split3(x)
    return _dot_nt(sel_bf16, p0) + (_dot_nt(sel_bf16, p1) + _dot_nt(sel_bf16, p2))


def _sel_dot(sel_bf16, x):
    p0, p1, p2 = _split3(x)
    return _dot(sel_bf16, p0) + (_dot(sel_bf16, p1) + _dot(sel_bf16, p2))


def _split2(x):
    hi = x.astype(BF16)
    return hi, (x - hi.astype(F32)).astype(BF16)


def _dot3(ah, al, bh, bl):
    return _dot(ah, bh) + (_dot(al, bh) + _dot(ah, bl))


def _group_id(shape, axis, size):
    return lax.shift_right_logical(lax.broadcasted_iota(jnp.int32, shape, axis), int(math.log2(size)))


def _rms_scale(x):
    return lax.rsqrt(jnp.mean(x * x, axis=-1, keepdims=True) + NORM_EPS)


def _sigmoid(x):
    return 1.0 / (1.0 + jnp.exp(-x))


def _norm_matmul_body(x_ref, g_ref, *refs, n_out, col_chunk, transposed):
    w_refs, o_refs = refs[:n_out], refs[n_out:]
    x = x_ref[...]
    h = (x * _rms_scale(x) * g_ref[...]).astype(BF16)
    for w_ref, o_ref in zip(w_refs, o_refs):
        n = w_ref.shape[0 if transposed else 1]
        for c0 in range(0, n, col_chunk):
            c1 = min(n, c0 + col_chunk)
            o_ref[:, c0:c1] = _dot_nt(h, w_ref[c0:c1, :]) if transposed else _dot(h, w_ref[:, c0:c1])


def _norm_matmul(x2, g, ws, *, row_tile, name, transposed=False):
    M, D = x2.shape
    widths = [w.shape[0 if transposed else 1] for w in ws]
    return pl.pallas_call(
        functools.partial(_norm_matmul_body, n_out=len(ws), col_chunk=1024, transposed=transposed),
        grid=(M // row_tile,),
        in_specs=[pl.BlockSpec((row_tile, D), lambda i: (i, 0)), _resident((1, D))] + [_resident(w.shape) for w in ws],
        out_specs=[pl.BlockSpec((row_tile, n), lambda i: (i, 0)) for n in widths],
        out_shape=[jax.ShapeDtypeStruct((M, n), F32) for n in widths],
        compiler_params=_params("arbitrary"),
        name=name,
    )(x2, g.reshape(1, D), *ws)


def _flash_t_body(*refs, n_maps, use_bias, tq, tk, group, sub_scale):
    refs = list(refs)
    qt_ref, k_ref, vt_ref = refs[:3]
    ck_ref = refs[3] if use_bias else None
    lam_ref, gsub_ref = (refs[-3], refs[-2]) if n_maps > 1 else (None, None)
    o_ref = refs[-1]
    i = pl.program_id(1)
    q0 = i * tq
    n_full = q0 // tk
    n_diag = tq // tk
    C = HEADS * HEAD_DIM
    width = HEAD_DIM // n_maps

    def step(pairs, qts, j, carries, masked):
        start = pl.multiple_of(j * tk, tk)
        kj = k_ref[pl.ds(start, tk), :]
        if masked:
            kpos = start + lax.broadcasted_iota(jnp.int32, (tk, tq), 0)
            qpos = q0 + lax.broadcasted_iota(jnp.int32, (tk, tq), 1)
            keep = kpos <= qpos
        if use_bias:
            ck = ck_ref[pl.ds(start, tk), :]
        sts = [_dot(kj, qt) for qt in qts]
        if use_bias:
            sts = [st - ck[:, h:h + 1] for st, (h, _) in zip(sts, pairs)]
        if masked:
            sts = [jnp.where(keep, st, NEG_BIG) for st in sts]
        m_news = [jnp.maximum(m, jnp.max(st, axis=0, keepdims=True)) for st, (m, _, _) in zip(sts, carries)]
        ps = [jnp.exp(st - m_new) for st, m_new in zip(sts, m_news)]
        alphas = [jnp.exp(m - m_new) for (m, _, _), m_new in zip(carries, m_news)]
        ls = [a * l + jnp.sum(p, axis=0, keepdims=True) for a, (_, l, _), p in zip(alphas, carries, ps)]
        pvs = [_dot(vt_ref[0, h * HEAD_DIM:(h + 1) * HEAD_DIM, pl.ds(start, tk)], p.astype(BF16))
               for (h, _), p in zip(pairs, ps)]
        accs = [a * acc + pv for a, (_, _, acc), pv in zip(alphas, carries, pvs)]
        return tuple(zip(m_news, ls, accs))

    all_pairs = [(h, mp) for h in range(HEADS) for mp in range(n_maps)]
    qt_all = qt_ref[0]
    row_chain = _group_id((C, tq), 0, width)
    normalized = {}
    for g0 in range(0, len(all_pairs), group):
        pairs = all_pairs[g0:g0 + group]
        qts = [jnp.where(row_chain == h * n_maps + mp, qt_all, jnp.zeros_like(qt_all)) for h, mp in pairs]
        init = tuple((jnp.full((1, tq), NEG_BIG, F32), jnp.zeros((1, tq), F32), jnp.zeros((HEAD_DIM, tq), F32))
                     for _ in pairs)
        carries = lax.fori_loop(0, n_full, lambda j, c: step(pairs, qts, j, c, False), init)
        for dj in range(n_diag):
            carries = step(pairs, qts, n_full + dj, carries, True)
        for pair, (_, l, acc) in zip(pairs, carries):
            normalized[pair] = acc / l
    outs = []
    for h in range(HEADS):
        if n_maps == 1:
            outs.append(normalized[(h, 0)])
        else:
            d = normalized[(h, 0)] - lam_ref[...] * normalized[(h, 1)]
            ms = jnp.mean(d * d, axis=0, keepdims=True)
            outs.append(d * lax.rsqrt(ms + NORM_EPS) * gsub_ref[...] * sub_scale)
    o_ref[...] = jnp.concatenate(outs, axis=0).T.astype(o_ref.dtype)


def _flash_attention_t(qt, k, vt, *, n_maps, tq, tk, group=4, ck=None, lam=None, gsub=None, sub_scale=1.0, name):
    B, C, T = qt.shape
    nq = T // tq
    operands = [qt, k, vt]
    in_specs = [pl.BlockSpec((1, C, tq), lambda b, i: (b, 0, i)),
                pl.BlockSpec((T, C), lambda b, i: (b, 0)),
                pl.BlockSpec((1, C, T), lambda b, i: (b, 0, 0))]
    if ck is not None:
        operands.append(ck)
        in_specs.append(pl.BlockSpec((T, LANES), lambda b, i: (b, 0)))
    if n_maps > 1:
        operands += [lam.reshape(1, 1), gsub.reshape(HEAD_DIM, 1)]
        in_specs += [_resident((1, 1)), _resident((HEAD_DIM, 1))]
    return pl.pallas_call(
        functools.partial(_flash_t_body, n_maps=n_maps, use_bias=ck is not None, tq=tq, tk=tk, group=group,
                          sub_scale=sub_scale),
        grid=(B, nq),
        in_specs=in_specs,
        out_specs=pl.BlockSpec((tq, C), lambda b, i: (b * nq + i, 0)),
        out_shape=jax.ShapeDtypeStruct((B * T, C), BF16),
        compiler_params=_params("arbitrary", "arbitrary"),
        name=name,
    )(*operands)


def _pad_lanes(x, width=LANES):
    return jnp.pad(x, [(0, 0)] * (x.ndim - 1) + [(0, width - x.shape[-1])])


Q_ROWS = 8


def _paged_body(pt_ref, *refs, n_pages, n_maps, use_bias, scale, sub_scale, chains):
    del pt_ref
    refs = list(refs)
    q_ref = refs.pop(0)
    lam_ref, gsub_ref = (None, None) if use_bias else (refs.pop(0), refs.pop(0))
    knew_ref, vnew_ref = refs.pop(0), refs.pop(0)
    lfnew_ref = refs.pop(0) if use_bias else None
    k_refs, refs = refs[:n_pages], refs[n_pages:]
    v_refs, refs = refs[:n_pages], refs[n_pages:]
    lf_refs, refs = (refs[:n_pages], refs[n_pages:]) if use_bias else (None, refs)
    o_ref, m_ref, l_ref, acc_ref = refs[:4]
    carry_ref = refs[4] if use_bias else None
    g = pl.program_id(1)
    R = HEADS * n_maps * Q_ROWS
    C = HEADS * HEAD_DIM

    @pl.when(g == 0)
    def _init():
        m_ref[...] = jnp.full(m_ref.shape, NEG_BIG, F32)
        l_ref[...] = jnp.zeros(l_ref.shape, F32)
        acc_ref[...] = jnp.zeros(acc_ref.shape, F32)
        if use_bias:
            carry_ref[...] = jnp.zeros(carry_ref.shape, F32)

    q = q_ref[0]

    def head_rows(x):
        return jnp.concatenate([jnp.broadcast_to(x[h:h + 1], (Q_ROWS, x.shape[1])) for h in range(HEADS)], axis=0)

    def attend(kts, vts, lfs, causal_new, splits=1):
        n_all = len(kts)
        bias = None
        if use_bias:
            upper = jnp.where(lax.broadcasted_iota(jnp.int32, (LANES, LANES), 0)
                              <= lax.broadcasted_iota(jnp.int32, (LANES, LANES), 1), 1.0, 0.0).astype(BF16)
            rows = HEADS * Q_ROWS
            n = len(lfs)
            within = _dot_sel(jnp.concatenate([head_rows(lf) for lf in lfs], axis=0), upper)
            incl = jnp.broadcast_to(within[:, LANES - 1:LANES], within.shape)
            d = 1
            while d < n:
                incl = incl + jnp.concatenate([jnp.zeros((d * rows, LANES), F32), incl[:(n - d) * rows]], axis=0)
                d *= 2
            carry = carry_ref[...]
            before = jnp.concatenate([jnp.zeros((rows, LANES), F32), incl[:(n - 1) * rows]], axis=0) if n > 1 else 0.0
            c_all = within + before
            carry_ref[...] = carry + incl[(n - 1) * rows:]
            bias = [c_all[i * rows:(i + 1) * rows] + carry for i in range(n)]
        per = n_all // splits
        sls = [slice(c * per, (c + 1) * per) for c in range(splits)]
        ss = [_dot(q, jnp.concatenate(kts[sl], axis=1).astype(BF16)) * scale for sl in sls]
        if use_bias:
            ss = [s - jnp.concatenate(bias[sl], axis=1) for s, sl in zip(ss, sls)]
        if causal_new:
            t = jnp.bitwise_and(lax.broadcasted_iota(jnp.int32, ss[0].shape, 0), Q_ROWS - 1)
            ss = [jnp.where(lax.broadcasted_iota(jnp.int32, s.shape, 1) <= t, s, NEG_BIG) for s in ss]
        ms = [jnp.max(s, axis=1, keepdims=True) for s in ss]
        ps = [jnp.exp(s - m_c) for s, m_c in zip(ss, ms)]
        ls = [jnp.sum(p, axis=1, keepdims=True) for p in ps]
        pvs = [_dot_nt(p.astype(BF16), jnp.concatenate(vts[sl], axis=1).astype(BF16)) for p, sl in zip(ps, sls)]
        parts = list(zip(ms, ls, pvs))
        m_prev = m_ref[...]
        m_new = m_prev
        for m_c, _, _ in parts:
            m_new = jnp.maximum(m_new, m_c)
        alpha = jnp.exp(m_prev - m_new)
        l_new = alpha * l_ref[...]
        acc_new = alpha * acc_ref[...]
        for m_c, l_c, pv_c in parts:
            w_c = jnp.exp(m_c - m_new)
            l_new = l_new + w_c * l_c
            acc_new = acc_new + w_c * pv_c
        l_ref[...] = l_new
        acc_ref[...] = acc_new
        m_ref[...] = m_new

    attend([r[0, 0] for r in k_refs], [r[0, 0] for r in v_refs], [r[0, 0] for r in lf_refs] if use_bias else None, False,
           splits=chains)

    @pl.when(g == pl.num_programs(1) - 1)
    def _finish():
        attend([knew_ref[0]], [vnew_ref[0]], [lfnew_ref[0]] if use_bias else None, True)
        full = acc_ref[...] / l_ref[...]
        lane_head = _group_id((Q_ROWS, C), 1, HEAD_DIM)
        outs = []
        for mp in range(n_maps):
            o = jnp.zeros((Q_ROWS, C), F32)
            for h in range(HEADS):
                r0 = (h * n_maps + mp) * Q_ROWS
                o = o + jnp.where(lane_head == h, full[r0:r0 + Q_ROWS], 0.0)
            outs.append(o)
        if n_maps == 1:
            res = outs[0]
        else:
            d = outs[0] - lam_ref[...] * outs[1]
            head_sum = jnp.where(_group_id((C, C), 0, HEAD_DIM) == _group_id((C, C), 1, HEAD_DIM), 1.0, 0.0).astype(BF16)
            ms = _dot_sel(d * d, head_sum) * (1.0 / HEAD_DIM)
            res = d * lax.rsqrt(ms + NORM_EPS) * gsub_ref[...] * sub_scale
        o_ref[0] = res.astype(o_ref.dtype)


def _paged_attention(q_rows, k_cache_t, v_cache_t, knew_t, vnew_t, page_table, layer, *, pages_per_step, n_maps, scale,
                     lf_cache_t=None, lfnew_t=None, lam=None, gsub=None, sub_scale=1.0, name):
    B, R, C = q_rows.shape
    n_pages_total = page_table.shape[1]
    P = pages_per_step
    use_bias = lf_cache_t is not None
    per_seq = lambda shape: pl.BlockSpec((1,) + shape, lambda b, g, pt: (b,) + (0,) * len(shape))
    const = lambda shape: pl.BlockSpec(shape, lambda b, g, pt: (0,) * len(shape))

    def page_spec(i, tail):
        return pl.BlockSpec((1, 1) + tail, lambda b, g, pt: (layer, pt[b, g * P + i]) + (0,) * len(tail))

    operands = [q_rows]
    in_specs = [per_seq((R, C))]
    if not use_bias:
        operands += [lam.reshape(1, 1), jnp.tile(gsub, HEADS).reshape(1, C)]
        in_specs += [const((1, 1)), const((1, C))]
    operands += [knew_t, vnew_t]
    in_specs += [per_seq((C, LANES)), per_seq((C, LANES))]
    if use_bias:
        operands.append(lfnew_t)
        in_specs.append(per_seq((HEADS, LANES)))
    kv_tail = (C, LANES)
    operands += [k_cache_t] * P + [v_cache_t] * P
    in_specs += [page_spec(i, kv_tail) for i in range(P)] + [page_spec(i, kv_tail) for i in range(P)]
    scratch = [pltpu.VMEM((R, 1), F32), pltpu.VMEM((R, 1), F32), pltpu.VMEM((R, C), F32)]
    if use_bias:
        operands += [lf_cache_t] * P
        in_specs += [page_spec(i, (HEADS, LANES)) for i in range(P)]
        scratch.append(pltpu.VMEM((HEADS * Q_ROWS, LANES), F32))
    return pl.pallas_call(
        functools.partial(_paged_body, n_pages=P, n_maps=n_maps, use_bias=use_bias, scale=scale, sub_scale=sub_scale,
                          chains=math.gcd(P, 4)),
        grid_spec=pltpu.PrefetchScalarGridSpec(
            num_scalar_prefetch=1, grid=(B, n_pages_total // P), in_specs=in_specs,
            out_specs=pl.BlockSpec((1, Q_ROWS, C), lambda b, g, pt: (b, 0, 0)), scratch_shapes=scratch),
        out_shape=jax.ShapeDtypeStruct((B, Q_ROWS, C), F32),
        compiler_params=_params("arbitrary", "arbitrary"),
        name=name,
    )(page_table, *operands)


def _query_rows(q, n_maps):
    B, T, H, M, d = q.shape
    eye = jnp.eye(H * M, dtype=q.dtype).reshape(H, M, H, M)
    rows = jnp.einsum('bthmd,hmgn->bhmtgnd', q, eye)
    rows = jnp.pad(rows, ((0, 0), (0, 0), (0, 0), (0, Q_ROWS - T), (0, 0), (0, 0), (0, 0)))
    return rows.reshape(B, H * M * Q_ROWS, H * M * d).astype(BF16)


def _new_page(x):
    return jnp.pad(jnp.swapaxes(x, 1, 2), ((0, 0), (0, 0), (0, LANES - x.shape[1])))


def _gelu_tanh(x):
    return 0.5 * x * (1.0 + jnp.tanh(math.sqrt(2.0 / math.pi) * (x + 0.044715 * (x * x * x))))


def _lru_body(xc_ref, lg_ref, wa_ref, wx_ref, ba_ref, bx_ref, sp_ref, h0_ref, o_ref, ht_ref, carry_ref, *,
              rows, last_tile, last_row):
    t = pl.program_id(1)

    @pl.when(t == 0)
    def _init():
        carry_ref[...] = h0_ref[0]

    xc = xc_ref[...]
    xb = xc.astype(BF16)
    rg = _sigmoid(_dot(xb, wa_ref[...]) + ba_ref[...])
    ig = _sigmoid(_dot(xb, wx_ref[...]) + bx_ref[...])
    log_a = (-LRU_C) * rg * sp_ref[...]
    a = jnp.exp(log_a)
    b = jnp.sqrt(1.0 - jnp.exp(2.0 * log_a)) * (ig * xc)
    row = lax.broadcasted_iota(jnp.int32, a.shape, 0)
    d = 1
    while d < rows:
        a_prev = jnp.where(row >= d, pltpu.roll(a, d, 0), 1.0)
        b_prev = jnp.where(row >= d, pltpu.roll(b, d, 0), 0.0)
        b = a * b_prev + b
        a = a * a_prev
        d *= 2
    hs = b + a * carry_ref[...]
    carry_ref[...] = hs[rows - 1:rows]
    o_ref[...] = (hs * _gelu_tanh(lg_ref[...])).astype(o_ref.dtype)

    @pl.when(t == last_tile)
    def _state():
        ht_ref[0] = hs[last_row:last_row + 1]


def _lru(xc, lg, wa_bd, wx_bd, ba, bx, sp, h0, *, rows, t_valid, name):
    B, T, W = xc.shape
    nt = T // rows
    vec = lambda a: a.reshape(1, W)
    out, ht = pl.pallas_call(
        functools.partial(_lru_body, rows=rows, last_tile=(t_valid - 1) // rows, last_row=(t_valid - 1) % rows),
        grid=(B, nt),
        in_specs=[pl.BlockSpec((rows, W), lambda b, t: (b * nt + t, 0)),
                  pl.BlockSpec((rows, W), lambda b, t: (b * nt + t, 0)),
                  _resident((W, W)), _resident((W, W)), _resident((1, W)), _resident((1, W)), _resident((1, W)),
                  pl.BlockSpec((1, 1, W), lambda b, t: (b, 0, 0))],
        out_specs=[pl.BlockSpec((rows, W), lambda b, t: (b * nt + t, 0)),
                   pl.BlockSpec((1, 1, W), lambda b, t: (b, 0, 0))],
        out_shape=[jax.ShapeDtypeStruct((B * T, W), BF16), jax.ShapeDtypeStruct((B, 1, W), F32)],
        scratch_shapes=[pltpu.VMEM((1, W), F32)],
        compiler_params=_params("arbitrary", "arbitrary"),
        name=name,
    )(xc.reshape(B * T, W), lg.reshape(B * T, W), wa_bd, wx_bd, vec(ba), vec(bx), vec(sp), h0.reshape(B, 1, W))
    return out, ht.reshape(B, W)


def _gdn_chunk(qs, ks, vs, gbs, Ss, chunk, heads, head0s):
    each = lambda f, *ls: [f(*a) for a in zip(*ls)]
    R = heads * chunk
    SD = heads * HEAD_DIM
    r0 = lax.broadcasted_iota(jnp.int32, (R, R), 0)
    r1 = lax.broadcasted_iota(jnp.int32, (R, R), 1)
    same = _group_id((R, R), 0, chunk) == _group_id((R, R), 1, chunk)
    low = same & (r0 >= r1)
    strict = same & (r0 > r1)
    low_b = jnp.where(low, 1.0, 0.0).astype(BF16)
    same_b = jnp.where(same, 1.0, 0.0).astype(BF16)
    eye_b = jnp.where(r0 == r1, 1.0, 0.0).astype(BF16)

    colmat = each(lambda gb, h0: jnp.concatenate(
        [gb if h == 0 else pltpu.roll(gb, LANES - h, 1) for h in range(h0, h0 + heads)], axis=0), gbs, head0s)
    cparts = each(_split3, colmat)
    sel3 = lambda sel, p: _dot(sel, p[0]) + (_dot(sel, p[1]) + _dot(sel, p[2]))
    cum_cols = each(lambda p: sel3(low_b, p), cparts)
    tot_cols = each(lambda p: sel3(same_b, p), cparts)
    gcum_col = each(lambda c: c[:, 0:1], cum_cols)
    beta_col = each(lambda c: c[:, HEADS:HEADS + 1], colmat)
    gtot_col = each(lambda c: c[:, 0:1], tot_cols)
    cum_parts = each(_split3, cum_cols)
    gcum_row = each(lambda p: (_dot_tn(p[0], eye_b) + (_dot_tn(p[1], eye_b) + _dot_tn(p[2], eye_b)))[0:1, :], cum_parts)

    decay = each(lambda c, r: jnp.where(low, jnp.exp(jnp.where(low, c - r, 0.0)), 0.0), gcum_col, gcum_row)
    kb = each(lambda k: k.astype(BF16), ks)
    kk = each(lambda b: _dot_nt(b, b), kb)
    qk = each(lambda q, b: _dot_nt(q.astype(BF16), b), qs, kb)
    nmat = each(lambda a, d, b: jnp.where(strict, a * d * b, 0.0), kk, decay, beta_col)
    xs = each(lambda v, k, b, c: jnp.concatenate([v * b, k * (b * jnp.exp(c))], axis=1), vs, ks, beta_col, gcum_col)
    pw = each(_split2, nmat)
    xs = each(lambda x, p: x - _dot3(p[0], p[1], *_split2(x)), xs, pw)
    width = 2
    while width < chunk:
        pw = each(lambda p: _split2(_dot3(p[0], p[1], p[0], p[1])), pw)
        xs = each(lambda x, p: x + _dot3(p[0], p[1], *_split2(x)), xs, pw)
        width *= 2
    u = each(lambda x: x[:, 0:HEAD_DIM], xs)
    w = each(lambda x: pltpu.roll(x, HEAD_DIM, 1)[:, 0:HEAD_DIM], xs)

    head_match = _group_id((R, SD), 0, chunk) == _group_id((R, SD), 1, HEAD_DIM)

    def spread(x):
        return jnp.where(head_match, jnp.concatenate([x] * heads, axis=1), 0.0).astype(BF16)

    sb = each(lambda S: S.astype(BF16), Ss)
    v_new = each(lambda u_, w_, s: u_ - _dot(spread(w_), s), u, w, sb)
    vb = each(lambda x: x.astype(BF16), v_new)
    o = each(lambda q, c, s, a, d, x: _dot(spread(q * jnp.exp(c)), s) + _dot((a * d).astype(BF16), x),
             qs, gcum_col, sb, qk, decay, vb)
    kg = each(lambda k, t, c: spread(k * jnp.exp(t - c)), ks, gtot_col, gcum_col)
    tok = jnp.bitwise_and(lax.broadcasted_iota(jnp.int32, (R, SD), 0), chunk - 1)
    first = jnp.where(head_match & (tok == 0), 1.0, 0.0).astype(BF16)
    gtot_state = each(lambda p: _dot_tn(first, p[0]) + (_dot_tn(first, p[1]) + _dot_tn(first, p[2])),
                      each(_split3, tot_cols))
    s_new = each(lambda S, t, g, x: S * jnp.exp(t[:, 0:1]) + _dot_tn(g, x), Ss, gtot_state, kg, vb)
    return o, s_new


def _gdn_body(q_ref, k_ref, v_ref, gb_ref, s0_ref, o_ref, sT_ref, s_ref, *, chunk, group):
    c = pl.program_id(1)

    @pl.when(c == 0)
    def _init():
        s_ref[...] = s0_ref[...]

    hpg = min(HEADS, LANES // chunk)
    R = hpg * chunk
    probs = [(g, h0) for g in range(group) for h0 in range(0, HEADS, hpg)]
    hsl = lambda h0: slice(h0, h0 + hpg)
    rsl = lambda h0: slice(h0 * HEAD_DIM, (h0 + hpg) * HEAD_DIM)
    o, s_new = _gdn_chunk([q_ref[g, hsl(h0)].reshape(R, HEAD_DIM) for g, h0 in probs],
                          [k_ref[g, hsl(h0)].reshape(R, HEAD_DIM) for g, h0 in probs],
                          [v_ref[g, hsl(h0)].reshape(R, HEAD_DIM) for g, h0 in probs],
                          [gb_ref[g] for g, _ in probs], [s_ref[g, rsl(h0)] for g, h0 in probs], chunk, hpg,
                          [h0 for _, h0 in probs])
    for i, (g, h0) in enumerate(probs):
        o_ref[g, hsl(h0)] = o[i].reshape(hpg, chunk, HEAD_DIM)
        s_ref[g, rsl(h0)] = s_new[i]

    @pl.when(c == pl.num_programs(1) - 1)
    def _state():
        sT_ref[...] = s_ref[...]


def _gdn(q, k, v, gb, S0, *, chunk, group, name):
    B, H, T, hd = q.shape
    nc = T // chunk
    SD = H * hd
    tok = pl.BlockSpec((group, H, chunk, hd), lambda b, c: (b, 0, c, 0))
    st = pl.BlockSpec((group, SD, hd), lambda b, c: (b, 0, 0))
    return pl.pallas_call(
        functools.partial(_gdn_body, chunk=chunk, group=group),
        grid=(B // group, nc),
        in_specs=[tok, tok, tok, pl.BlockSpec((group, chunk, LANES), lambda b, c: (b, c, 0)), st],
        out_specs=[tok, st],
        out_shape=[jax.ShapeDtypeStruct((B, H, T, hd), F32), jax.ShapeDtypeStruct((B, SD, hd), F32)],
        scratch_shapes=[pltpu.VMEM((group, SD, hd), F32)],
        compiler_params=_params("arbitrary", "arbitrary"),
        name=name,
    )(q, k, v, gb, S0)


def _log1p_exp_neg_abs(x):
    return jnp.log(1.0 + jnp.exp(-jnp.abs(x)))


def _conv_tap_sum(x, tail, w):
    rows = x.shape[0]
    first = lax.broadcasted_iota(jnp.int32, (8, x.shape[1]), 0)
    y = x * w[CONV_W - 1:CONV_W]
    for back in range(1, CONV_W):
        rolled = pltpu.roll(x, back, 0)
        top = jnp.where(first < back, pltpu.roll(tail, back, 0), rolled[0:8])
        y = y + jnp.concatenate([top, rolled[8:rows]], axis=0) * w[CONV_W - 1 - back:CONV_W - back]
    return y


def _prep_body(fq_ref, fk_ref, fv_ref, lx_ref, dq_ref, dk_ref, dv_ref, gq_ref, gk_ref, gv_ref, sm_ref,
               fqg_ref, fkg_ref, fb_ref, dqg_ref, dkg_ref, cos_ref, sin_ref, gw_ref, lw_ref, lb_ref, ga_ref, gdt_ref,
               fqt_o, fk_o, fvt_o, fkT_o, fvT_o, flT_o, ck_o, dqt_o, dk_o, dvt_o, dkT_o, dvT_o,
               gq_o, gk_o, gv_o, gb_o, xc_o, lst_o, gst_o, ltail, gtail, ccarry, *, rows):
    t = pl.program_id(1)
    C = HEADS * HEAD_DIM

    @pl.when(t == 0)
    def _init():
        ltail[...] = jnp.zeros(ltail.shape, F32)
        gtail[...] = jnp.zeros(gtail.shape, F32)
        ccarry[...] = jnp.zeros(ccarry.shape, F32)

    def group_sum(width):
        return jnp.where(_group_id((C, C), 0, width) == _group_id((C, C), 1, width), 1.0, 0.0).astype(BF16)

    sum64, sum32 = group_sum(HEAD_DIM), group_sum(DIFF_HD)

    def group_rms(x, sel, width):
        return x * lax.rsqrt(_dot_sel(x * x, sel) * (1.0 / width) + NORM_EPS)

    kn = group_rms(fk_ref[...], sum64, FOX_HD) * fkg_ref[...]
    qn = group_rms(fq_ref[...], sum64, FOX_HD) * fqg_ref[...] * (1.0 / math.sqrt(FOX_HD))
    fv = fv_ref[...]
    knT, fvT = kn.T, fv.T
    fqt_o[0] = qn.T.astype(BF16)
    fk_o[...] = kn.astype(BF16)
    fvt_o[0] = fvT.astype(BF16)
    fkT_o[0] = knT
    fvT_o[0] = fvT
    sm = sm_ref[...]
    xf = sm + fb_ref[...]
    logf = jnp.minimum(xf, 0.0) - _log1p_exp_neg_abs(xf)
    flT_o[0] = logf.T[0:8]
    lower = jnp.where(lax.broadcasted_iota(jnp.int32, (rows, rows), 0) >= lax.broadcasted_iota(jnp.int32, (rows, rows), 1),
                      1.0, 0.0).astype(BF16)
    c = _sel_dot(lower, logf) + ccarry[...]
    ccarry[...] = c[rows - 1:rows]
    ck_o[...] = c

    lane = lax.broadcasted_iota(jnp.int32, (rows, LANES), 1)
    first_half = jnp.bitwise_and(lane, DIFF_HD - 1) < DIFF_HD // 2

    def swap_halves(x):
        parts = []
        for c0 in range(0, C, LANES):
            h = x[:, c0:c0 + LANES]
            parts.append(jnp.where(first_half, pltpu.roll(h, LANES - DIFF_HD // 2, 1), pltpu.roll(h, DIFF_HD // 2, 1)))
        return jnp.concatenate(parts, axis=1)

    def rotary(x):
        return x * cos_ref[...] + swap_halves(x) * sin_ref[...]

    dkr = rotary(group_rms(dk_ref[...], sum32, DIFF_HD) * dkg_ref[...])
    dqr = rotary(group_rms(dq_ref[...], sum32, DIFF_HD) * dqg_ref[...]) * (1.0 / math.sqrt(DIFF_HD))
    dv = dv_ref[...]
    dkrT, dvT = dkr.T, dv.T
    dqt_o[0] = dqr.T.astype(BF16)
    dk_o[...] = dkr.astype(BF16)
    dvt_o[0] = dvT.astype(BF16)
    dkT_o[0] = dkrT
    dvT_o[0] = dvT

    gw = gw_ref[...]
    raws = (gq_ref[...], gk_ref[...], gv_ref[...])
    acts = []
    for i, raw in enumerate(raws):
        y = _conv_tap_sum(raw, gtail[:, i * C:(i + 1) * C], gw[:, i * C:(i + 1) * C])
        acts.append(y * _sigmoid(y))
    gqn = acts[0] * lax.rsqrt(_dot_sel(acts[0] * acts[0], sum64) + NORM_EPS) * (GDN_HD ** -0.5)
    gkn = acts[1] * lax.rsqrt(_dot_sel(acts[1] * acts[1], sum64) + NORM_EPS)
    pick_r = lax.broadcasted_iota(jnp.int32, (C, HEAD_DIM), 0)
    pick_c = lax.broadcasted_iota(jnp.int32, (C, HEAD_DIM), 1)
    for h in range(HEADS):
        pick = jnp.where(pick_r == pick_c + h * HEAD_DIM, 1.0, 0.0).astype(BF16)
        gq_o[0, h] = _dot_sel(gqn, pick)
        gk_o[0, h] = _dot_sel(gkn, pick)
        gv_o[0, h] = _dot_sel(acts[2], pick)
    sh = pltpu.roll(sm, LANES - HEADS, 1)
    xa = sh + gdt_ref[...]
    decay = -ga_ref[...] * (jnp.maximum(xa, 0.0) + _log1p_exp_neg_abs(xa))
    gb_o[...] = jnp.where(lane < HEADS, decay, _sigmoid(sh))

    lx = lx_ref[...]
    xc_o[...] = _conv_tap_sum(lx, ltail[...], lw_ref[...]) + lb_ref[...]

    ltail[...] = lx[rows - 8:rows]
    gtail[...] = jnp.concatenate([r[rows - 8:rows] for r in raws], axis=1)
    lst_o[0] = ltail[...]
    gst_o[0] = gtail[...]


def _prep(zm2, B, T, p, cos_t, sin_t, *, rows, name):
    M = B * T
    nt = T // rows
    C = HEADS * HEAD_DIM
    colblk = lambda j, w=C: pl.BlockSpec((rows, w), lambda b, t: (b * nt + t, j))
    vec = lambda a, w=C: jnp.pad(a.reshape(1, -1), ((0, 0), (0, w - a.size)))
    tile4 = lambda g: jnp.tile(g, C // g.size).reshape(1, C)
    in_specs = ([colblk(j) for j in (0, 1, 2, 3, 5, 6, 7, 8, 9, 10)] + [colblk(_C_SMALL // LANES, LANES)]
                + [_resident((1, C))] * 2 + [_resident((1, LANES))] + [_resident((1, C))] * 2
                + [pl.BlockSpec((rows, C), lambda b, t: (t, 0))] * 2
                + [_resident((CONV_W, 3 * C)), _resident((CONV_W, C)), _resident((1, C)), _resident((1, LANES)),
                   _resident((1, LANES))])
    chan = lambda dt: (jax.ShapeDtypeStruct((B, C, T), dt), pl.BlockSpec((1, C, rows), lambda b, t: (b, 0, t)))
    tokm = lambda w, dt: (jax.ShapeDtypeStruct((M, w), dt), pl.BlockSpec((rows, w), lambda b, t: (b * nt + t, 0)))
    headm = (jax.ShapeDtypeStruct((B, HEADS, T, HEAD_DIM), F32),
             pl.BlockSpec((1, HEADS, rows, HEAD_DIM), lambda b, t: (b, 0, t, 0)))
    state = lambda w: (jax.ShapeDtypeStruct((B, 8, w), F32), pl.BlockSpec((1, 8, w), lambda b, t: (b, 0, 0)))
    outs = [chan(BF16), tokm(C, BF16), chan(BF16), chan(F32), chan(F32),
            (jax.ShapeDtypeStruct((B, 8, T), F32), pl.BlockSpec((1, 8, rows), lambda b, t: (b, 0, t))), tokm(LANES, F32),
            chan(BF16), tokm(C, BF16), chan(BF16), chan(F32), chan(F32),
            headm, headm, headm, tokm(LANES, F32), tokm(C, F32), state(C), state(3 * C)]
    names = ['fqt', 'fk', 'fvt', 'fkT', 'fvT', 'flT', 'ck', 'dqt', 'dk', 'dvt', 'dkT', 'dvT', 'gq', 'gk', 'gv', 'gb', 'xc',
             'lru_tail', 'gdn_tail']
    res = pl.pallas_call(
        functools.partial(_prep_body, rows=rows),
        grid=(B, nt),
        in_specs=in_specs,
        out_specs=[o[1] for o in outs],
        out_shape=[o[0] for o in outs],
        scratch_shapes=[pltpu.VMEM((8, C), F32), pltpu.VMEM((8, 3 * C), F32), pltpu.VMEM((1, LANES), F32)],
        compiler_params=_params("arbitrary", "arbitrary"),
        name=name,
    )(*([zm2] * 11), tile4(p['fox_q_norm']), tile4(p['fox_k_norm']), vec(p['fox_b_f'], LANES),
      tile4(p['diff_q_norm']), tile4(p['diff_k_norm']), cos_t, sin_t, p['gdn_conv_w'], p['lru_conv_w'],
      p['lru_conv_b'].reshape(1, C), vec(jnp.exp(p['gdn_A_log']), LANES), vec(p['gdn_dt_bias'], LANES))
    return dict(zip(names, res))


def _rope_tables(pos):
    inv = ROPE_THETA ** (-jnp.arange(0, DIFF_HD, 2, dtype=F32) / DIFF_HD)
    ang = pos.astype(F32)[:, None] * inv[None, :]
    cos = jnp.tile(jnp.cos(ang), (1, 2 * BR_WIDTH // DIFF_HD))
    sin = jnp.tile(jnp.concatenate([-jnp.sin(ang), jnp.sin(ang)], axis=1), (1, BR_WIDTH // DIFF_HD))
    return cos, sin


def _merge_body(x_ref, b0_ref, b1_ref, b2_ref, b3_ref, gates_ref, wb_ref, wo_ref, o_ref):
    projs = [_dot(b_ref[...], wb_ref[n]) for n, b_ref in enumerate((b0_ref, b1_ref, b2_ref, b3_ref))]
    merged = None
    for n, proj in enumerate(projs):
        term = _sigmoid(gates_ref[:, n * D_MODEL:(n + 1) * D_MODEL]) * proj
        merged = term if merged is None else merged + term
    o_ref[...] = x_ref[...] + _dot(merged.astype(BF16), wo_ref[...])


def _merge(x2, branches, gates, wb, wo, *, row_tile, name):
    M, D = x2.shape
    row = lambda w: pl.BlockSpec((row_tile, w), lambda i: (i, 0))
    return pl.pallas_call(
        _merge_body,
        grid=(M // row_tile,),
        in_specs=[row(D)] + [row(BR_WIDTH)] * N_BRANCH + [row(N_BRANCH * D), _resident(wb.shape), _resident(wo.shape)],
        out_specs=row(D),
        out_shape=jax.ShapeDtypeStruct((M, D), F32),
        compiler_params=_params("arbitrary"),
        name=name,
    )(x2, *branches, gates, wb, wo)


def _mem_body(x_ref, g_ref, wq_ref, qg_ref, mk_ref, mv_ref, wo_ref, o_ref):
    x = x_ref[0]
    h = (x * _rms_scale(x) * g_ref[...]).astype(BF16)
    q = _dot(h, wq_ref[...])
    W = q.shape[1]
    head_sum = jnp.where(_group_id((W, W), 0, MEM_HD) == _group_id((W, W), 1, MEM_HD), 1.0, 0.0).astype(BF16)
    ms = _dot_sel(q * q, head_sum) * (1.0 / MEM_HD)
    qn = q * lax.rsqrt(ms + NORM_EPS) * qg_ref[...] * (1.0 / math.sqrt(MEM_HD))
    mk = mk_ref[0].astype(BF16)
    mv = mv_ref[0].astype(BF16)
    lane_head = _group_id(q.shape, 1, MEM_HD)
    heads = range(MEM_HEADS)
    ss = [_dot_nt(jnp.where(lane_head == hh, qn, 0.0).astype(BF16), mk) for hh in heads]
    ps = [jnp.exp(s - jnp.max(s, axis=1, keepdims=True)) for s in ss]
    ps = [p / jnp.sum(p, axis=1, keepdims=True) for p in ps]
    pvs = [_dot(p.astype(BF16), mv) for p in ps]
    o = jnp.zeros(q.shape, F32)
    for hh in heads:
        o = o + jnp.where(lane_head == hh, pvs[hh], 0.0)
    o_ref[0] = x + _dot(o.astype(BF16), wo_ref[...])


def _mem_sublayer(x, g, wq, qg, mk, mv, wo, *, row_tile, name):
    B, T, D = x.shape
    nt = T // row_tile
    kv = pl.BlockSpec((1,) + mk.shape[1:], lambda b, t: (b, 0, 0))
    return pl.pallas_call(
        _mem_body,
        grid=(B, nt),
        in_specs=[pl.BlockSpec((1, row_tile, D), lambda b, t: (b, t, 0)), _resident((1, D)), _resident(wq.shape),
                  _resident((1, MEM_WIDTH)), kv, kv, _resident(wo.shape)],
        out_specs=pl.BlockSpec((1, row_tile, D), lambda b, t: (b, t, 0)),
        out_shape=jax.ShapeDtypeStruct((B, T, D), F32),
        compiler_params=_params("arbitrary", "arbitrary"),
        name=name,
    )(x, g.reshape(1, D), wq, jnp.tile(qg, MEM_HEADS).reshape(1, MEM_WIDTH), mk, mv, wo)


def _mlp_body(x_ref, g_ref, wu_ref, wd_ref, o_ref, *, ff_chunk):
    x = x_ref[...]
    h = (x * _rms_scale(x) * g_ref[...]).astype(BF16)
    acc = x
    for c in range(D_FF // ff_chunk):
        u = _dot(h, wu_ref[:, c * ff_chunk:(c + 1) * ff_chunk])
        u = jnp.square(jnp.maximum(u, 0.0)).astype(BF16)
        acc = acc + _dot(u, wd_ref[c * ff_chunk:(c + 1) * ff_chunk, :])
    o_ref[...] = acc


def _mlp_sublayer(x2, g, w_up, w_down, *, row_tile, name):
    M, D = x2.shape
    return pl.pallas_call(
        functools.partial(_mlp_body, ff_chunk=1024),
        grid=(M // row_tile,),
        in_specs=[pl.BlockSpec((row_tile, D), lambda i: (i, 0)), _resident((1, D)), _resident(w_up.shape),
                  _resident(w_down.shape)],
        out_specs=pl.BlockSpec((row_tile, D), lambda i: (i, 0)),
        out_shape=jax.ShapeDtypeStruct((M, D), F32),
        compiler_params=_params("arbitrary"),
        name=name,
    )(x2, g.reshape(1, D), w_up, w_down)


def _rmsnorm(x, g):
    xf = x.astype(F32)
    return xf * lax.rsqrt(jnp.mean(xf * xf, axis=-1, keepdims=True) + NORM_EPS) * g.astype(F32)


def _l2norm(x):
    return x * lax.rsqrt(jnp.sum(x * x, axis=-1, keepdims=True) + NORM_EPS)


def _rope(x, pos):
    d = x.shape[-1]
    inv = ROPE_THETA ** (-jnp.arange(0, d, 2, dtype=F32) / d)
    ang = pos.astype(F32)[:, None] * inv[None, :]
    ang = ang.reshape((1, ang.shape[0]) + (1,) * (x.ndim - 3) + (d // 2,))
    cos, sin = jnp.cos(ang), jnp.sin(ang)
    x1, x2 = x[..., : d // 2], x[..., d // 2:]
    return jnp.concatenate([x1 * cos - x2 * sin, x2 * cos + x1 * sin], axis=-1)


def _causal_conv(x, buf, w, b=None):
    xp = jnp.concatenate([buf.astype(x.dtype), x], axis=1)
    T = x.shape[1]
    y = xp[:, 0:T] * w[0]
    for i in range(1, CONV_W):
        y = y + xp[:, i:i + T] * w[i]
    if b is not None:
        y = y + b
    return y, xp[:, -(CONV_W - 1):]


def _key_minor_pages(cache):
    nd = cache.ndim
    t = jnp.transpose(cache, (0, 1) + tuple(range(3, nd)) + (2,))
    return t.reshape(cache.shape[:2] + (-1, cache.shape[2]))


def _head_major(x):
    return jnp.swapaxes(x, 1, 2)


def _pad_time(x, t_pad, axis=1):
    pad = [(0, 0)] * x.ndim
    pad[axis] = (0, t_pad - x.shape[axis])
    return jnp.pad(x, pad)


def _pack_layer_weights(p):
    o = np.cumsum((0,) + (3 * BR_WIDTH, FOX_HEADS, LRU_WIDTH, LRU_WIDTH, 3 * BR_WIDTH, 3 * BR_WIDTH, GDN_HEADS,
                          GDN_HEADS, BR_WIDTH, N_BRANCH * D_MODEL)).tolist()
    wt = p['w_in'].T
    seg = lambda i: wt[o[i]:o[i + 1]]
    small = jnp.concatenate([seg(1), seg(6), seg(7), jnp.zeros((SMALL_COLS - 3 * HEADS, D_MODEL), wt.dtype)], axis=0)
    w_main = jnp.concatenate([seg(0), seg(2), seg(3), seg(4), seg(5), seg(8), small], axis=0).astype(BF16)
    blockdiag = lambda wb: jax.scipy.linalg.block_diag(*[wb[i] for i in range(LRU_BLOCKS)]).astype(BF16)
    return dict(w_main=w_main, w_gates=seg(9).astype(BF16), lru_wa=blockdiag(p['lru_w_a']),
                lru_wx=blockdiag(p['lru_w_x']), w_branch=p['w_branch'].astype(BF16), w_out=p['w_out'].astype(BF16),
                w_mem_q=p['w_mem_q'].astype(BF16), w_mem_kv=p['w_mem_kv'].astype(BF16),
                w_mem_o=p['w_mem_o'].astype(BF16), w_mlp_up=p['w_mlp_up'].astype(BF16),
                w_mlp_down=p['w_mlp_down'].astype(BF16))


_C_FOX, _C_LRUX, _C_LRUG, _C_DIFF, _C_GDN, _C_GDNZ, _C_SMALL = 0, 768, 1024, 1280, 2048, 2816, 3072


def _gdn_gate(go, z, g):
    B, T = go.shape[:2]
    out = _rmsnorm(go, g) * jax.nn.silu(z.reshape(B, T, GDN_HEADS, GDN_HD))
    return out.reshape(B * T, BR_WIDTH).astype(BF16)


def _mixer_sublayer_prompt(x, p, pw, lam_init, rope_tables, *, row_tile, attn_tile, attn_chunk, prep_rows, tag):
    B, T, D = x.shape
    M = B * T
    x2 = x.reshape(M, D)
    zm, gates = _norm_matmul(x2, p['norm_mix'], [pw['w_main'], pw['w_gates']], row_tile=row_tile, name=tag + "in_proj",
                             transposed=True)
    pre = _prep(zm, B, T, p, *rope_tables, rows=prep_rows, name=tag + "prep")
    fox_out = _flash_attention_t(pre['fqt'], pre['fk'], pre['fvt'], n_maps=1, tq=attn_tile, tk=attn_chunk, ck=pre['ck'],
                                 name=tag + "fox_attn")
    lp = p['diff_lambda']
    lam = jnp.exp(jnp.sum(lp[0] * lp[1])) - jnp.exp(jnp.sum(lp[2] * lp[3])) + lam_init
    diff_out = _flash_attention_t(pre['dqt'], pre['dk'], pre['dvt'], n_maps=2, tq=attn_tile, tk=attn_chunk, lam=lam,
                                  gsub=p['diff_sub_norm'], sub_scale=1.0 - lam_init, name=tag + "diff_attn")
    zm3 = zm.reshape(B, T, MAIN_COLS)
    lru_out, lru_hT = _lru(pre['xc'].reshape(B, T, LRU_WIDTH), zm3[..., _C_LRUG:_C_LRUG + LRU_WIDTH], pw['lru_wa'],
                           pw['lru_wx'], p['lru_b_a'], p['lru_b_x'], jax.nn.softplus(-p['lru_lambda']),
                           jnp.zeros((B, LRU_WIDTH), F32), rows=256, t_valid=T, name=tag + "lru")
    go, gS = _gdn(pre['gq'], pre['gk'], pre['gv'], pre['gb'].reshape(B, T, LANES),
                  jnp.zeros((B, GDN_HEADS * GDN_HD, GDN_HD), F32), chunk=64, group=math.gcd(B, 4), name=tag + "gdn")
    gdn_out = _gdn_gate(_head_major(go), zm3[..., _C_GDNZ:_C_GDNZ + BR_WIDTH], p['gdn_out_norm'])
    x2 = _merge(x2, (fox_out, lru_out, diff_out, gdn_out), gates, pw['w_branch'], pw['w_out'], row_tile=row_tile,
                name=tag + "merge")
    token_major = lambda a: jnp.swapaxes(a, 1, 2)
    state = (token_major(pre['fkT']).reshape(B, T, FOX_HEADS, FOX_HD),
             token_major(pre['fvT']).reshape(B, T, FOX_HEADS, FOX_HD),
             token_major(pre['flT'][:, :FOX_HEADS]),
             token_major(pre['dkT']).reshape(B, T, DIFF_HEADS, 2, DIFF_HD),
             token_major(pre['dvT']).reshape(B, T, DIFF_HEADS, DIFF_VD),
             lru_hT, pre['lru_tail'][:, 8 - (CONV_W - 1):], gS.reshape(B, GDN_HEADS, GDN_HD, GDN_HD),
             pre['gdn_tail'][:, 8 - (CONV_W - 1):])
    return x2.reshape(B, T, D), state


def _mixer_sublayer(x, pos, p, pw, lam_init, fox_past, diff_past, lru_h0, lru_buf, gdn_S0, gdn_buf, *, row_tile, tag):
    B, T, D = x.shape
    M = B * T
    x2 = x.reshape(M, D)
    zm, gates = _norm_matmul(x2, p['norm_mix'], [pw['w_main'], pw['w_gates']], row_tile=row_tile, name=tag + "in_proj",
                             transposed=True)
    zm = zm.reshape(B, T, MAIN_COLS)
    seg = lambda c0, w: zm[..., c0:c0 + w]

    r = seg(_C_FOX, 3 * BR_WIDTH).reshape(B, T, 3, FOX_HEADS, FOX_HD)
    fq = _rmsnorm(r[:, :, 0], p['fox_q_norm'])
    fk = _rmsnorm(r[:, :, 1], p['fox_k_norm'])
    fv = r[:, :, 2]
    flogf = jax.nn.log_sigmoid(seg(_C_SMALL, FOX_HEADS) + p['fox_b_f'])
    fo = _paged_attention(_query_rows(fq[:, :, :, None], 1), fox_past['k'], fox_past['v'],
                          _new_page(fk.reshape(B, T, BR_WIDTH)), _new_page(fv.reshape(B, T, BR_WIDTH)),
                          fox_past['page_table'], fox_past['layer'], pages_per_step=fox_past['pages_per_step'],
                          n_maps=1, scale=1.0 / math.sqrt(FOX_HD), lf_cache_t=fox_past['logf'],
                          lfnew_t=_new_page(flogf), name=tag + "fox_paged")
    fox_out = fo[:, :T].reshape(M, BR_WIDTH).astype(BF16)

    xc, lru_buf_new = _causal_conv(seg(_C_LRUX, LRU_WIDTH), lru_buf, p['lru_conv_w'], p['lru_conv_b'])
    lru_rows = min(256, T) if T % 8 == 0 else 8
    t_pad = -(-T // lru_rows) * lru_rows
    lru_out, lru_hT = _lru(_pad_time(xc, t_pad), _pad_time(seg(_C_LRUG, LRU_WIDTH), t_pad), pw['lru_wa'], pw['lru_wx'],
                           p['lru_b_a'], p['lru_b_x'], jax.nn.softplus(-p['lru_lambda']), lru_h0,
                           rows=lru_rows, t_valid=T, name=tag + "lru")
    if t_pad != T:
        lru_out = lru_out.reshape(B, t_pad, LRU_WIDTH)[:, :T].reshape(M, LRU_WIDTH)

    r = seg(_C_DIFF, 3 * BR_WIDTH).reshape(B, T, 3, DIFF_HEADS, DIFF_VD)
    dq = _rope(_rmsnorm(r[:, :, 0].reshape(B, T, DIFF_HEADS, 2, DIFF_HD), p['diff_q_norm']), pos)
    dk = _rope(_rmsnorm(r[:, :, 1].reshape(B, T, DIFF_HEADS, 2, DIFF_HD), p['diff_k_norm']), pos)
    dvv = r[:, :, 2]
    lp = p['diff_lambda']
    lam = jnp.exp(jnp.sum(lp[0] * lp[1])) - jnp.exp(jnp.sum(lp[2] * lp[3])) + lam_init
    do = _paged_attention(_query_rows(dq, 2), diff_past['k'], diff_past['v'],
                          _new_page(dk.reshape(B, T, BR_WIDTH)), _new_page(dvv.reshape(B, T, BR_WIDTH)),
                          diff_past['page_table'], diff_past['layer'], pages_per_step=diff_past['pages_per_step'],
                          n_maps=2, scale=1.0 / math.sqrt(DIFF_HD), lam=lam, gsub=p['diff_sub_norm'],
                          sub_scale=1.0 - lam_init, name=tag + "diff_paged")
    diff_out = do[:, :T].reshape(M, BR_WIDTH).astype(BF16)

    gc, gdn_buf_new = _causal_conv(seg(_C_GDN, 3 * BR_WIDTH), gdn_buf, p['gdn_conv_w'])
    gc = jax.nn.silu(gc).reshape(B, T, 3, GDN_HEADS, GDN_HD)
    gq = _l2norm(gc[:, :, 0]) * (GDN_HD ** -0.5)
    gk = _l2norm(gc[:, :, 1])
    gvv = gc[:, :, 2]
    gbeta = jax.nn.sigmoid(seg(_C_SMALL + 2 * HEADS, GDN_HEADS))
    gg = -jnp.exp(p['gdn_A_log']) * jax.nn.softplus(seg(_C_SMALL + HEADS, GDN_HEADS) + p['gdn_dt_bias'])
    chunk = 64 if T % 64 == 0 else 32
    tg = -(-T // chunk) * chunk
    gb = _pad_lanes(_pad_time(jnp.concatenate([gg, gbeta], axis=-1), tg))
    hm = lambda a: _head_major(_pad_time(a, tg))
    go, gS = _gdn(hm(gq), hm(gk), hm(gvv), gb, gdn_S0.reshape(B, GDN_HEADS * GDN_HD, GDN_HD), chunk=chunk,
                  group=math.gcd(B, 4), name=tag + "gdn")
    gdn_out = _gdn_gate(_head_major(go)[:, :T], seg(_C_GDNZ, BR_WIDTH), p['gdn_out_norm'])

    x2 = _merge(x2, (fox_out, lru_out, diff_out, gdn_out), gates, pw['w_branch'], pw['w_out'], row_tile=row_tile,
                name=tag + "merge")
    state = (fk, fv, flogf, dk, dvv, lru_hT, lru_buf_new, gS.reshape(B, GDN_HEADS, GDN_HD, GDN_HD), gdn_buf_new)
    return x2.reshape(B, T, D), state


def _memory_kv(mem, p, pw):
    B, Tm, D = mem.shape
    kv, = _norm_matmul(mem.reshape(B * Tm, D), p['norm_mem_src'], [pw['w_mem_kv']], row_tile=256, name="mem_kv")
    kv = kv.reshape(B, Tm, 2, MEM_HEADS, MEM_HD)
    return _rmsnorm(kv[:, :, 0], p['mem_k_norm']), kv[:, :, 1]


def _memory_sublayer(x, mk, mv, p, pw, *, row_tile, name):
    B, T, D = x.shape
    t_pad = -(-T // row_tile) * row_tile
    flat = lambda a: a.reshape(a.shape[0], a.shape[1], MEM_WIDTH)
    out = _mem_sublayer(_pad_time(x, t_pad), p['norm_mem'], pw['w_mem_q'], p['mem_q_norm'], flat(mk), flat(mv),
                        pw['w_mem_o'], row_tile=row_tile, name=name)
    return out[:, :T]


def kernel(x_prompt, x_sample, cache_fox_k, cache_fox_v, cache_fox_logf, cache_diff_k, cache_diff_v, cache_mem_k, cache_mem_v, state_lru_h, state_lru_conv, state_gdn_S, state_gdn_conv, page_table, mem_prompt, norm_mix, w_in, fox_b_f, fox_q_norm, fox_k_norm, lru_conv_w, lru_conv_b, lru_w_a, lru_b_a, lru_w_x, lru_b_x, lru_lambda, diff_q_norm, diff_k_norm, diff_lambda, diff_sub_norm, gdn_conv_w, gdn_A_log, gdn_dt_bias, gdn_out_norm, w_branch, w_out, norm_mem, norm_mem_src, w_mem_q, w_mem_kv, mem_q_norm, mem_k_norm, w_mem_o, norm_mlp, w_mlp_up, w_mlp_down):
    stacked = dict(norm_mix=norm_mix, w_in=w_in, fox_b_f=fox_b_f, fox_q_norm=fox_q_norm, fox_k_norm=fox_k_norm,
                   lru_conv_w=lru_conv_w, lru_conv_b=lru_conv_b, lru_w_a=lru_w_a, lru_b_a=lru_b_a,
                   lru_w_x=lru_w_x, lru_b_x=lru_b_x, lru_lambda=lru_lambda, diff_q_norm=diff_q_norm,
                   diff_k_norm=diff_k_norm, diff_lambda=diff_lambda, diff_sub_norm=diff_sub_norm,
                   gdn_conv_w=gdn_conv_w, gdn_A_log=gdn_A_log, gdn_dt_bias=gdn_dt_bias, gdn_out_norm=gdn_out_norm,
                   w_branch=w_branch, w_out=w_out, norm_mem=norm_mem, norm_mem_src=norm_mem_src, w_mem_q=w_mem_q,
                   w_mem_kv=w_mem_kv, mem_q_norm=mem_q_norm, mem_k_norm=mem_k_norm, w_mem_o=w_mem_o,
                   norm_mlp=norm_mlp, w_mlp_up=w_mlp_up, w_mlp_down=w_mlp_down)
    Bp, Tp, D = x_prompt.shape
    Bs, Ts, _ = x_sample.shape
    past_len = page_table.shape[1] * cache_fox_k.shape[2]
    rope_p = _rope_tables(jnp.arange(Tp))
    pos_s = past_len + jnp.arange(Ts)

    fox_k_t, fox_v_t, diff_k_t, diff_v_t = (_key_minor_pages(c) for c in (cache_fox_k, cache_fox_v, cache_diff_k,
                                                                      cache_diff_v))
    fox_logf_t = jnp.swapaxes(cache_fox_logf, 2, 3)

    xp, xs = x_prompt, x_sample
    pst = [[] for _ in range(11)]
    sst = [[] for _ in range(9)]
    for l in range(DEPTH):
        p = {name: arr[l] for name, arr in stacked.items()}
        pw = _pack_layer_weights(p)
        lam_init = 0.8 - 0.6 * math.exp(-0.3 * l)
        xp, st_p = _mixer_sublayer_prompt(xp, p, pw, lam_init, rope_p, row_tile=256, attn_tile=512, attn_chunk=512,
                                          prep_rows=256, tag="p_")
        mk, mv = _memory_kv(mem_prompt, p, pw)
        xp = _memory_sublayer(xp, mk, mv, p, pw, row_tile=512, name="p_mem")
        xp = _mlp_sublayer(xp.reshape(Bp * Tp, D), p['norm_mlp'], pw['w_mlp_up'], pw['w_mlp_down'], row_tile=512,
                           name="p_mlp").reshape(Bp, Tp, D)
        for lst, val in zip(pst, (st_p[0], st_p[1], st_p[2], st_p[3], st_p[4], mk, mv,
                                  st_p[5], st_p[6], st_p[7], st_p[8])):
            lst.append(val)
        fox_past = dict(k=fox_k_t, v=fox_v_t, logf=fox_logf_t, page_table=page_table, layer=l, pages_per_step=32)
        diff_past = dict(k=diff_k_t, v=diff_v_t, page_table=page_table, layer=l, pages_per_step=32)
        xs, st_s = _mixer_sublayer(xs, pos_s, p, pw, lam_init, fox_past, diff_past, state_lru_h[l], state_lru_conv[l],
                                   state_gdn_S[l], state_gdn_conv[l], row_tile=Bs * Ts, tag="s_")
        xs = _memory_sublayer(xs, cache_mem_k[l], cache_mem_v[l], p, pw, row_tile=8, name="s_mem")
        xs = _mlp_sublayer(xs.reshape(Bs * Ts, D), p['norm_mlp'], pw['w_mlp_up'], pw['w_mlp_down'], row_tile=Bs * Ts,
                           name="s_mlp").reshape(Bs, Ts, D)
        for lst, val in zip(sst, st_s):
            lst.append(val)

    return tuple([xp, xs] + [jnp.stack(v_, axis=0) for v_ in pst] + [jnp.stack(v_, axis=0) for v_ in sst])
```

```python
import functools
import math

import jax
import jax.numpy as jnp
import numpy as np
from jax import lax
from jax.experimental import pallas as pl
from jax.experimental.pallas import tpu as pltpu

D_MODEL = 1024
DEPTH = 2
N_BRANCH = 4
BR_WIDTH = D_MODEL // 4
FOX_HEADS = 4
FOX_HD = BR_WIDTH // FOX_HEADS
LRU_WIDTH = BR_WIDTH
LRU_BLOCKS = 4
LRU_C = 8.0
CONV_W = 4
DIFF_HEADS = 4
DIFF_VD = BR_WIDTH // DIFF_HEADS
DIFF_HD = DIFF_VD // 2
GDN_HEADS = 4
GDN_HD = BR_WIDTH // GDN_HEADS
MEM_HEADS = 4
MEM_HD = 64
MEM_WIDTH = MEM_HEADS * MEM_HD
D_FF = 4 * D_MODEL
ROPE_THETA = 10000.0
NORM_EPS = 1e-6
HEADS = 4
HEAD_DIM = 64
SMALL_COLS = 128
MAIN_COLS = 12 * BR_WIDTH + SMALL_COLS

F32 = jnp.float32
BF16 = jnp.bfloat16
NEG_BIG = -1e30
LANES = 128
VMEM_LIMIT_BYTES = 56 * 1024 * 1024


def _params(*sem):
    return pltpu.CompilerParams(dimension_semantics=sem, vmem_limit_bytes=VMEM_LIMIT_BYTES)


def _resident(shape):
    nd = len(shape)
    return pl.BlockSpec(shape, lambda *_: (0,) * nd, pipeline_mode=pl.Buffered(1))


def _dot(a, b):
    return jnp.dot(a, b, preferred_element_type=F32)


def _dot_nt(a, b):
    return lax.dot_general(a, b, (((1,), (1,)), ((), ())), preferred_element_type=F32)


def _dot_tn(a, b):
    return lax.dot_general(a, b, (((0,), (0,)), ((), ())), preferred_element_type=F32)


def _split3(x):
    p0 = x.astype(BF16)
    r = x - p0.astype(F32)
    p1 = r.astype(BF16)
    p2 = (r - p1.astype(F32)).astype(BF16)
    return p0, p1, p2


def _dot_sel(x, sel_bf16):
    p0, p1, p2 = _split3(x)
    return _dot(p0, sel_bf16) + (_dot(p1, sel_bf16) + _dot(p2, sel_bf16))


def _sel_dot(sel_bf16, x):
    p0, p1, p2 = _split3(x)
    return _dot(sel_bf16, p0) + (_dot(sel_bf16, p1) + _dot(sel_bf16, p2))


def _split2(x):
    hi = x.astype(BF16)
    return hi, (x - hi.astype(F32)).astype(BF16)


def _dot3(ah, al, bh, bl):
    return _dot(ah, bh) + (_dot(al, bh) + _dot(ah, bl))


def _group_id(shape, axis, size):
    return lax.shift_right_logical(lax.broadcasted_iota(jnp.int32, shape, axis), int(math.log2(size)))


def _rms_scale(x):
    return lax.rsqrt(jnp.mean(x * x, axis=-1, keepdims=True) + NORM_EPS)


def _sigmoid(x):
    return 1.0 / (1.0 + jnp.exp(-x))


def _norm_matmul_body(x_ref, g_ref, *refs, n_out, col_chunk, transposed):
    w_refs, o_refs = refs[:n_out], refs[n_out:]
    x = x_ref[...]
    h = (x * _rms_scale(x) * g_ref[...]).astype(BF16)
    for w_ref, o_ref in zip(w_refs, o_refs):
        n = w_ref.shape[0 if transposed else 1]
        for c0 in range(0, n, col_chunk):
            c1 = min(n, c0 + col_chunk)
            o_ref[:, c0:c1] = _dot_nt(h, w_ref[c0:c1, :]) if transposed else _dot(h, w_ref[:, c0:c1])


def _norm_matmul(x2, g, ws, *, row_tile, name, transposed=False):
    M, D = x2.shape
    widths = [w.shape[0 if transposed else 1] for w in ws]
    return pl.pallas_call(
        functools.partial(_norm_matmul_body, n_out=len(ws), col_chunk=1024, transposed=transposed),
        grid=(M // row_tile,),
        in_specs=[pl.BlockSpec((row_tile, D), lambda i: (i, 0)), _resident((1, D))] + [_resident(w.shape) for w in ws],
        out_specs=[pl.BlockSpec((row_tile, n), lambda i: (i, 0)) for n in widths],
        out_shape=[jax.ShapeDtypeStruct((M, n), F32) for n in widths],
        compiler_params=_params("arbitrary"),
        name=name,
    )(x2, g.reshape(1, D), *ws)


def _flash_t_body(*refs, n_maps, use_bias, tq, tk, group, sub_scale):
    refs = list(refs)
    qt_ref, k_ref, vt_ref = refs[:3]
    ck_ref = refs[3] if use_bias else None
    lam_ref, gsub_ref = (refs[-3], refs[-2]) if n_maps > 1 else (None, None)
    o_ref = refs[-1]
    i = pl.program_id(1)
    q0 = i * tq
    n_full = q0 // tk
    n_diag = tq // tk
    C = HEADS * HEAD_DIM
    width = HEAD_DIM // n_maps

    def step(pairs, qts, j, carries, masked):
        start = pl.multiple_of(j * tk, tk)
        kj = k_ref[pl.ds(start, tk), :]
        if masked:
            kpos = start + lax.broadcasted_iota(jnp.int32, (tk, tq), 0)
            qpos = q0 + lax.broadcasted_iota(jnp.int32, (tk, tq), 1)
            keep = kpos <= qpos
        if use_bias:
            ck = ck_ref[pl.ds(start, tk), :]
        sts = [_dot(kj, qt) for qt in qts]
        if use_bias:
            sts = [st - ck[:, h:h + 1] for st, (h, _) in zip(sts, pairs)]
        if masked:
            sts = [jnp.where(keep, st, NEG_BIG) for st in sts]
        m_news = [jnp.maximum(m, jnp.max(st, axis=0, keepdims=True)) for st, (m, _, _) in zip(sts, carries)]
        ps = [jnp.exp(st - m_new) for st, m_new in zip(sts, m_news)]
        alphas = [jnp.exp(m - m_new) for (m, _, _), m_new in zip(carries, m_news)]
        ls = [a * l + jnp.sum(p, axis=0, keepdims=True) for a, (_, l, _), p in zip(alphas, carries, ps)]
        pvs = [_dot(vt_ref[0, h * HEAD_DIM:(h + 1) * HEAD_DIM, pl.ds(start, tk)], p.astype(BF16))
               for (h, _), p in zip(pairs, ps)]
        accs = [a * acc + pv for a, (_, _, acc), pv in zip(alphas, carries, pvs)]
        return tuple(zip(m_news, ls, accs))

    all_pairs = [(h, mp) for h in range(HEADS) for mp in range(n_maps)]
    qt_all = qt_ref[0]
    row_chain = _group_id((C, tq), 0, width)
    normalized = {}
    for g0 in range(0, len(all_pairs), group):
        pairs = all_pairs[g0:g0 + group]
        qts = [jnp.where(row_chain == h * n_maps + mp, qt_all, jnp.zeros_like(qt_all)) for h, mp in pairs]
        init = tuple((jnp.full((1, tq), NEG_BIG, F32), jnp.zeros((1, tq), F32), jnp.zeros((HEAD_DIM, tq), F32))
                     for _ in pairs)
        carries = lax.fori_loop(0, n_full, lambda j, c: step(pairs, qts, j, c, False), init)
        for dj in range(n_diag):
            carries = step(pairs, qts, n_full + dj, carries, True)
        for pair, (_, l, acc) in zip(pairs, carries):
            normalized[pair] = acc / l
    outs = []
    for h in range(HEADS):
        if n_maps == 1:
            outs.append(normalized[(h, 0)])
        else:
            d = normalized[(h, 0)] - lam_ref[...] * normalized[(h, 1)]
            ms = jnp.mean(d * d, axis=0, keepdims=True)
            outs.append(d * lax.rsqrt(ms + NORM_EPS) * gsub_ref[...] * sub_scale)
    o_ref[...] = jnp.concatenate(outs, axis=0).T.astype(o_ref.dtype)


def _flash_attention_t(qt, k, vt, *, n_maps, tq, tk, group=4, ck=None, lam=None, gsub=None, sub_scale=1.0, name):
    B, C, T = qt.shape
    nq = T // tq
    operands = [qt, k, vt]
    in_specs = [pl.BlockSpec((1, C, tq), lambda b, i: (b, 0, i)),
                pl.BlockSpec((T, C), lambda b, i: (b, 0)),
                pl.BlockSpec((1, C, T), lambda b, i: (b, 0, 0))]
    if ck is not None:
        operands.append(ck)
        in_specs.append(pl.BlockSpec((T, LANES), lambda b, i: (b, 0)))
    if n_maps > 1:
        operands += [lam.reshape(1, 1), gsub.reshape(HEAD_DIM, 1)]
        in_specs += [_resident((1, 1)), _resident((HEAD_DIM, 1))]
    return pl.pallas_call(
        functools.partial(_flash_t_body, n_maps=n_maps, use_bias=ck is not None, tq=tq, tk=tk, group=group,
                          sub_scale=sub_scale),
        grid=(B, nq),
        in_specs=in_specs,
        out_specs=pl.BlockSpec((tq, C), lambda b, i: (b * nq + i, 0)),
        out_shape=jax.ShapeDtypeStruct((B * T, C), BF16),
        compiler_params=_params("arbitrary", "arbitrary"),
        name=name,
    )(*operands)


def _pad_lanes(x, width=LANES):
    return jnp.pad(x, [(0, 0)] * (x.ndim - 1) + [(0, width - x.shape[-1])])


Q_ROWS = 8


def _paged_body(pt_ref, *refs, n_pages, n_maps, use_bias, scale, sub_scale, chains):
    del pt_ref
    refs = list(refs)
    q_ref = refs.pop(0)
    lam_ref, gsub_ref = (None, None) if use_bias else (refs.pop(0), refs.pop(0))
    knew_ref, vnew_ref = refs.pop(0), refs.pop(0)
    lfnew_ref = refs.pop(0) if use_bias else None
    k_refs, refs = refs[:n_pages], refs[n_pages:]
    v_refs, refs = refs[:n_pages], refs[n_pages:]
    lf_refs, refs = (refs[:n_pages], refs[n_pages:]) if use_bias else (None, refs)
    o_ref, m_ref, l_ref, acc_ref = refs[:4]
    carry_ref = refs[4] if use_bias else None
    g = pl.program_id(1)
    R = HEADS * n_maps * Q_ROWS
    C = HEADS * HEAD_DIM

    @pl.when(g == 0)
    def _init():
        m_ref[...] = jnp.full(m_ref.shape, NEG_BIG, F32)
        l_ref[...] = jnp.zeros(l_ref.shape, F32)
        acc_ref[...] = jnp.zeros(acc_ref.shape, F32)
        if use_bias:
            carry_ref[...] = jnp.zeros(carry_ref.shape, F32)

    q = q_ref[0]

    def head_rows(x):
        return jnp.concatenate([jnp.broadcast_to(x[h:h + 1], (Q_ROWS, x.shape[1])) for h in range(HEADS)], axis=0)

    def attend(kts, vts, lfs, causal_new, splits=1):
        n_all = len(kts)
        bias = None
        if use_bias:
            upper = jnp.where(lax.broadcasted_iota(jnp.int32, (LANES, LANES), 0)
                              <= lax.broadcasted_iota(jnp.int32, (LANES, LANES), 1), 1.0, 0.0).astype(BF16)
            rows = HEADS * Q_ROWS
            n = len(lfs)
            within = _dot_sel(jnp.concatenate([head_rows(lf) for lf in lfs], axis=0), upper)
            incl = jnp.broadcast_to(within[:, LANES - 1:LANES], within.shape)
            d = 1
            while d < n:
                incl = incl + jnp.concatenate([jnp.zeros((d * rows, LANES), F32), incl[:(n - d) * rows]], axis=0)
                d *= 2
            carry = carry_ref[...]
            before = jnp.concatenate([jnp.zeros((rows, LANES), F32), incl[:(n - 1) * rows]], axis=0) if n > 1 else 0.0
            c_all = within + before
            carry_ref[...] = carry + incl[(n - 1) * rows:]
            bias = [c_all[i * rows:(i + 1) * rows] + carry for i in range(n)]
        per = n_all // splits
        sls = [slice(c * per, (c + 1) * per) for c in range(splits)]
        ss = [_dot(q, jnp.concatenate(kts[sl], axis=1).astype(BF16)) * scale for sl in sls]
        if use_bias:
            ss = [s - jnp.concatenate(bias[sl], axis=1) for s, sl in zip(ss, sls)]
        if causal_new:
            t = jnp.bitwise_and(lax.broadcasted_iota(jnp.int32, ss[0].shape, 0), Q_ROWS - 1)
            ss = [jnp.where(lax.broadcasted_iota(jnp.int32, s.shape, 1) <= t, s, NEG_BIG) for s in ss]
        ms = [jnp.max(s, axis=1, keepdims=True) for s in ss]
        ps = [jnp.exp(s - m_c) for s, m_c in zip(ss, ms)]
        ls = [jnp.sum(p, axis=1, keepdims=True) for p in ps]
        pvs = [_dot_nt(p.astype(BF16), jnp.concatenate(vts[sl], axis=1).astype(BF16)) for p, sl in zip(ps, sls)]
        parts = list(zip(ms, ls, pvs))
        m_prev = m_ref[...]
        m_new = m_prev
        for m_c, _, _ in parts:
            m_new = jnp.maximum(m_new, m_c)
        alpha = jnp.exp(m_prev - m_new)
        l_new = alpha * l_ref[...]
        acc_new = alpha * acc_ref[...]
        for m_c, l_c, pv_c in parts:
            w_c = jnp.exp(m_c - m_new)
            l_new = l_new + w_c * l_c
            acc_new = acc_new + w_c * pv_c
        l_ref[...] = l_new
        acc_ref[...] = acc_new
        m_ref[...] = m_new

    attend([r[0, 0] for r in k_refs], [r[0, 0] for r in v_refs], [r[0, 0] for r in lf_refs] if use_bias else None, False,
           splits=chains)

    @pl.when(g == pl.num_programs(1) - 1)
    def _finish():
        attend([knew_ref[0]], [vnew_ref[0]], [lfnew_ref[0]] if use_bias else None, True)
        full = acc_ref[...] / l_ref[...]
        lane_head = _group_id((Q_ROWS, C), 1, HEAD_DIM)
        outs = []
        for mp in range(n_maps):
            o = jnp.zeros((Q_ROWS, C), F32)
            for h in range(HEADS):
                r0 = (h * n_maps + mp) * Q_ROWS
                o = o + jnp.where(lane_head == h, full[r0:r0 + Q_ROWS], 0.0)
            outs.append(o)
        if n_maps == 1:
            res = outs[0]
        else:
            d = outs[0] - lam_ref[...] * outs[1]
            head_sum = jnp.where(_group_id((C, C), 0, HEAD_DIM) == _group_id((C, C), 1, HEAD_DIM), 1.0, 0.0).astype(BF16)
            ms = _dot_sel(d * d, head_sum) * (1.0 / HEAD_DIM)
            res = d * lax.rsqrt(ms + NORM_EPS) * gsub_ref[...] * sub_scale
        o_ref[0] = res.astype(o_ref.dtype)


def _paged_attention(q_rows, k_cache_t, v_cache_t, knew_t, vnew_t, page_table, layer, *, pages_per_step, n_maps, scale,
                     lf_cache_t=None, lfnew_t=None, lam=None, gsub=None, sub_scale=1.0, name):
    B, R, C = q_rows.shape
    n_pages_total = page_table.shape[1]
    P = pages_per_step
    use_bias = lf_cache_t is not None
    per_seq = lambda shape: pl.BlockSpec((1,) + shape, lambda b, g, pt: (b,) + (0,) * len(shape))
    const = lambda shape: pl.BlockSpec(shape, lambda b, g, pt: (0,) * len(shape))

    def page_spec(i, tail):
        return pl.BlockSpec((1, 1) + tail, lambda b, g, pt: (layer, pt[b, g * P + i]) + (0,) * len(tail))

    operands = [q_rows]
    in_specs = [per_seq((R, C))]
    if not use_bias:
        operands += [lam.reshape(1, 1), jnp.tile(gsub, HEADS).reshape(1, C)]
        in_specs += [const((1, 1)), const((1, C))]
    operands += [knew_t, vnew_t]
    in_specs += [per_seq((C, LANES)), per_seq((C, LANES))]
    if use_bias:
        operands.append(lfnew_t)
        in_specs.append(per_seq((HEADS, LANES)))
    kv_tail = (C, LANES)
    operands += [k_cache_t] * P + [v_cache_t] * P
    in_specs += [page_spec(i, kv_tail) for i in range(P)] + [page_spec(i, kv_tail) for i in range(P)]
    scratch = [pltpu.VMEM((R, 1), F32), pltpu.VMEM((R, 1), F32), pltpu.VMEM((R, C), F32)]
    if use_bias:
        operands += [lf_cache_t] * P
        in_specs += [page_spec(i, (HEADS, LANES)) for i in range(P)]
        scratch.append(pltpu.VMEM((HEADS * Q_ROWS, LANES), F32))
    return pl.pallas_call(
        functools.partial(_paged_body, n_pages=P, n_maps=n_maps, use_bias=use_bias, scale=scale, sub_scale=sub_scale,
                          chains=math.gcd(P, 4)),
        grid_spec=pltpu.PrefetchScalarGridSpec(
            num_scalar_prefetch=1, grid=(B, n_pages_total // P), in_specs=in_specs,
            out_specs=pl.BlockSpec((1, Q_ROWS, C), lambda b, g, pt: (b, 0, 0)), scratch_shapes=scratch),
        out_shape=jax.ShapeDtypeStruct((B, Q_ROWS, C), F32),
        compiler_params=_params("arbitrary", "arbitrary"),
        name=name,
    )(page_table, *operands)


def _query_rows(q, n_maps):
    B, T, H, M, d = q.shape
    eye = jnp.eye(H * M, dtype=q.dtype).reshape(H, M, H, M)
    rows = jnp.einsum('bthmd,hmgn->bhmtgnd', q, eye)
    rows = jnp.pad(rows, ((0, 0), (0, 0), (0, 0), (0, Q_ROWS - T), (0, 0), (0, 0), (0, 0)))
    return rows.reshape(B, H * M * Q_ROWS, H * M * d).astype(BF16)


def _new_page(x):
    return jnp.pad(jnp.swapaxes(x, 1, 2), ((0, 0), (0, 0), (0, LANES - x.shape[1])))


def _gelu_tanh(x):
    return 0.5 * x * (1.0 + jnp.tanh(math.sqrt(2.0 / math.pi) * (x + 0.044715 * (x * x * x))))


def _lru_body(xc_ref, lg_ref, wa_ref, wx_ref, ba_ref, bx_ref, sp_ref, h0_ref, o_ref, ht_ref, carry_ref, *,
              rows, last_tile, last_row):
    t = pl.program_id(1)

    @pl.when(t == 0)
    def _init():
        carry_ref[...] = h0_ref[0]

    xc = xc_ref[...]
    xb = xc.astype(BF16)
    rg = _sigmoid(_dot(xb, wa_ref[...]) + ba_ref[...])
    ig = _sigmoid(_dot(xb, wx_ref[...]) + bx_ref[...])
    log_a = (-LRU_C) * rg * sp_ref[...]
    a = jnp.exp(log_a)
    b = jnp.sqrt(1.0 - jnp.exp(2.0 * log_a)) * (ig * xc)
    row = lax.broadcasted_iota(jnp.int32, a.shape, 0)
    d = 1
    while d < rows:
        a_prev = jnp.where(row >= d, pltpu.roll(a, d, 0), 1.0)
        b_prev = jnp.where(row >= d, pltpu.roll(b, d, 0), 0.0)
        b = a * b_prev + b
        a = a * a_prev
        d *= 2
    hs = b + a * carry_ref[...]
    carry_ref[...] = hs[rows - 1:rows]
    o_ref[...] = (hs * _gelu_tanh(lg_ref[...])).astype(o_ref.dtype)

    @pl.when(t == last_tile)
    def _state():
        ht_ref[0] = hs[last_row:last_row + 1]


def _lru(xc, lg, wa_bd, wx_bd, ba, bx, sp, h0, *, rows, t_valid, name):
    B, T, W = xc.shape
    nt = T // rows
    vec = lambda a: a.reshape(1, W)
    out, ht = pl.pallas_call(
        functools.partial(_lru_body, rows=rows, last_tile=(t_valid - 1) // rows, last_row=(t_valid - 1) % rows),
        grid=(B, nt),
        in_specs=[pl.BlockSpec((rows, W), lambda b, t: (b * nt + t, 0)),
                  pl.BlockSpec((rows, W), lambda b, t: (b * nt + t, 0)),
                  _resident((W, W)), _resident((W, W)), _resident((1, W)), _resident((1, W)), _resident((1, W)),
                  pl.BlockSpec((1, 1, W), lambda b, t: (b, 0, 0))],
        out_specs=[pl.BlockSpec((rows, W), lambda b, t: (b * nt + t, 0)),
                   pl.BlockSpec((1, 1, W), lambda b, t: (b, 0, 0))],
        out_shape=[jax.ShapeDtypeStruct((B * T, W), BF16), jax.ShapeDtypeStruct((B, 1, W), F32)],
        scratch_shapes=[pltpu.VMEM((1, W), F32)],
        compiler_params=_params("arbitrary", "arbitrary"),
        name=name,
    )(xc.reshape(B * T, W), lg.reshape(B * T, W), wa_bd, wx_bd, vec(ba), vec(bx), vec(sp), h0.reshape(B, 1, W))
    return out, ht.reshape(B, W)


def _gdn_chunk(qs, ks, vs, gbs, Ss, chunk, heads, head0s):
    each = lambda f, *ls: [f(*a) for a in zip(*ls)]
    R = heads * chunk
    SD = heads * HEAD_DIM
    r0 = lax.broadcasted_iota(jnp.int32, (R, R), 0)
    r1 = lax.broadcasted_iota(jnp.int32, (R, R), 1)
    same = _group_id((R, R), 0, chunk) == _group_id((R, R), 1, chunk)
    low = same & (r0 >= r1)
    strict = same & (r0 > r1)
    low_b = jnp.where(low, 1.0, 0.0).astype(BF16)
    same_b = jnp.where(same, 1.0, 0.0).astype(BF16)
    eye_b = jnp.where(r0 == r1, 1.0, 0.0).astype(BF16)

    colmat = each(lambda gb, h0: jnp.concatenate(
        [gb if h == 0 else pltpu.roll(gb, LANES - h, 1) for h in range(h0, h0 + heads)], axis=0), gbs, head0s)
    cparts = each(_split3, colmat)
    sel3 = lambda sel, p: _dot(sel, p[0]) + (_dot(sel, p[1]) + _dot(sel, p[2]))
    cum_cols = each(lambda p: sel3(low_b, p), cparts)
    tot_cols = each(lambda p: sel3(same_b, p), cparts)
    gcum_col = each(lambda c: c[:, 0:1], cum_cols)
    beta_col = each(lambda c: c[:, HEADS:HEADS + 1], colmat)
    gtot_col = each(lambda c: c[:, 0:1], tot_cols)
    cum_parts = each(_split3, cum_cols)
    gcum_row = each(lambda p: (_dot_tn(p[0], eye_b) + (_dot_tn(p[1], eye_b) + _dot_tn(p[2], eye_b)))[0:1, :], cum_parts)

    decay = each(lambda c, r: jnp.where(low, jnp.exp(jnp.where(low, c - r, 0.0)), 0.0), gcum_col, gcum_row)
    kb = each(lambda k: k.astype(BF16), ks)
    kk = each(lambda b: _dot_nt(b, b), kb)
    qk = each(lambda q, b: _dot_nt(q.astype(BF16), b), qs, kb)
    nmat = each(lambda a, d, b: jnp.where(strict, a * d * b, 0.0), kk, decay, beta_col)
    xs = each(lambda v, k, b, c: jnp.concatenate([v * b, k * (b * jnp.exp(c))], axis=1), vs, ks, beta_col, gcum_col)
    pw = each(_split2, nmat)
    xs = each(lambda x, p: x - _dot3(p[0], p[1], *_split2(x)), xs, pw)
    width = 2
    while width < chunk:
        pw = each(lambda p: _split2(_dot3(p[0], p[1], p[0], p[1])), pw)
        xs = each(lambda x, p: x + _dot3(p[0], p[1], *_split2(x)), xs, pw)
        width *= 2
    u = each(lambda x: x[:, 0:HEAD_DIM], xs)
    w = each(lambda x: pltpu.roll(x, HEAD_DIM, 1)[:, 0:HEAD_DIM], xs)

    head_match = _group_id((R, SD), 0, chunk) == _group_id((R, SD), 1, HEAD_DIM)

    def spread(x):
        return jnp.where(head_match, jnp.concatenate([x] * heads, axis=1), 0.0).astype(BF16)

    sb = each(lambda S: S.astype(BF16), Ss)
    v_new = each(lambda u_, w_, s: u_ - _dot(spread(w_), s), u, w, sb)
    vb = each(lambda x: x.astype(BF16), v_new)
    o = each(lambda q, c, s, a, d, x: _dot(spread(q * jnp.exp(c)), s) + _dot((a * d).astype(BF16), x),
             qs, gcum_col, sb, qk, decay, vb)
    kg = each(lambda k, t, c: spread(k * jnp.exp(t - c)), ks, gtot_col, gcum_col)
    tok = jnp.bitwise_and(lax.broadcasted_iota(jnp.int32, (R, SD), 0), chunk - 1)
    first = jnp.where(head_match & (tok == 0), 1.0, 0.0).astype(BF16)
    gtot_state = each(lambda p: _dot_tn(first, p[0]) + (_dot_tn(first, p[1]) + _dot_tn(first, p[2])),
                      each(_split3, tot_cols))
    s_new = each(lambda S, t, g, x: S * jnp.exp(t[:, 0:1]) + _dot_tn(g, x), Ss, gtot_state, kg, vb)
    return o, s_new


def _gdn_body(q_ref, k_ref, v_ref, gb_ref, s0_ref, o_ref, sT_ref, s_ref, *, chunk, group):
    c = pl.program_id(1)

    @pl.when(c == 0)
    def _init():
        s_ref[...] = s0_ref[...]

    hpg = min(HEADS, LANES // chunk)
    R = hpg * chunk
    probs = [(g, h0) for g in range(group) for h0 in range(0, HEADS, hpg)]
    hsl = lambda h0: slice(h0, h0 + hpg)
    rsl = lambda h0: slice(h0 * HEAD_DIM, (h0 + hpg) * HEAD_DIM)
    o, s_new = _gdn_chunk([q_ref[g, hsl(h0)].reshape(R, HEAD_DIM) for g, h0 in probs],
                          [k_ref[g, hsl(h0)].reshape(R, HEAD_DIM) for g, h0 in probs],
                          [v_ref[g, hsl(h0)].reshape(R, HEAD_DIM) for g, h0 in probs],
                          [gb_ref[g] for g, _ in probs], [s_ref[g, rsl(h0)] for g, h0 in probs], chunk, hpg,
                          [h0 for _, h0 in probs])
    for i, (g, h0) in enumerate(probs):
        o_ref[g, hsl(h0)] = o[i].reshape(hpg, chunk, HEAD_DIM)
        s_ref[g, rsl(h0)] = s_new[i]

    @pl.when(c == pl.num_programs(1) - 1)
    def _state():
        sT_ref[...] = s_ref[...]


def _gdn(q, k, v, gb, S0, *, chunk, group, name):
    B, H, T, hd = q.shape
    nc = T // chunk
    SD = H * hd
    tok = pl.BlockSpec((group, H, chunk, hd), lambda b, c: (b, 0, c, 0))
    st = pl.BlockSpec((group, SD, hd), lambda b, c: (b, 0, 0))
    return pl.pallas_call(
        functools.partial(_gdn_body, chunk=chunk, group=group),
        grid=(B // group, nc),
        in_specs=[tok, tok, tok, pl.BlockSpec((group, chunk, LANES), lambda b, c: (b, c, 0)), st],
        out_specs=[tok, st],
        out_shape=[jax.ShapeDtypeStruct((B, H, T, hd), F32), jax.ShapeDtypeStruct((B, SD, hd), F32)],
        scratch_shapes=[pltpu.VMEM((group, SD, hd), F32)],
        compiler_params=_params("arbitrary", "arbitrary"),
        name=name,
    )(q, k, v, gb, S0)


def _log1p_exp_neg_abs(x):
    return jnp.log(1.0 + jnp.exp(-jnp.abs(x)))


def _conv_tap_sum(x, tail, w):
    rows = x.shape[0]
    first = lax.broadcasted_iota(jnp.int32, (8, x.shape[1]), 0)
    y = x * w[CONV_W - 1:CONV_W]
    for back in range(1, CONV_W):
        rolled = pltpu.roll(x, back, 0)
        top = jnp.where(first < back, pltpu.roll(tail, back, 0), rolled[0:8])
        y = y + jnp.concatenate([top, rolled[8:rows]], axis=0) * w[CONV_W - 1 - back:CONV_W - back]
    return y


def _prep_body(fq_ref, fk_ref, fv_ref, lx_ref, dq_ref, dk_ref, dv_ref, gq_ref, gk_ref, gv_ref, sm_ref,
               fqg_ref, fkg_ref, fb_ref, dqg_ref, dkg_ref, cos_ref, sin_ref, gw_ref, lw_ref, lb_ref, ga_ref, gdt_ref,
               fqt_o, fk_o, fvt_o, fkT_o, fvT_o, flT_o, ck_o, dqt_o, dk_o, dvt_o, dkT_o, dvT_o,
               gq_o, gk_o, gv_o, gb_o, xc_o, lst_o, gst_o, ltail, gtail, ccarry, *, rows):
    t = pl.program_id(1)
    C = HEADS * HEAD_DIM

    @pl.when(t == 0)
    def _init():
        ltail[...] = jnp.zeros(ltail.shape, F32)
        gtail[...] = jnp.zeros(gtail.shape, F32)
        ccarry[...] = jnp.zeros(ccarry.shape, F32)

    def group_sum(width):
        return jnp.where(_group_id((C, C), 0, width) == _group_id((C, C), 1, width), 1.0, 0.0).astype(BF16)

    sum64, sum32 = group_sum(HEAD_DIM), group_sum(DIFF_HD)

    def group_rms(x, sel, width):
        return x * lax.rsqrt(_dot_sel(x * x, sel) * (1.0 / width) + NORM_EPS)

    kn = group_rms(fk_ref[...], sum64, FOX_HD) * fkg_ref[...]
    qn = group_rms(fq_ref[...], sum64, FOX_HD) * fqg_ref[...] * (1.0 / math.sqrt(FOX_HD))
    fv = fv_ref[...]
    knT, fvT = kn.T, fv.T
    fqt_o[0] = qn.T.astype(BF16)
    fk_o[...] = kn.astype(BF16)
    fvt_o[0] = fvT.astype(BF16)
    fkT_o[0] = knT
    fvT_o[0] = fvT
    sm = sm_ref[...]
    xf = sm + fb_ref[...]
    logf = jnp.minimum(xf, 0.0) - _log1p_exp_neg_abs(xf)
    flT_o[0] = logf.T[0:8]
    lower = jnp.where(lax.broadcasted_iota(jnp.int32, (rows, rows), 0) >= lax.broadcasted_iota(jnp.int32, (rows, rows), 1),
                      1.0, 0.0).astype(BF16)
    c = _sel_dot(lower, logf) + ccarry[...]
    ccarry[...] = c[rows - 1:rows]
    ck_o[...] = c

    lane = lax.broadcasted_iota(jnp.int32, (rows, LANES), 1)
    first_half = jnp.bitwise_and(lane, DIFF_HD - 1) < DIFF_HD // 2

    def swap_halves(x):
        parts = []
        for c0 in range(0, C, LANES):
            h = x[:, c0:c0 + LANES]
            parts.append(jnp.where(first_half, pltpu.roll(h, LANES - DIFF_HD // 2, 1), pltpu.roll(h, DIFF_HD // 2, 1)))
        return jnp.concatenate(parts, axis=1)

    def rotary(x):
        return x * cos_ref[...] + swap_halves(x) * sin_ref[...]

    dkr = rotary(group_rms(dk_ref[...], sum32, DIFF_HD) * dkg_ref[...])
    dqr = rotary(group_rms(dq_ref[...], sum32, DIFF_HD) * dqg_ref[...]) * (1.0 / math.sqrt(DIFF_HD))
    dv = dv_ref[...]
    dkrT, dvT = dkr.T, dv.T
    dqt_o[0] = dqr.T.astype(BF16)
    dk_o[...] = dkr.astype(BF16)
    dvt_o[0] = dvT.astype(BF16)
    dkT_o[0] = dkrT
    dvT_o[0] = dvT

    gw = gw_ref[...]
    raws = (gq_ref[...], gk_ref[...], gv_ref[...])
    acts = []
    for i, raw in enumerate(raws):
        y = _conv_tap_sum(raw, gtail[:, i * C:(i + 1) * C], gw[:, i * C:(i + 1) * C])
        acts.append(y * _sigmoid(y))
    gqn = acts[0] * lax.rsqrt(_dot_sel(acts[0] * acts[0], sum64) + NORM_EPS) * (GDN_HD ** -0.5)
    gkn = acts[1] * lax.rsqrt(_dot_sel(acts[1] * acts[1], sum64) + NORM_EPS)
    pick_r = lax.broadcasted_iota(jnp.int32, (C, HEAD_DIM), 0)
    pick_c = lax.broadcasted_iota(jnp.int32, (C, HEAD_DIM), 1)
    for h in range(HEADS):
        pick = jnp.where(pick_r == pick_c + h * HEAD_DIM, 1.0, 0.0).astype(BF16)
        gq_o[0, h] = _dot_sel(gqn, pick)
        gk_o[0, h] = _dot_sel(gkn, pick)
        gv_o[0, h] = _dot_sel(acts[2], pick)
    sh = pltpu.roll(sm, LANES - HEADS, 1)
    xa = sh + gdt_ref[...]
    decay = -ga_ref[...] * (jnp.maximum(xa, 0.0) + _log1p_exp_neg_abs(xa))
    gb_o[...] = jnp.where(lane < HEADS, decay, _sigmoid(sh))

    lx = lx_ref[...]
    xc_o[...] = _conv_tap_sum(lx, ltail[...], lw_ref[...]) + lb_ref[...]

    ltail[...] = lx[rows - 8:rows]
    gtail[...] = jnp.concatenate([r[rows - 8:rows] for r in raws], axis=1)
    lst_o[0] = ltail[...]
    gst_o[0] = gtail[...]


def _prep(zm2, B, T, p, cos_t, sin_t, *, rows, name):
    M = B * T
    nt = T // rows
    C = HEADS * HEAD_DIM
    colblk = lambda j, w=C: pl.BlockSpec((rows, w), lambda b, t: (b * nt + t, j))
    vec = lambda a, w=C: jnp.pad(a.reshape(1, -1), ((0, 0), (0, w - a.size)))
    tile4 = lambda g: jnp.tile(g, C // g.size).reshape(1, C)
    in_specs = ([colblk(j) for j in (0, 1, 2, 3, 5, 6, 7, 8, 9, 10)] + [colblk(_C_SMALL // LANES, LANES)]
                + [_resident((1, C))] * 2 + [_resident((1, LANES))] + [_resident((1, C))] * 2
                + [pl.BlockSpec((rows, C), lambda b, t: (t, 0))] * 2
                + [_resident((CONV_W, 3 * C)), _resident((CONV_W, C)), _resident((1, C)), _resident((1, LANES)),
                   _resident((1, LANES))])
    chan = lambda dt: (jax.ShapeDtypeStruct((B, C, T), dt), pl.BlockSpec((1, C, rows), lambda b, t: (b, 0, t)))
    tokm = lambda w, dt: (jax.ShapeDtypeStruct((M, w), dt), pl.BlockSpec((rows, w), lambda b, t: (b * nt + t, 0)))
    headm = (jax.ShapeDtypeStruct((B, HEADS, T, HEAD_DIM), F32),
             pl.BlockSpec((1, HEADS, rows, HEAD_DIM), lambda b, t: (b, 0, t, 0)))
    state = lambda w: (jax.ShapeDtypeStruct((B, 8, w), F32), pl.BlockSpec((1, 8, w), lambda b, t: (b, 0, 0)))
    outs = [chan(BF16), tokm(C, BF16), chan(BF16), chan(F32), chan(F32),
            (jax.ShapeDtypeStruct((B, 8, T), F32), pl.BlockSpec((1, 8, rows), lambda b, t: (b, 0, t))), tokm(LANES, F32),
            chan(BF16), tokm(C, BF16), chan(BF16), chan(F32), chan(F32),
            headm, headm, headm, tokm(LANES, F32), tokm(C, F32), state(C), state(3 * C)]
    names = ['fqt', 'fk', 'fvt', 'fkT', 'fvT', 'flT', 'ck', 'dqt', 'dk', 'dvt', 'dkT', 'dvT', 'gq', 'gk', 'gv', 'gb', 'xc',
             'lru_tail', 'gdn_tail']
    res = pl.pallas_call(
        functools.partial(_prep_body, rows=rows),
        grid=(B, nt),
        in_specs=in_specs,
        out_specs=[o[1] for o in outs],
        out_shape=[o[0] for o in outs],
        scratch_shapes=[pltpu.VMEM((8, C), F32), pltpu.VMEM((8, 3 * C), F32), pltpu.VMEM((1, LANES), F32)],
        compiler_params=_params("arbitrary", "arbitrary"),
        name=name,
    )(*([zm2] * 11), tile4(p['fox_q_norm']), tile4(p['fox_k_norm']), vec(p['fox_b_f'], LANES),
      tile4(p['diff_q_norm']), tile4(p['diff_k_norm']), cos_t, sin_t, p['gdn_conv_w'], p['lru_conv_w'],
      p['lru_conv_b'].reshape(1, C), vec(jnp.exp(p['gdn_A_log']), LANES), vec(p['gdn_dt_bias'], LANES))
    return dict(zip(names, res))


def _rope_tables(pos):
    inv = ROPE_THETA ** (-jnp.arange(0, DIFF_HD, 2, dtype=F32) / DIFF_HD)
    ang = pos.astype(F32)[:, None] * inv[None, :]
    cos = jnp.tile(jnp.cos(ang), (1, 2 * BR_WIDTH // DIFF_HD))
    sin = jnp.tile(jnp.concatenate([-jnp.sin(ang), jnp.sin(ang)], axis=1), (1, BR_WIDTH // DIFF_HD))
    return cos, sin


def _merge_body(x_ref, b0_ref, b1_ref, b2_ref, b3_ref, gates_ref, wb_ref, wo_ref, o_ref):
    projs = [_dot(b_ref[...], wb_ref[n]) for n, b_ref in enumerate((b0_ref, b1_ref, b2_ref, b3_ref))]
    merged = None
    for n, proj in enumerate(projs):
        term = _sigmoid(gates_ref[:, n * D_MODEL:(n + 1) * D_MODEL]) * proj
        merged = term if merged is None else merged + term
    o_ref[...] = x_ref[...] + _dot(merged.astype(BF16), wo_ref[...])


def _merge(x2, branches, gates, wb, wo, *, row_tile, name):
    M, D = x2.shape
    row = lambda w: pl.BlockSpec((row_tile, w), lambda i: (i, 0))
    return pl.pallas_call(
        _merge_body,
        grid=(M // row_tile,),
        in_specs=[row(D)] + [row(BR_WIDTH)] * N_BRANCH + [row(N_BRANCH * D), _resident(wb.shape), _resident(wo.shape)],
        out_specs=row(D),
        out_shape=jax.ShapeDtypeStruct((M, D), F32),
        compiler_params=_params("arbitrary"),
        name=name,
    )(x2, *branches, gates, wb, wo)


def _mem_body(x_ref, g_ref, wq_ref, qg_ref, mk_ref, mv_ref, wo_ref, o_ref):
    x = x_ref[0]
    h = (x * _rms_scale(x) * g_ref[...]).astype(BF16)
    q = _dot(h, wq_ref[...])
    W = q.shape[1]
    head_sum = jnp.where(_group_id((W, W), 0, MEM_HD) == _group_id((W, W), 1, MEM_HD), 1.0, 0.0).astype(BF16)
    ms = _dot_sel(q * q, head_sum) * (1.0 / MEM_HD)
    qn = q * lax.rsqrt(ms + NORM_EPS) * qg_ref[...] * (1.0 / math.sqrt(MEM_HD))
    mk = mk_ref[0].astype(BF16)
    mv = mv_ref[0].astype(BF16)
    lane_head = _group_id(q.shape, 1, MEM_HD)
    heads = range(MEM_HEADS)
    ss = [_dot_nt(jnp.where(lane_head == hh, qn, 0.0).astype(BF16), mk) for hh in heads]
    ps = [jnp.exp(s - jnp.max(s, axis=1, keepdims=True)) for s in ss]
    ps = [p / jnp.sum(p, axis=1, keepdims=True) for p in ps]
    pvs = [_dot(p.astype(BF16), mv) for p in ps]
    o = jnp.zeros(q.shape, F32)
    for hh in heads:
        o = o + jnp.where(lane_head == hh, pvs[hh], 0.0)
    o_ref[0] = x + _dot(o.astype(BF16), wo_ref[...])


def _mem_sublayer(x, g, wq, qg, mk, mv, wo, *, row_tile, name):
    B, T, D = x.shape
    nt = T // row_tile
    kv = pl.BlockSpec((1,) + mk.shape[1:], lambda b, t: (b, 0, 0))
    return pl.pallas_call(
        _mem_body,
        grid=(B, nt),
        in_specs=[pl.BlockSpec((1, row_tile, D), lambda b, t: (b, t, 0)), _resident((1, D)), _resident(wq.shape),
                  _resident((1, MEM_WIDTH)), kv, kv, _resident(wo.shape)],
        out_specs=pl.BlockSpec((1, row_tile, D), lambda b, t: (b, t, 0)),
        out_shape=jax.ShapeDtypeStruct((B, T, D), F32),
        compiler_params=_params("arbitrary", "arbitrary"),
        name=name,
    )(x, g.reshape(1, D), wq, jnp.tile(qg, MEM_HEADS).reshape(1, MEM_WIDTH), mk, mv, wo)


def _mlp_body(x_ref, g_ref, wu_ref, wd_ref, o_ref, *, ff_chunk):
    x = x_ref[...]
    h = (x * _rms_scale(x) * g_ref[...]).astype(BF16)
    acc = x
    for c in range(D_FF // ff_chunk):
        u = _dot(h, wu_ref[:, c * ff_chunk:(c + 1) * ff_chunk])
        u = jnp.square(jnp.maximum(u, 0.0)).astype(BF16)
        acc = acc + _dot(u, wd_ref[c * ff_chunk:(c + 1) * ff_chunk, :])
    o_ref[...] = acc


def _mlp_sublayer(x2, g, w_up, w_down, *, row_tile, name):
    M, D = x2.shape
    return pl.pallas_call(
        functools.partial(_mlp_body, ff_chunk=1024),
        grid=(M // row_tile,),
        in_specs=[pl.BlockSpec((row_tile, D), lambda i: (i, 0)), _resident((1, D)), _resident(w_up.shape),
                  _resident(w_down.shape)],
        out_specs=pl.BlockSpec((row_tile, D), lambda i: (i, 0)),
        out_shape=jax.ShapeDtypeStruct((M, D), F32),
        compiler_params=_params("arbitrary"),
        name=name,
    )(x2, g.reshape(1, D), w_up, w_down)


def _rmsnorm(x, g):
    xf = x.astype(F32)
    return xf * lax.rsqrt(jnp.mean(xf * xf, axis=-1, keepdims=True) + NORM_EPS) * g.astype(F32)


def _l2norm(x):
    return x * lax.rsqrt(jnp.sum(x * x, axis=-1, keepdims=True) + NORM_EPS)


def _rope(x, pos):
    d = x.shape[-1]
    inv = ROPE_THETA ** (-jnp.arange(0, d, 2, dtype=F32) / d)
    ang = pos.astype(F32)[:, None] * inv[None, :]
    ang = ang.reshape((1, ang.shape[0]) + (1,) * (x.ndim - 3) + (d // 2,))
    cos, sin = jnp.cos(ang), jnp.sin(ang)
    x1, x2 = x[..., : d // 2], x[..., d // 2:]
    return jnp.concatenate([x1 * cos - x2 * sin, x2 * cos + x1 * sin], axis=-1)


def _causal_conv(x, buf, w, b=None):
    xp = jnp.concatenate([buf.astype(x.dtype), x], axis=1)
    T = x.shape[1]
    y = xp[:, 0:T] * w[0]
    for i in range(1, CONV_W):
        y = y + xp[:, i:i + T] * w[i]
    if b is not None:
        y = y + b
    return y, xp[:, -(CONV_W - 1):]


def _key_minor_pages(cache):
    nd = cache.ndim
    t = jnp.transpose(cache, (0, 1) + tuple(range(3, nd)) + (2,))
    return t.reshape(cache.shape[:2] + (-1, cache.shape[2]))


def _head_major(x):
    return jnp.swapaxes(x, 1, 2)


def _pad_time(x, t_pad, axis=1):
    pad = [(0, 0)] * x.ndim
    pad[axis] = (0, t_pad - x.shape[axis])
    return jnp.pad(x, pad)


def _pack_layer_weights(p):
    o = np.cumsum((0,) + (3 * BR_WIDTH, FOX_HEADS, LRU_WIDTH, LRU_WIDTH, 3 * BR_WIDTH, 3 * BR_WIDTH, GDN_HEADS,
                          GDN_HEADS, BR_WIDTH, N_BRANCH * D_MODEL)).tolist()
    wt = p['w_in'].T
    seg = lambda i: wt[o[i]:o[i + 1]]
    small = jnp.concatenate([seg(1), seg(6), seg(7), jnp.zeros((SMALL_COLS - 3 * HEADS, D_MODEL), wt.dtype)], axis=0)
    w_main = jnp.concatenate([seg(0), seg(2), seg(3), seg(4), seg(5), seg(8), small], axis=0).astype(BF16)
    blockdiag = lambda wb: jax.scipy.linalg.block_diag(*[wb[i] for i in range(LRU_BLOCKS)]).astype(BF16)
    return dict(w_main=w_main, w_gates=seg(9).astype(BF16), lru_wa=blockdiag(p['lru_w_a']),
                lru_wx=blockdiag(p['lru_w_x']), w_branch=p['w_branch'].astype(BF16), w_out=p['w_out'].astype(BF16),
                w_mem_q=p['w_mem_q'].astype(BF16), w_mem_kv=p['w_mem_kv'].astype(BF16),
                w_mem_o=p['w_mem_o'].astype(BF16), w_mlp_up=p['w_mlp_up'].astype(BF16),
                w_mlp_down=p['w_mlp_down'].astype(BF16))


_C_FOX, _C_LRUX, _C_LRUG, _C_DIFF, _C_GDN, _C_GDNZ, _C_SMALL = 0, 768, 1024, 1280, 2048, 2816, 3072


def _gdn_gate(go, z, g):
    B, T = go.shape[:2]
    out = _rmsnorm(go, g) * jax.nn.silu(z.reshape(B, T, GDN_HEADS, GDN_HD))
    return out.reshape(B * T, BR_WIDTH).astype(BF16)


def _mixer_sublayer_prompt(x, p, pw, lam_init, rope_tables, *, row_tile, attn_tile, attn_chunk, prep_rows, tag):
    B, T, D = x.shape
    M = B * T
    x2 = x.reshape(M, D)
    zm, gates = _norm_matmul(x2, p['norm_mix'], [pw['w_main'], pw['w_gates']], row_tile=row_tile, name=tag + "in_proj",
                             transposed=True)
    pre = _prep(zm, B, T, p, *rope_tables, rows=prep_rows, name=tag + "prep")
    fox_out = _flash_attention_t(pre['fqt'], pre['fk'], pre['fvt'], n_maps=1, tq=attn_tile, tk=attn_chunk, ck=pre['ck'],
                                 name=tag + "fox_attn")
    lp = p['diff_lambda']
    lam = jnp.exp(jnp.sum(lp[0] * lp[1])) - jnp.exp(jnp.sum(lp[2] * lp[3])) + lam_init
    diff_out = _flash_attention_t(pre['dqt'], pre['dk'], pre['dvt'], n_maps=2, tq=attn_tile, tk=attn_chunk, lam=lam,
                                  gsub=p['diff_sub_norm'], sub_scale=1.0 - lam_init, name=tag + "diff_attn")
    zm3 = zm.reshape(B, T, MAIN_COLS)
    lru_out, lru_hT = _lru(pre['xc'].reshape(B, T, LRU_WIDTH), zm3[..., _C_LRUG:_C_LRUG + LRU_WIDTH], pw['lru_wa'],
                           pw['lru_wx'], p['lru_b_a'], p['lru_b_x'], jax.nn.softplus(-p['lru_lambda']),
                           jnp.zeros((B, LRU_WIDTH), F32), rows=256, t_valid=T, name=tag + "lru")
    go, gS = _gdn(pre['gq'], pre['gk'], pre['gv'], pre['gb'].reshape(B, T, LANES),
                  jnp.zeros((B, GDN_HEADS * GDN_HD, GDN_HD), F32), chunk=64, group=math.gcd(B, 8), name=tag + "gdn")
    gdn_out = _gdn_gate(_head_major(go), zm3[..., _C_GDNZ:_C_GDNZ + BR_WIDTH], p['gdn_out_norm'])
    x2 = _merge(x2, (fox_out, lru_out, diff_out, gdn_out), gates, pw['w_branch'], pw['w_out'], row_tile=row_tile,
                name=tag + "merge")
    token_major = lambda a: jnp.swapaxes(a, 1, 2)
    state = (token_major(pre['fkT']).reshape(B, T, FOX_HEADS, FOX_HD),
             token_major(pre['fvT']).reshape(B, T, FOX_HEADS, FOX_HD),
             token_major(pre['flT'][:, :FOX_HEADS]),
             token_major(pre['dkT']).reshape(B, T, DIFF_HEADS, 2, DIFF_HD),
             token_major(pre['dvT']).reshape(B, T, DIFF_HEADS, DIFF_VD),
             lru_hT, pre['lru_tail'][:, 8 - (CONV_W - 1):], gS.reshape(B, GDN_HEADS, GDN_HD, GDN_HD),
             pre['gdn_tail'][:, 8 - (CONV_W - 1):])
    return x2.reshape(B, T, D), state


def _mixer_sublayer(x, pos, p, pw, lam_init, fox_past, diff_past, lru_h0, lru_buf, gdn_S0, gdn_buf, *, row_tile, tag):
    B, T, D = x.shape
    M = B * T
    x2 = x.reshape(M, D)
    zm, gates = _norm_matmul(x2, p['norm_mix'], [pw['w_main'], pw['w_gates']], row_tile=row_tile, name=tag + "in_proj",
                             transposed=True)
    zm = zm.reshape(B, T, MAIN_COLS)
    seg = lambda c0, w: zm[..., c0:c0 + w]

    r = seg(_C_FOX, 3 * BR_WIDTH).reshape(B, T, 3, FOX_HEADS, FOX_HD)
    fq = _rmsnorm(r[:, :, 0], p['fox_q_norm'])
    fk = _rmsnorm(r[:, :, 1], p['fox_k_norm'])
    fv = r[:, :, 2]
    flogf = jax.nn.log_sigmoid(seg(_C_SMALL, FOX_HEADS) + p['fox_b_f'])
    fo = _paged_attention(_query_rows(fq[:, :, :, None], 1), fox_past['k'], fox_past['v'],
                          _new_page(fk.reshape(B, T, BR_WIDTH)), _new_page(fv.reshape(B, T, BR_WIDTH)),
                          fox_past['page_table'], fox_past['layer'], pages_per_step=fox_past['pages_per_step'],
                          n_maps=1, scale=1.0 / math.sqrt(FOX_HD), lf_cache_t=fox_past['logf'],
                          lfnew_t=_new_page(flogf), name=tag + "fox_paged")
    fox_out = fo[:, :T].reshape(M, BR_WIDTH).astype(BF16)

    xc, lru_buf_new = _causal_conv(seg(_C_LRUX, LRU_WIDTH), lru_buf, p['lru_conv_w'], p['lru_conv_b'])
    lru_rows = min(256, T) if T % 8 == 0 else 8
    t_pad = -(-T // lru_rows) * lru_rows
    lru_out, lru_hT = _lru(_pad_time(xc, t_pad), _pad_time(seg(_C_LRUG, LRU_WIDTH), t_pad), pw['lru_wa'], pw['lru_wx'],
                           p['lru_b_a'], p['lru_b_x'], jax.nn.softplus(-p['lru_lambda']), lru_h0,
                           rows=lru_rows, t_valid=T, name=tag + "lru")
    if t_pad != T:
        lru_out = lru_out.reshape(B, t_pad, LRU_WIDTH)[:, :T].reshape(M, LRU_WIDTH)

    r = seg(_C_DIFF, 3 * BR_WIDTH).reshape(B, T, 3, DIFF_HEADS, DIFF_VD)
    dq = _rope(_rmsnorm(r[:, :, 0].reshape(B, T, DIFF_HEADS, 2, DIFF_HD), p['diff_q_norm']), pos)
    dk = _rope(_rmsnorm(r[:, :, 1].reshape(B, T, DIFF_HEADS, 2, DIFF_HD), p['diff_k_norm']), pos)
    dvv = r[:, :, 2]
    lp = p['diff_lambda']
    lam = jnp.exp(jnp.sum(lp[0] * lp[1])) - jnp.exp(jnp.sum(lp[2] * lp[3])) + lam_init
    do = _paged_attention(_query_rows(dq, 2), diff_past['k'], diff_past['v'],
                          _new_page(dk.reshape(B, T, BR_WIDTH)), _new_page(dvv.reshape(B, T, BR_WIDTH)),
                          diff_past['page_table'], diff_past['layer'], pages_per_step=diff_past['pages_per_step'],
                          n_maps=2, scale=1.0 / math.sqrt(DIFF_HD), lam=lam, gsub=p['diff_sub_norm'],
                          sub_scale=1.0 - lam_init, name=tag + "diff_paged")
    diff_out = do[:, :T].reshape(M, BR_WIDTH).astype(BF16)

    gc, gdn_buf_new = _causal_conv(seg(_C_GDN, 3 * BR_WIDTH), gdn_buf, p['gdn_conv_w'])
    gc = jax.nn.silu(gc).reshape(B, T, 3, GDN_HEADS, GDN_HD)
    gq = _l2norm(gc[:, :, 0]) * (GDN_HD ** -0.5)
    gk = _l2norm(gc[:, :, 1])
    gvv = gc[:, :, 2]
    gbeta = jax.nn.sigmoid(seg(_C_SMALL + 2 * HEADS, GDN_HEADS))
    gg = -jnp.exp(p['gdn_A_log']) * jax.nn.softplus(seg(_C_SMALL + HEADS, GDN_HEADS) + p['gdn_dt_bias'])
    chunk = 64 if T % 64 == 0 else 32
    tg = -(-T // chunk) * chunk
    gb = _pad_lanes(_pad_time(jnp.concatenate([gg, gbeta], axis=-1), tg))
    hm = lambda a: _head_major(_pad_time(a, tg))
    go, gS = _gdn(hm(gq), hm(gk), hm(gvv), gb, gdn_S0.reshape(B, GDN_HEADS * GDN_HD, GDN_HD), chunk=chunk,
                  group=math.gcd(B, 4), name=tag + "gdn")
    gdn_out = _gdn_gate(_head_major(go)[:, :T], seg(_C_GDNZ, BR_WIDTH), p['gdn_out_norm'])

    x2 = _merge(x2, (fox_out, lru_out, diff_out, gdn_out), gates, pw['w_branch'], pw['w_out'], row_tile=row_tile,
                name=tag + "merge")
    state = (fk, fv, flogf, dk, dvv, lru_hT, lru_buf_new, gS.reshape(B, GDN_HEADS, GDN_HD, GDN_HD), gdn_buf_new)
    return x2.reshape(B, T, D), state


def _memory_kv(mem, p, pw):
    B, Tm, D = mem.shape
    kv, = _norm_matmul(mem.reshape(B * Tm, D), p['norm_mem_src'], [pw['w_mem_kv']], row_tile=256, name="mem_kv")
    kv = kv.reshape(B, Tm, 2, MEM_HEADS, MEM_HD)
    return _rmsnorm(kv[:, :, 0], p['mem_k_norm']), kv[:, :, 1]


def _memory_sublayer(x, mk, mv, p, pw, *, row_tile, name):
    B, T, D = x.shape
    t_pad = -(-T // row_tile) * row_tile
    flat = lambda a: a.reshape(a.shape[0], a.shape[1], MEM_WIDTH)
    out = _mem_sublayer(_pad_time(x, t_pad), p['norm_mem'], pw['w_mem_q'], p['mem_q_norm'], flat(mk), flat(mv),
                        pw['w_mem_o'], row_tile=row_tile, name=name)
    return out[:, :T]


def kernel(x_prompt, x_sample, cache_fox_k, cache_fox_v, cache_fox_logf, cache_diff_k, cache_diff_v, cache_mem_k, cache_mem_v, state_lru_h, state_lru_conv, state_gdn_S, state_gdn_conv, page_table, mem_prompt, norm_mix, w_in, fox_b_f, fox_q_norm, fox_k_norm, lru_conv_w, lru_conv_b, lru_w_a, lru_b_a, lru_w_x, lru_b_x, lru_lambda, diff_q_norm, diff_k_norm, diff_lambda, diff_sub_norm, gdn_conv_w, gdn_A_log, gdn_dt_bias, gdn_out_norm, w_branch, w_out, norm_mem, norm_mem_src, w_mem_q, w_mem_kv, mem_q_norm, mem_k_norm, w_mem_o, norm_mlp, w_mlp_up, w_mlp_down):
    stacked = dict(norm_mix=norm_mix, w_in=w_in, fox_b_f=fox_b_f, fox_q_norm=fox_q_norm, fox_k_norm=fox_k_norm,
                   lru_conv_w=lru_conv_w, lru_conv_b=lru_conv_b, lru_w_a=lru_w_a, lru_b_a=lru_b_a,
                   lru_w_x=lru_w_x, lru_b_x=lru_b_x, lru_lambda=lru_lambda, diff_q_norm=diff_q_norm,
                   diff_k_norm=diff_k_norm, diff_lambda=diff_lambda, diff_sub_norm=diff_sub_norm,
                   gdn_conv_w=gdn_conv_w, gdn_A_log=gdn_A_log, gdn_dt_bias=gdn_dt_bias, gdn_out_norm=gdn_out_norm,
                   w_branch=w_branch, w_out=w_out, norm_mem=norm_mem, norm_mem_src=norm_mem_src, w_mem_q=w_mem_q,
                   w_mem_kv=w_mem_kv, mem_q_norm=mem_q_norm, mem_k_norm=mem_k_norm, w_mem_o=w_mem_o,
                   norm_mlp=norm_mlp, w_mlp_up=w_mlp_up, w_mlp_down=w_mlp_down)
    Bp, Tp, D = x_prompt.shape
    Bs, Ts, _ = x_sample.shape
    past_len = page_table.shape[1] * cache_fox_k.shape[2]
    rope_p = _rope_tables(jnp.arange(Tp))
    pos_s = past_len + jnp.arange(Ts)

    fox_k_t, fox_v_t, diff_k_t, diff_v_t = (_key_minor_pages(c) for c in (cache_fox_k, cache_fox_v, cache_diff_k,
                                                                      cache_diff_v))
    fox_logf_t = jnp.swapaxes(cache_fox_logf, 2, 3)

    xp, xs = x_prompt, x_sample
    pst = [[] for _ in range(11)]
    sst = [[] for _ in range(9)]
    for l in range(DEPTH):
        p = {name: arr[l] for name, arr in stacked.items()}
        pw = _pack_layer_weights(p)
        lam_init = 0.8 - 0.6 * math.exp(-0.3 * l)
        xp, st_p = _mixer_sublayer_prompt(xp, p, pw, lam_init, rope_p, row_tile=256, attn_tile=512, attn_chunk=512,
                                          prep_rows=256, tag="p_")
        mk, mv = _memory_kv(mem_prompt, p, pw)
        xp = _memory_sublayer(xp, mk, mv, p, pw, row_tile=512, name="p_mem")
        xp = _mlp_sublayer(xp.reshape(Bp * Tp, D), p['norm_mlp'], pw['w_mlp_up'], pw['w_mlp_down'], row_tile=512,
                           name="p_mlp").reshape(Bp, Tp, D)
        for lst, val in zip(pst, (st_p[0], st_p[1], st_p[2], st_p[3], st_p[4], mk, mv,
                                  st_p[5], st_p[6], st_p[7], st_p[8])):
            lst.append(val)
        fox_past = dict(k=fox_k_t, v=fox_v_t, logf=fox_logf_t, page_table=page_table, layer=l, pages_per_step=32)
        diff_past = dict(k=diff_k_t, v=diff_v_t, page_table=page_table, layer=l, pages_per_step=32)
        xs, st_s = _mixer_sublayer(xs, pos_s, p, pw, lam_init, fox_past, diff_past, state_lru_h[l], state_lru_conv[l],
                                   state_gdn_S[l], state_gdn_conv[l], row_tile=Bs * Ts, tag="s_")
        xs = _memory_sublayer(xs, cache_mem_k[l], cache_mem_v[l], p, pw, row_tile=8, name="s_mem")
        xs = _mlp_sublayer(xs.reshape(Bs * Ts, D), p['norm_mlp'], pw['w_mlp_up'], pw['w_mlp_down'], row_tile=Bs * Ts,
                           name="s_mlp").reshape(Bs, Ts, D)
        for lst, val in zip(sst, st_s):
            lst.append(val)

    return tuple([xp, xs] + [jnp.stack(v_, axis=0) for v_ in pst] + [jnp.stack(v_, axis=0) for v_ in sst])
```

```python
import functools
import math

import jax
import jax.numpy as jnp
import numpy as np
from jax import lax
from jax.experimental import pallas as pl
from jax.experimental.pallas import tpu as pltpu

D_MODEL = 1024
DEPTH = 2
N_BRANCH = 4
BR_WIDTH = D_MODEL // 4
FOX_HEADS = 4
FOX_HD = BR_WIDTH // FOX_HEADS
LRU_WIDTH = BR_WIDTH
LRU_BLOCKS = 4
LRU_C = 8.0
CONV_W = 4
DIFF_HEADS = 4
DIFF_VD = BR_WIDTH // DIFF_HEADS
DIFF_HD = DIFF_VD // 2
GDN_HEADS = 4
GDN_HD = BR_WIDTH // GDN_HEADS
MEM_HEADS = 4
MEM_HD = 64
MEM_WIDTH = MEM_HEADS * MEM_HD
D_FF = 4 * D_MODEL
ROPE_THETA = 10000.0
NORM_EPS = 1e-6
HEADS = 4
HEAD_DIM = 64
SMALL_COLS = 128
MAIN_COLS = 12 * BR_WIDTH + SMALL_COLS

F32 = jnp.float32
BF16 = jnp.bfloat16
NEG_BIG = -1e30
LANES = 128
VMEM_LIMIT_BYTES = 56 * 1024 * 1024


def _params(*sem):
    return pltpu.CompilerParams(dimension_semantics=sem, vmem_limit_bytes=VMEM_LIMIT_BYTES)


def _resident(shape):
    nd = len(shape)
    return pl.BlockSpec(shape, lambda *_: (0,) * nd, pipeline_mode=pl.Buffered(1))


def _dot(a, b):
    return jnp.dot(a, b, preferred_element_type=F32)


def _dot_nt(a, b):
    return lax.dot_general(a, b, (((1,), (1,)), ((), ())), preferred_element_type=F32)


def _dot_tn(a, b):
    return lax.dot_general(a, b, (((0,), (0,)), ((), ())), preferred_element_type=F32)


def _split3(x):
    p0 = x.astype(BF16)
    r = x - p0.astype(F32)
    p1 = r.astype(BF16)
    p2 = (r - p1.astype(F32)).astype(BF16)
    return p0, p1, p2


def _dot_sel(x, sel_bf16):
    p0, p1, p2 = _split3(x)
    return _dot(p0, sel_bf16) + (_dot(p1, sel_bf16) + _dot(p2, sel_bf16))


def _sel_dot(sel_bf16, x):
    p0, p1, p2 = _split3(x)
    return _dot(sel_bf16, p0) + (_dot(sel_bf16, p1) + _dot(sel_bf16, p2))


def _split2(x):
    hi = x.astype(BF16)
    return hi, (x - hi.astype(F32)).astype(BF16)


def _dot3(ah, al, bh, bl):
    return _dot(ah, bh) + (_dot(al, bh) + _dot(ah, bl))


def _group_id(shape, axis, size):
    return lax.shift_right_logical(lax.broadcasted_iota(jnp.int32, shape, axis), int(math.log2(size)))


def _rms_scale(x):
    return lax.rsqrt(jnp.mean(x * x, axis=-1, keepdims=True) + NORM_EPS)


def _sigmoid(x):
    return 1.0 / (1.0 + jnp.exp(-x))


def _norm_matmul_body(x_ref, g_ref, *refs, n_out, col_chunk, transposed):
    w_refs, o_refs = refs[:n_out], refs[n_out:]
    x = x_ref[...]
    h = (x * _rms_scale(x) * g_ref[...]).astype(BF16)
    for w_ref, o_ref in zip(w_refs, o_refs):
        n = w_ref.shape[0 if transposed else 1]
        for c0 in range(0, n, col_chunk):
            c1 = min(n, c0 + col_chunk)
            o_ref[:, c0:c1] = _dot_nt(h, w_ref[c0:c1, :]) if transposed else _dot(h, w_ref[:, c0:c1])


def _norm_matmul(x2, g, ws, *, row_tile, name, transposed=False):
    M, D = x2.shape
    widths = [w.shape[0 if transposed else 1] for w in ws]
    return pl.pallas_call(
        functools.partial(_norm_matmul_body, n_out=len(ws), col_chunk=1024, transposed=transposed),
        grid=(M // row_tile,),
        in_specs=[pl.BlockSpec((row_tile, D), lambda i: (i, 0)), _resident((1, D))] + [_resident(w.shape) for w in ws],
        out_specs=[pl.BlockSpec((row_tile, n), lambda i: (i, 0)) for n in widths],
        out_shape=[jax.ShapeDtypeStruct((M, n), F32) for n in widths],
        compiler_params=_params("arbitrary"),
        name=name,
    )(x2, g.reshape(1, D), *ws)


def _flash_t_body(*refs, n_maps, use_bias, tq, tk, group, sub_scale):
    refs = list(refs)
    qt_ref, k_ref, vt_ref = refs[:3]
    ck_ref = refs[3] if use_bias else None
    lam_ref, gsub_ref = (refs[-3], refs[-2]) if n_maps > 1 else (None, None)
    o_ref = refs[-1]
    i = pl.program_id(1)
    q0 = i * tq
    n_full = q0 // tk
    n_diag = tq // tk
    C = HEADS * HEAD_DIM
    width = HEAD_DIM // n_maps

    def step(pairs, qts, j, carries, masked):
        start = pl.multiple_of(j * tk, tk)
        kj = k_ref[pl.ds(start, tk), :]
        if masked:
            kpos = start + lax.broadcasted_iota(jnp.int32, (tk, tq), 0)
            qpos = q0 + lax.broadcasted_iota(jnp.int32, (tk, tq), 1)
            keep = kpos <= qpos
        if use_bias:
            ck = ck_ref[pl.ds(start, tk), :]
        sts = [_dot(kj, qt) for qt in qts]
        if use_bias:
            sts = [st - ck[:, h:h + 1] for st, (h, _) in zip(sts, pairs)]
        if masked:
            sts = [jnp.where(keep, st, NEG_BIG) for st in sts]
        m_news = [jnp.maximum(m, jnp.max(st, axis=0, keepdims=True)) for st, (m, _, _) in zip(sts, carries)]
        ps = [jnp.exp(st - m_new) for st, m_new in zip(sts, m_news)]
        alphas = [jnp.exp(m - m_new) for (m, _, _), m_new in zip(carries, m_news)]
        ls = [a * l + jnp.sum(p, axis=0, keepdims=True) for a, (_, l, _), p in zip(alphas, carries, ps)]
        pvs = [_dot(vt_ref[0, h * HEAD_DIM:(h + 1) * HEAD_DIM, pl.ds(start, tk)], p.astype(BF16))
               for (h, _), p in zip(pairs, ps)]
        accs = [a * acc + pv for a, (_, _, acc), pv in zip(alphas, carries, pvs)]
        return tuple(zip(m_news, ls, accs))

    all_pairs = [(h, mp) for h in range(HEADS) for mp in range(n_maps)]
    qt_all = qt_ref[0]
    row_chain = _group_id((C, tq), 0, width)
    normalized = {}
    for g0 in range(0, len(all_pairs), group):
        pairs = all_pairs[g0:g0 + group]
        qts = [jnp.where(row_chain == h * n_maps + mp, qt_all, jnp.zeros_like(qt_all)) for h, mp in pairs]
        init = tuple((jnp.full((1, tq), NEG_BIG, F32), jnp.zeros((1, tq), F32), jnp.zeros((HEAD_DIM, tq), F32))
                     for _ in pairs)
        carries = lax.fori_loop(0, n_full, lambda j, c: step(pairs, qts, j, c, False), init)
        for dj in range(n_diag):
            carries = step(pairs, qts, n_full + dj, carries, True)
        for pair, (_, l, acc) in zip(pairs, carries):
            normalized[pair] = acc / l
    outs = []
    for h in range(HEADS):
        if n_maps == 1:
            outs.append(normalized[(h, 0)])
        else:
            d = normalized[(h, 0)] - lam_ref[...] * normalized[(h, 1)]
            ms = jnp.mean(d * d, axis=0, keepdims=True)
            outs.append(d * lax.rsqrt(ms + NORM_EPS) * gsub_ref[...] * sub_scale)
    o_ref[...] = jnp.concatenate(outs, axis=0).T.astype(o_ref.dtype)


def _flash_attention_t(qt, k, vt, *, n_maps, tq, tk, group=4, ck=None, lam=None, gsub=None, sub_scale=1.0, name):
    B, C, T = qt.shape
    nq = T // tq
    operands = [qt, k, vt]
    in_specs = [pl.BlockSpec((1, C, tq), lambda b, i: (b, 0, i)),
                pl.BlockSpec((T, C), lambda b, i: (b, 0)),
                pl.BlockSpec((1, C, T), lambda b, i: (b, 0, 0))]
    if ck is not None:
        operands.append(ck)
        in_specs.append(pl.BlockSpec((T, LANES), lambda b, i: (b, 0)))
    if n_maps > 1:
        operands += [lam.reshape(1, 1), gsub.reshape(HEAD_DIM, 1)]
        in_specs += [_resident((1, 1)), _resident((HEAD_DIM, 1))]
    return pl.pallas_call(
        functools.partial(_flash_t_body, n_maps=n_maps, use_bias=ck is not None, tq=tq, tk=tk, group=group,
                          sub_scale=sub_scale),
        grid=(B, nq),
        in_specs=in_specs,
        out_specs=pl.BlockSpec((tq, C), lambda b, i: (b * nq + i, 0)),
        out_shape=jax.ShapeDtypeStruct((B * T, C), BF16),
        compiler_params=_params("arbitrary", "arbitrary"),
        name=name,
    )(*operands)


def _pad_lanes(x, width=LANES):
    return jnp.pad(x, [(0, 0)] * (x.ndim - 1) + [(0, width - x.shape[-1])])


Q_ROWS = 8


def _paged_body(pt_ref, *refs, n_pages, n_maps, use_bias, scale, sub_scale, chains, layer):
    refs = list(refs)
    q_ref = refs.pop(0)
    lam_ref, gsub_ref = (None, None) if use_bias else (refs.pop(0), refs.pop(0))
    knew_ref, vnew_ref = refs.pop(0), refs.pop(0)
    lfnew_ref = refs.pop(0) if use_bias else None
    caches = [refs.pop(0), refs.pop(0)] + ([refs.pop(0)] if use_bias else [])
    o_ref, m_ref, l_ref, acc_ref = refs[:4]
    refs = refs[4:]
    carry_ref = refs.pop(0) if use_bias else None
    bufs = [refs.pop(0) for _ in caches]
    sems = [refs.pop(0) for _ in caches]
    b = pl.program_id(0)
    g = pl.program_id(1)
    n_g = pl.num_programs(1)
    R = HEADS * n_maps * Q_ROWS
    C = HEADS * HEAD_DIM

    step = b * n_g + g
    slot = jnp.bitwise_and(step, 1)
    last_g = g + 1 == n_g
    nxt_b = jnp.where(last_g, b + 1, b)
    nxt_g = jnp.where(last_g, 0, g + 1)

    def page_copies(bb, gg, sl):
        out = []
        for i in range(n_pages):
            page = pt_ref[bb, gg * n_pages + i]
            for cache, buf, sem in zip(caches, bufs, sems):
                out.append(pltpu.make_async_copy(cache.at[layer, page], buf.at[sl, i], sem.at[sl]))
        return out

    @pl.when(step == 0)
    def _first():
        for cp in page_copies(b, g, slot):
            cp.start()

    @pl.when(step + 1 < pl.num_programs(0) * n_g)
    def _prefetch():
        for cp in page_copies(nxt_b, nxt_g, 1 - slot):
            cp.start()

    for cp in page_copies(b, g, slot):
        cp.wait()
    k_refs = [bufs[0].at[slot, i] for i in range(n_pages)]
    v_refs = [bufs[1].at[slot, i] for i in range(n_pages)]
    lf_refs = [bufs[2].at[slot, i] for i in range(n_pages)] if use_bias else None

    @pl.when(g == 0)
    def _init():
        m_ref[...] = jnp.full(m_ref.shape, NEG_BIG, F32)
        l_ref[...] = jnp.zeros(l_ref.shape, F32)
        acc_ref[...] = jnp.zeros(acc_ref.shape, F32)
        if use_bias:
            carry_ref[...] = jnp.zeros(carry_ref.shape, F32)

    q = q_ref[0]

    def head_rows(x):
        return jnp.concatenate([jnp.broadcast_to(x[h:h + 1], (Q_ROWS, x.shape[1])) for h in range(HEADS)], axis=0)

    def attend(kts, vts, lfs, causal_new, splits=1):
        n_all = len(kts)
        bias = None
        if use_bias:
            upper = jnp.where(lax.broadcasted_iota(jnp.int32, (LANES, LANES), 0)
                              <= lax.broadcasted_iota(jnp.int32, (LANES, LANES), 1), 1.0, 0.0).astype(BF16)
            rows = HEADS * Q_ROWS
            n = len(lfs)
            within = _dot_sel(jnp.concatenate([head_rows(lf) for lf in lfs], axis=0), upper)
            incl = jnp.broadcast_to(within[:, LANES - 1:LANES], within.shape)
            d = 1
            while d < n:
                incl = incl + jnp.concatenate([jnp.zeros((d * rows, LANES), F32), incl[:(n - d) * rows]], axis=0)
                d *= 2
            carry = carry_ref[...]
            before = jnp.concatenate([jnp.zeros((rows, LANES), F32), incl[:(n - 1) * rows]], axis=0) if n > 1 else 0.0
            c_all = within + before
            carry_ref[...] = carry + incl[(n - 1) * rows:]
            bias = [c_all[i * rows:(i + 1) * rows] + carry for i in range(n)]
        per = n_all // splits
        sls = [slice(c * per, (c + 1) * per) for c in range(splits)]
        ss = [_dot(q, jnp.concatenate(kts[sl], axis=1).astype(BF16)) * scale for sl in sls]
        if use_bias:
            ss = [s - jnp.concatenate(bias[sl], axis=1) for s, sl in zip(ss, sls)]
        if causal_new:
            t = jnp.bitwise_and(lax.broadcasted_iota(jnp.int32, ss[0].shape, 0), Q_ROWS - 1)
            ss = [jnp.where(lax.broadcasted_iota(jnp.int32, s.shape, 1) <= t, s, NEG_BIG) for s in ss]
        ms = [jnp.max(s, axis=1, keepdims=True) for s in ss]
        ps = [jnp.exp(s - m_c) for s, m_c in zip(ss, ms)]
        ls = [jnp.sum(p, axis=1, keepdims=True) for p in ps]
        pvs = [_dot_nt(p.astype(BF16), jnp.concatenate(vts[sl], axis=1).astype(BF16)) for p, sl in zip(ps, sls)]
        parts = list(zip(ms, ls, pvs))
        m_prev = m_ref[...]
        m_new = m_prev
        for m_c, _, _ in parts:
            m_new = jnp.maximum(m_new, m_c)
        alpha = jnp.exp(m_prev - m_new)
        l_new = alpha * l_ref[...]
        acc_new = alpha * acc_ref[...]
        for m_c, l_c, pv_c in parts:
            w_c = jnp.exp(m_c - m_new)
            l_new = l_new + w_c * l_c
            acc_new = acc_new + w_c * pv_c
        l_ref[...] = l_new
        acc_ref[...] = acc_new
        m_ref[...] = m_new

    attend([r[...] for r in k_refs], [r[...] for r in v_refs], [r[...] for r in lf_refs] if use_bias else None, False,
           splits=chains)

    @pl.when(g == pl.num_programs(1) - 1)
    def _finish():
        attend([knew_ref[0]], [vnew_ref[0]], [lfnew_ref[0]] if use_bias else None, True)
        full = acc_ref[...] / l_ref[...]
        lane_head = _group_id((Q_ROWS, C), 1, HEAD_DIM)
        outs = []
        for mp in range(n_maps):
            o = jnp.zeros((Q_ROWS, C), F32)
            for h in range(HEADS):
                r0 = (h * n_maps + mp) * Q_ROWS
                o = o + jnp.where(lane_head == h, full[r0:r0 + Q_ROWS], 0.0)
            outs.append(o)
        if n_maps == 1:
            res = outs[0]
        else:
            d = outs[0] - lam_ref[...] * outs[1]
            head_sum = jnp.where(_group_id((C, C), 0, HEAD_DIM) == _group_id((C, C), 1, HEAD_DIM), 1.0, 0.0).astype(BF16)
            ms = _dot_sel(d * d, head_sum) * (1.0 / HEAD_DIM)
            res = d * lax.rsqrt(ms + NORM_EPS) * gsub_ref[...] * sub_scale
        o_ref[0] = res.astype(o_ref.dtype)


def _paged_attention(q_rows, k_cache_t, v_cache_t, knew_t, vnew_t, page_table, layer, *, pages_per_step, n_maps, scale,
                     lf_cache_t=None, lfnew_t=None, lam=None, gsub=None, sub_scale=1.0, name):
    B, R, C = q_rows.shape
    n_pages_total = page_table.shape[1]
    P = pages_per_step
    use_bias = lf_cache_t is not None
    per_seq = lambda shape: pl.BlockSpec((1,) + shape, lambda b, g, pt: (b,) + (0,) * len(shape))
    const = lambda shape: pl.BlockSpec(shape, lambda b, g, pt: (0,) * len(shape))

    in_hbm = pl.BlockSpec(memory_space=pl.ANY)

    operands = [q_rows]
    in_specs = [per_seq((R, C))]
    if not use_bias:
        operands += [lam.reshape(1, 1), jnp.tile(gsub, HEADS).reshape(1, C)]
        in_specs += [const((1, 1)), const((1, C))]
    operands += [knew_t, vnew_t]
    in_specs += [per_seq((C, LANES)), per_seq((C, LANES))]
    if use_bias:
        operands.append(lfnew_t)
        in_specs.append(per_seq((HEADS, LANES)))
    operands += [k_cache_t, v_cache_t]
    in_specs += [in_hbm, in_hbm]
    scratch = [pltpu.VMEM((R, 1), F32), pltpu.VMEM((R, 1), F32), pltpu.VMEM((R, C), F32)]
    page_rows = [C, C]
    if use_bias:
        operands.append(lf_cache_t)
        in_specs.append(in_hbm)
        scratch.append(pltpu.VMEM((HEADS * Q_ROWS, LANES), F32))
        page_rows.append(HEADS)
    scratch += [pltpu.VMEM((2, P, r, LANES), F32) for r in page_rows]
    scratch += [pltpu.SemaphoreType.DMA((2,)) for _ in page_rows]
    return pl.pallas_call(
        functools.partial(_paged_body, n_pages=P, n_maps=n_maps, use_bias=use_bias, scale=scale, sub_scale=sub_scale,
                          chains=math.gcd(P, 4), layer=layer),
        grid_spec=pltpu.PrefetchScalarGridSpec(
            num_scalar_prefetch=1, grid=(B, n_pages_total // P), in_specs=in_specs,
            out_specs=pl.BlockSpec((1, Q_ROWS, C), lambda b, g, pt: (b, 0, 0)), scratch_shapes=scratch),
        out_shape=jax.ShapeDtypeStruct((B, Q_ROWS, C), F32),
        compiler_params=_params("arbitrary", "arbitrary"),
        name=name,
    )(page_table, *operands)


def _query_rows(q, n_maps):
    B, T, H, M, d = q.shape
    eye = jnp.eye(H * M, dtype=q.dtype).reshape(H, M, H, M)
    rows = jnp.einsum('bthmd,hmgn->bhmtgnd', q, eye)
    rows = jnp.pad(rows, ((0, 0), (0, 0), (0, 0), (0, Q_ROWS - T), (0, 0), (0, 0), (0, 0)))
    return rows.reshape(B, H * M * Q_ROWS, H * M * d).astype(BF16)


def _new_page(x):
    return jnp.pad(jnp.swapaxes(x, 1, 2), ((0, 0), (0, 0), (0, LANES - x.shape[1])))


def _gelu_tanh(x):
    return 0.5 * x * (1.0 + jnp.tanh(math.sqrt(2.0 / math.pi) * (x + 0.044715 * (x * x * x))))


def _lru_body(xc_ref, lg_ref, wa_ref, wx_ref, ba_ref, bx_ref, sp_ref, h0_ref, o_ref, ht_ref, carry_ref, *,
              rows, last_tile, last_row):
    t = pl.program_id(1)

    @pl.when(t == 0)
    def _init():
        carry_ref[...] = h0_ref[0]

    xc = xc_ref[...]
    xb = xc.astype(BF16)
    rg = _sigmoid(_dot(xb, wa_ref[...]) + ba_ref[...])
    ig = _sigmoid(_dot(xb, wx_ref[...]) + bx_ref[...])
    log_a = (-LRU_C) * rg * sp_ref[...]
    a = jnp.exp(log_a)
    b = jnp.sqrt(1.0 - jnp.exp(2.0 * log_a)) * (ig * xc)
    row = lax.broadcasted_iota(jnp.int32, a.shape, 0)
    d = 1
    while d < rows:
        a_prev = jnp.where(row >= d, pltpu.roll(a, d, 0), 1.0)
        b_prev = jnp.where(row >= d, pltpu.roll(b, d, 0), 0.0)
        b = a * b_prev + b
        a = a * a_prev
        d *= 2
    hs = b + a * carry_ref[...]
    carry_ref[...] = hs[rows - 1:rows]
    o_ref[...] = (hs * _gelu_tanh(lg_ref[...])).astype(o_ref.dtype)

    @pl.when(t == last_tile)
    def _state():
        ht_ref[0] = hs[last_row:last_row + 1]


def _lru(xc, lg, wa_bd, wx_bd, ba, bx, sp, h0, *, rows, t_valid, name):
    B, T, W = xc.shape
    nt = T // rows
    vec = lambda a: a.reshape(1, W)
    out, ht = pl.pallas_call(
        functools.partial(_lru_body, rows=rows, last_tile=(t_valid - 1) // rows, last_row=(t_valid - 1) % rows),
        grid=(B, nt),
        in_specs=[pl.BlockSpec((rows, W), lambda b, t: (b * nt + t, 0)),
                  pl.BlockSpec((rows, W), lambda b, t: (b * nt + t, 0)),
                  _resident((W, W)), _resident((W, W)), _resident((1, W)), _resident((1, W)), _resident((1, W)),
                  pl.BlockSpec((1, 1, W), lambda b, t: (b, 0, 0))],
        out_specs=[pl.BlockSpec((rows, W), lambda b, t: (b * nt + t, 0)),
                   pl.BlockSpec((1, 1, W), lambda b, t: (b, 0, 0))],
        out_shape=[jax.ShapeDtypeStruct((B * T, W), BF16), jax.ShapeDtypeStruct((B, 1, W), F32)],
        scratch_shapes=[pltpu.VMEM((1, W), F32)],
        compiler_params=_params("arbitrary", "arbitrary"),
        name=name,
    )(xc.reshape(B * T, W), lg.reshape(B * T, W), wa_bd, wx_bd, vec(ba), vec(bx), vec(sp), h0.reshape(B, 1, W))
    return out, ht.reshape(B, W)


def _gdn_chunk(qs, ks, vs, gbs, Ss, chunk, heads, head0s):
    each = lambda f, *ls: [f(*a) for a in zip(*ls)]
    R = heads * chunk
    SD = heads * HEAD_DIM
    r0 = lax.broadcasted_iota(jnp.int32, (R, R), 0)
    r1 = lax.broadcasted_iota(jnp.int32, (R, R), 1)
    same = _group_id((R, R), 0, chunk) == _group_id((R, R), 1, chunk)
    low = same & (r0 >= r1)
    strict = same & (r0 > r1)
    low_b = jnp.where(low, 1.0, 0.0).astype(BF16)
    same_b = jnp.where(same, 1.0, 0.0).astype(BF16)
    eye_b = jnp.where(r0 == r1, 1.0, 0.0).astype(BF16)

    colmat = each(lambda gb, h0: jnp.concatenate(
        [gb if h == 0 else pltpu.roll(gb, LANES - h, 1) for h in range(h0, h0 + heads)], axis=0), gbs, head0s)
    cparts = each(_split3, colmat)
    sel3 = lambda sel, p: _dot(sel, p[0]) + (_dot(sel, p[1]) + _dot(sel, p[2]))
    cum_cols = each(lambda p: sel3(low_b, p), cparts)
    tot_cols = each(lambda p: sel3(same_b, p), cparts)
    gcum_col = each(lambda c: c[:, 0:1], cum_cols)
    beta_col = each(lambda c: c[:, HEADS:HEADS + 1], colmat)
    gtot_col = each(lambda c: c[:, 0:1], tot_cols)
    cum_parts = each(_split3, cum_cols)
    gcum_row = each(lambda p: (_dot_tn(p[0], eye_b) + (_dot_tn(p[1], eye_b) + _dot_tn(p[2], eye_b)))[0:1, :], cum_parts)

    decay = each(lambda c, r: jnp.where(low, jnp.exp(jnp.where(low, c - r, 0.0)), 0.0), gcum_col, gcum_row)
    kb = each(lambda k: k.astype(BF16), ks)
    kk = each(lambda b: _dot_nt(b, b), kb)
    qk = each(lambda q, b: _dot_nt(q.astype(BF16), b), qs, kb)
    nmat = each(lambda a, d, b: jnp.where(strict, a * d * b, 0.0), kk, decay, beta_col)
    xs = each(lambda v, k, b, c: jnp.concatenate([v * b, k * (b * jnp.exp(c))], axis=1), vs, ks, beta_col, gcum_col)
    pw = each(_split2, nmat)
    xs = each(lambda x, p: x - _dot3(p[0], p[1], *_split2(x)), xs, pw)
    width = 2
    while width < chunk:
        pw = each(lambda p: _split2(_dot3(p[0], p[1], p[0], p[1])), pw)
        xs = each(lambda x, p: x + _dot3(p[0], p[1], *_split2(x)), xs, pw)
        width *= 2
    u = each(lambda x: x[:, 0:HEAD_DIM], xs)
    w = each(lambda x: pltpu.roll(x, HEAD_DIM, 1)[:, 0:HEAD_DIM], xs)

    head_match = _group_id((R, SD), 0, chunk) == _group_id((R, SD), 1, HEAD_DIM)

    def spread(x):
        return jnp.where(head_match, jnp.concatenate([x] * heads, axis=1), 0.0).astype(BF16)

    sb = each(lambda S: S.astype(BF16), Ss)
    v_new = each(lambda u_, w_, s: u_ - _dot(spread(w_), s), u, w, sb)
    vb = each(lambda x: x.astype(BF16), v_new)
    o = each(lambda q, c, s, a, d, x: _dot(spread(q * jnp.exp(c)), s) + _dot((a * d).astype(BF16), x),
             qs, gcum_col, sb, qk, decay, vb)
    kg = each(lambda k, t, c: spread(k * jnp.exp(t - c)), ks, gtot_col, gcum_col)
    tok = jnp.bitwise_and(lax.broadcasted_iota(jnp.int32, (R, SD), 0), chunk - 1)
    first = jnp.where(head_match & (tok == 0), 1.0, 0.0).astype(BF16)
    gtot_state = each(lambda p: _dot_tn(first, p[0]) + (_dot_tn(first, p[1]) + _dot_tn(first, p[2])),
                      each(_split3, tot_cols))
    s_new = each(lambda S, t, g, x: S * jnp.exp(t[:, 0:1]) + _dot_tn(g, x), Ss, gtot_state, kg, vb)
    return o, s_new


def _gdn_body(q_ref, k_ref, v_ref, gb_ref, s0_ref, o_ref, sT_ref, s_ref, *, chunk, group):
    c = pl.program_id(1)

    @pl.when(c == 0)
    def _init():
        s_ref[...] = s0_ref[...]

    hpg = min(HEADS, LANES // chunk)
    R = hpg * chunk
    probs = [(g, h0) for g in range(group) for h0 in range(0, HEADS, hpg)]
    hsl = lambda h0: slice(h0, h0 + hpg)
    rsl = lambda h0: slice(h0 * HEAD_DIM, (h0 + hpg) * HEAD_DIM)
    o, s_new = _gdn_chunk([q_ref[g, hsl(h0)].reshape(R, HEAD_DIM) for g, h0 in probs],
                          [k_ref[g, hsl(h0)].reshape(R, HEAD_DIM) for g, h0 in probs],
                          [v_ref[g, hsl(h0)].reshape(R, HEAD_DIM) for g, h0 in probs],
                          [gb_ref[g] for g, _ in probs], [s_ref[g, rsl(h0)] for g, h0 in probs], chunk, hpg,
                          [h0 for _, h0 in probs])
    for i, (g, h0) in enumerate(probs):
        o_ref[g, hsl(h0)] = o[i].reshape(hpg, chunk, HEAD_DIM)
        s_ref[g, rsl(h0)] = s_new[i]

    @pl.when(c == pl.num_programs(1) - 1)
    def _state():
        sT_ref[...] = s_ref[...]


def _gdn(q, k, v, gb, S0, *, chunk, group, name):
    B, H, T, hd = q.shape
    nc = T // chunk
    SD = H * hd
    tok = pl.BlockSpec((group, H, chunk, hd), lambda b, c: (b, 0, c, 0))
    st = pl.BlockSpec((group, SD, hd), lambda b, c: (b, 0, 0))
    return pl.pallas_call(
        functools.partial(_gdn_body, chunk=chunk, group=group),
        grid=(B // group, nc),
        in_specs=[tok, tok, tok, pl.BlockSpec((group, chunk, LANES), lambda b, c: (b, c, 0)), st],
        out_specs=[tok, st],
        out_shape=[jax.ShapeDtypeStruct((B, H, T, hd), F32), jax.ShapeDtypeStruct((B, SD, hd), F32)],
        scratch_shapes=[pltpu.VMEM((group, SD, hd), F32)],
        compiler_params=_params("arbitrary", "arbitrary"),
        name=name,
    )(q, k, v, gb, S0)


def _log1p_exp_neg_abs(x):
    return jnp.log(1.0 + jnp.exp(-jnp.abs(x)))


def _conv_tap_sum(x, tail, w):
    rows = x.shape[0]
    first = lax.broadcasted_iota(jnp.int32, (8, x.shape[1]), 0)
    y = x * w[CONV_W - 1:CONV_W]
    for back in range(1, CONV_W):
        rolled = pltpu.roll(x, back, 0)
        top = jnp.where(first < back, pltpu.roll(tail, back, 0), rolled[0:8])
        y = y + jnp.concatenate([top, rolled[8:rows]], axis=0) * w[CONV_W - 1 - back:CONV_W - back]
    return y


def _prep_body(fq_ref, fk_ref, fv_ref, lx_ref, dq_ref, dk_ref, dv_ref, gq_ref, gk_ref, gv_ref, sm_ref,
               fqg_ref, fkg_ref, fb_ref, dqg_ref, dkg_ref, cos_ref, sin_ref, gw_ref, lw_ref, lb_ref, ga_ref, gdt_ref,
               fqt_o, fk_o, fvt_o, fkT_o, fvT_o, flT_o, ck_o, dqt_o, dk_o, dvt_o, dkT_o, dvT_o,
               gq_o, gk_o, gv_o, gb_o, xc_o, lst_o, gst_o, ltail, gtail, ccarry, *, rows):
    t = pl.program_id(1)
    C = HEADS * HEAD_DIM

    @pl.when(t == 0)
    def _init():
        ltail[...] = jnp.zeros(ltail.shape, F32)
        gtail[...] = jnp.zeros(gtail.shape, F32)
        ccarry[...] = jnp.zeros(ccarry.shape, F32)

    def group_sum(width):
        return jnp.where(_group_id((C, C), 0, width) == _group_id((C, C), 1, width), 1.0, 0.0).astype(BF16)

    sum64, sum32 = group_sum(HEAD_DIM), group_sum(DIFF_HD)

    def group_rms(x, sel, width):
        return x * lax.rsqrt(_dot_sel(x * x, sel) * (1.0 / width) + NORM_EPS)

    kn = group_rms(fk_ref[...], sum64, FOX_HD) * fkg_ref[...]
    qn = group_rms(fq_ref[...], sum64, FOX_HD) * fqg_ref[...] * (1.0 / math.sqrt(FOX_HD))
    fv = fv_ref[...]
    knT, fvT = kn.T, fv.T
    fqt_o[0] = qn.T.astype(BF16)
    fk_o[...] = kn.astype(BF16)
    fvt_o[0] = fvT.astype(BF16)
    fkT_o[0] = knT
    fvT_o[0] = fvT
    sm = sm_ref[...]
    xf = sm + fb_ref[...]
    logf = jnp.minimum(xf, 0.0) - _log1p_exp_neg_abs(xf)
    flT_o[0] = logf.T[0:8]
    lower = jnp.where(lax.broadcasted_iota(jnp.int32, (rows, rows), 0) >= lax.broadcasted_iota(jnp.int32, (rows, rows), 1),
                      1.0, 0.0).astype(BF16)
    c = _sel_dot(lower, logf) + ccarry[...]
    ccarry[...] = c[rows - 1:rows]
    ck_o[...] = c

    lane = lax.broadcasted_iota(jnp.int32, (rows, LANES), 1)
    first_half = jnp.bitwise_and(lane, DIFF_HD - 1) < DIFF_HD // 2

    def swap_halves(x):
        parts = []
        for c0 in range(0, C, LANES):
            h = x[:, c0:c0 + LANES]
            parts.append(jnp.where(first_half, pltpu.roll(h, LANES - DIFF_HD // 2, 1), pltpu.roll(h, DIFF_HD // 2, 1)))
        return jnp.concatenate(parts, axis=1)

    def rotary(x):
        return x * cos_ref[...] + swap_halves(x) * sin_ref[...]

    dkr = rotary(group_rms(dk_ref[...], sum32, DIFF_HD) * dkg_ref[...])
    dqr = rotary(group_rms(dq_ref[...], sum32, DIFF_HD) * dqg_ref[...]) * (1.0 / math.sqrt(DIFF_HD))
    dv = dv_ref[...]
    dkrT, dvT = dkr.T, dv.T
    dqt_o[0] = dqr.T.astype(BF16)
    dk_o[...] = dkr.astype(BF16)
    dvt_o[0] = dvT.astype(BF16)
    dkT_o[0] = dkrT
    dvT_o[0] = dvT

    gw = gw_ref[...]
    raws = (gq_ref[...], gk_ref[...], gv_ref[...])
    acts = []
    for i, raw in enumerate(raws):
        y = _conv_tap_sum(raw, gtail[:, i * C:(i + 1) * C], gw[:, i * C:(i + 1) * C])
        acts.append(y * _sigmoid(y))
    gqn = acts[0] * lax.rsqrt(_dot_sel(acts[0] * acts[0], sum64) + NORM_EPS) * (GDN_HD ** -0.5)
    gkn = acts[1] * lax.rsqrt(_dot_sel(acts[1] * acts[1], sum64) + NORM_EPS)
    pick_r = lax.broadcasted_iota(jnp.int32, (C, HEAD_DIM), 0)
    pick_c = lax.broadcasted_iota(jnp.int32, (C, HEAD_DIM), 1)
    for h in range(HEADS):
        pick = jnp.where(pick_r == pick_c + h * HEAD_DIM, 1.0, 0.0).astype(BF16)
        gq_o[0, h] = _dot_sel(gqn, pick)
        gk_o[0, h] = _dot_sel(gkn, pick)
        gv_o[0, h] = _dot_sel(acts[2], pick)
    sh = pltpu.roll(sm, LANES - HEADS, 1)
    xa = sh + gdt_ref[...]
    decay = -ga_ref[...] * (jnp.maximum(xa, 0.0) + _log1p_exp_neg_abs(xa))
    gb_o[...] = jnp.where(lane < HEADS, decay, _sigmoid(sh))

    lx = lx_ref[...]
    xc_o[...] = _conv_tap_sum(lx, ltail[...], lw_ref[...]) + lb_ref[...]

    ltail[...] = lx[rows - 8:rows]
    gtail[...] = jnp.concatenate([r[rows - 8:rows] for r in raws], axis=1)
    lst_o[0] = ltail[...]
    gst_o[0] = gtail[...]


def _prep(zm2, B, T, p, cos_t, sin_t, *, rows, name):
    M = B * T
    nt = T // rows
    C = HEADS * HEAD_DIM
    colblk = lambda j, w=C: pl.BlockSpec((rows, w), lambda b, t: (b * nt + t, j))
    vec = lambda a, w=C: jnp.pad(a.reshape(1, -1), ((0, 0), (0, w - a.size)))
    tile4 = lambda g: jnp.tile(g, C // g.size).reshape(1, C)
    in_specs = ([colblk(j) for j in (0, 1, 2, 3, 5, 6, 7, 8, 9, 10)] + [colblk(_C_SMALL // LANES, LANES)]
                + [_resident((1, C))] * 2 + [_resident((1, LANES))] + [_resident((1, C))] * 2
                + [pl.BlockSpec((rows, C), lambda b, t: (t, 0))] * 2
                + [_resident((CONV_W, 3 * C)), _resident((CONV_W, C)), _resident((1, C)), _resident((1, LANES)),
                   _resident((1, LANES))])
    chan = lambda dt: (jax.ShapeDtypeStruct((B, C, T), dt), pl.BlockSpec((1, C, rows), lambda b, t: (b, 0, t)))
    tokm = lambda w, dt: (jax.ShapeDtypeStruct((M, w), dt), pl.BlockSpec((rows, w), lambda b, t: (b * nt + t, 0)))
    headm = (jax.ShapeDtypeStruct((B, HEADS, T, HEAD_DIM), F32),
             pl.BlockSpec((1, HEADS, rows, HEAD_DIM), lambda b, t: (b, 0, t, 0)))
    state = lambda w: (jax.ShapeDtypeStruct((B, 8, w), F32), pl.BlockSpec((1, 8, w), lambda b, t: (b, 0, 0)))
    outs = [chan(BF16), tokm(C, BF16), chan(BF16), chan(F32), chan(F32),
            (jax.ShapeDtypeStruct((B, 8, T), F32), pl.BlockSpec((1, 8, rows), lambda b, t: (b, 0, t))), tokm(LANES, F32),
            chan(BF16), tokm(C, BF16), chan(BF16), chan(F32), chan(F32),
            headm, headm, headm, tokm(LANES, F32), tokm(C, F32), state(C), state(3 * C)]
    names = ['fqt', 'fk', 'fvt', 'fkT', 'fvT', 'flT', 'ck', 'dqt', 'dk', 'dvt', 'dkT', 'dvT', 'gq', 'gk', 'gv', 'gb', 'xc',
             'lru_tail', 'gdn_tail']
    res = pl.pallas_call(
        functools.partial(_prep_body, rows=rows),
        grid=(B, nt),
        in_specs=in_specs,
        out_specs=[o[1] for o in outs],
        out_shape=[o[0] for o in outs],
        scratch_shapes=[pltpu.VMEM((8, C), F32), pltpu.VMEM((8, 3 * C), F32), pltpu.VMEM((1, LANES), F32)],
        compiler_params=_params("arbitrary", "arbitrary"),
        name=name,
    )(*([zm2] * 11), tile4(p['fox_q_norm']), tile4(p['fox_k_norm']), vec(p['fox_b_f'], LANES),
      tile4(p['diff_q_norm']), tile4(p['diff_k_norm']), cos_t, sin_t, p['gdn_conv_w'], p['lru_conv_w'],
      p['lru_conv_b'].reshape(1, C), vec(jnp.exp(p['gdn_A_log']), LANES), vec(p['gdn_dt_bias'], LANES))
    return dict(zip(names, res))


def _rope_tables(pos):
    inv = ROPE_THETA ** (-jnp.arange(0, DIFF_HD, 2, dtype=F32) / DIFF_HD)
    ang = pos.astype(F32)[:, None] * inv[None, :]
    cos = jnp.tile(jnp.cos(ang), (1, 2 * BR_WIDTH // DIFF_HD))
    sin = jnp.tile(jnp.concatenate([-jnp.sin(ang), jnp.sin(ang)], axis=1), (1, BR_WIDTH // DIFF_HD))
    return cos, sin


def _merge_body(x_ref, b0_ref, b1_ref, b2_ref, b3_ref, gates_ref, wb_ref, wo_ref, o_ref):
    projs = [_dot(b_ref[...], wb_ref[n]) for n, b_ref in enumerate((b0_ref, b1_ref, b2_ref, b3_ref))]
    merged = None
    for n, proj in enumerate(projs):
        term = _sigmoid(gates_ref[:, n * D_MODEL:(n + 1) * D_MODEL]) * proj
        merged = term if merged is None else merged + term
    o_ref[...] = x_ref[...] + _dot(merged.astype(BF16), wo_ref[...])


def _merge(x2, branches, gates, wb, wo, *, row_tile, name):
    M, D = x2.shape
    row = lambda w: pl.BlockSpec((row_tile, w), lambda i: (i, 0))
    return pl.pallas_call(
        _merge_body,
        grid=(M // row_tile,),
        in_specs=[row(D)] + [row(BR_WIDTH)] * N_BRANCH + [row(N_BRANCH * D), _resident(wb.shape), _resident(wo.shape)],
        out_specs=row(D),
        out_shape=jax.ShapeDtypeStruct((M, D), F32),
        compiler_params=_params("arbitrary"),
        name=name,
    )(x2, *branches, gates, wb, wo)


def _mem_body(x_ref, g_ref, wq_ref, qg_ref, mk_ref, mv_ref, wo_ref, o_ref):
    x = x_ref[0]
    h = (x * _rms_scale(x) * g_ref[...]).astype(BF16)
    q = _dot(h, wq_ref[...])
    W = q.shape[1]
    head_sum = jnp.where(_group_id((W, W), 0, MEM_HD) == _group_id((W, W), 1, MEM_HD), 1.0, 0.0).astype(BF16)
    ms = _dot_sel(q * q, head_sum) * (1.0 / MEM_HD)
    qn = q * lax.rsqrt(ms + NORM_EPS) * qg_ref[...] * (1.0 / math.sqrt(MEM_HD))
    mk = mk_ref[0].astype(BF16)
    mv = mv_ref[0].astype(BF16)
    lane_head = _group_id(q.shape, 1, MEM_HD)
    heads = range(MEM_HEADS)
    ss = [_dot_nt(jnp.where(lane_head == hh, qn, 0.0).astype(BF16), mk) for hh in heads]
    ps = [jnp.exp(s - jnp.max(s, axis=1, keepdims=True)) for s in ss]
    ps = [p / jnp.sum(p, axis=1, keepdims=True) for p in ps]
    pvs = [_dot(p.astype(BF16), mv) for p in ps]
    o = jnp.zeros(q.shape, F32)
    for hh in heads:
        o = o + jnp.where(lane_head == hh, pvs[hh], 0.0)
    o_ref[0] = x + _dot(o.astype(BF16), wo_ref[...])


def _mem_sublayer(x, g, wq, qg, mk, mv, wo, *, row_tile, name):
    B, T, D = x.shape
    nt = T // row_tile
    kv = pl.BlockSpec((1,) + mk.shape[1:], lambda b, t: (b, 0, 0))
    return pl.pallas_call(
        _mem_body,
        grid=(B, nt),
        in_specs=[pl.BlockSpec((1, row_tile, D), lambda b, t: (b, t, 0)), _resident((1, D)), _resident(wq.shape),
                  _resident((1, MEM_WIDTH)), kv, kv, _resident(wo.shape)],
        out_specs=pl.BlockSpec((1, row_tile, D), lambda b, t: (b, t, 0)),
        out_shape=jax.ShapeDtypeStruct((B, T, D), F32),
        compiler_params=_params("arbitrary", "arbitrary"),
        name=name,
    )(x, g.reshape(1, D), wq, jnp.tile(qg, MEM_HEADS).reshape(1, MEM_WIDTH), mk, mv, wo)


def _mlp_body(x_ref, g_ref, wu_ref, wd_ref, o_ref, *, ff_chunk):
    x = x_ref[...]
    h = (x * _rms_scale(x) * g_ref[...]).astype(BF16)
    acc = x
    for c in range(D_FF // ff_chunk):
        u = _dot(h, wu_ref[:, c * ff_chunk:(c + 1) * ff_chunk])
        u = jnp.square(jnp.maximum(u, 0.0)).astype(BF16)
        acc = acc + _dot(u, wd_ref[c * ff_chunk:(c + 1) * ff_chunk, :])
    o_ref[...] = acc


def _mlp_sublayer(x2, g, w_up, w_down, *, row_tile, name):
    M, D = x2.shape
    return pl.pallas_call(
        functools.partial(_mlp_body, ff_chunk=1024),
        grid=(M // row_tile,),
        in_specs=[pl.BlockSpec((row_tile, D), lambda i: (i, 0)), _resident((1, D)), _resident(w_up.shape),
                  _resident(w_down.shape)],
        out_specs=pl.BlockSpec((row_tile, D), lambda i: (i, 0)),
        out_shape=jax.ShapeDtypeStruct((M, D), F32),
        compiler_params=_params("arbitrary"),
        name=name,
    )(x2, g.reshape(1, D), w_up, w_down)


def _rmsnorm(x, g):
    xf = x.astype(F32)
    return xf * lax.rsqrt(jnp.mean(xf * xf, axis=-1, keepdims=True) + NORM_EPS) * g.astype(F32)


def _l2norm(x):
    return x * lax.rsqrt(jnp.sum(x * x, axis=-1, keepdims=True) + NORM_EPS)


def _rope(x, pos):
    d = x.shape[-1]
    inv = ROPE_THETA ** (-jnp.arange(0, d, 2, dtype=F32) / d)
    ang = pos.astype(F32)[:, None] * inv[None, :]
    ang = ang.reshape((1, ang.shape[0]) + (1,) * (x.ndim - 3) + (d // 2,))
    cos, sin = jnp.cos(ang), jnp.sin(ang)
    x1, x2 = x[..., : d // 2], x[..., d // 2:]
    return jnp.concatenate([x1 * cos - x2 * sin, x2 * cos + x1 * sin], axis=-1)


def _causal_conv(x, buf, w, b=None):
    xp = jnp.concatenate([buf.astype(x.dtype), x], axis=1)
    T = x.shape[1]
    y = xp[:, 0:T] * w[0]
    for i in range(1, CONV_W):
        y = y + xp[:, i:i + T] * w[i]
    if b is not None:
        y = y + b
    return y, xp[:, -(CONV_W - 1):]


def _key_minor_pages(cache):
    nd = cache.ndim
    t = jnp.transpose(cache, (0, 1) + tuple(range(3, nd)) + (2,))
    return t.reshape(cache.shape[:2] + (-1, cache.shape[2]))


def _head_major(x):
    return jnp.swapaxes(x, 1, 2)


def _pad_time(x, t_pad, axis=1):
    pad = [(0, 0)] * x.ndim
    pad[axis] = (0, t_pad - x.shape[axis])
    return jnp.pad(x, pad)


def _pack_layer_weights(p):
    o = np.cumsum((0,) + (3 * BR_WIDTH, FOX_HEADS, LRU_WIDTH, LRU_WIDTH, 3 * BR_WIDTH, 3 * BR_WIDTH, GDN_HEADS,
                          GDN_HEADS, BR_WIDTH, N_BRANCH * D_MODEL)).tolist()
    wt = p['w_in'].T
    seg = lambda i: wt[o[i]:o[i + 1]]
    small = jnp.concatenate([seg(1), seg(6), seg(7), jnp.zeros((SMALL_COLS - 3 * HEADS, D_MODEL), wt.dtype)], axis=0)
    w_main = jnp.concatenate([seg(0), seg(2), seg(3), seg(4), seg(5), seg(8), small], axis=0).astype(BF16)
    blockdiag = lambda wb: jax.scipy.linalg.block_diag(*[wb[i] for i in range(LRU_BLOCKS)]).astype(BF16)
    return dict(w_main=w_main, w_gates=seg(9).astype(BF16), lru_wa=blockdiag(p['lru_w_a']),
                lru_wx=blockdiag(p['lru_w_x']), w_branch=p['w_branch'].astype(BF16), w_out=p['w_out'].astype(BF16),
                w_mem_q=p['w_mem_q'].astype(BF16), w_mem_kv=p['w_mem_kv'].astype(BF16),
                w_mem_o=p['w_mem_o'].astype(BF16), w_mlp_up=p['w_mlp_up'].astype(BF16),
                w_mlp_down=p['w_mlp_down'].astype(BF16))


_C_FOX, _C_LRUX, _C_LRUG, _C_DIFF, _C_GDN, _C_GDNZ, _C_SMALL = 0, 768, 1024, 1280, 2048, 2816, 3072


def _gdn_gate(go, z, g):
    B, T = go.shape[:2]
    out = _rmsnorm(go, g) * jax.nn.silu(z.reshape(B, T, GDN_HEADS, GDN_HD))
    return out.reshape(B * T, BR_WIDTH).astype(BF16)


def _mixer_sublayer_prompt(x, p, pw, lam_init, rope_tables, *, row_tile, attn_tile, attn_chunk, prep_rows, tag):
    B, T, D = x.shape
    M = B * T
    x2 = x.reshape(M, D)
    zm, gates = _norm_matmul(x2, p['norm_mix'], [pw['w_main'], pw['w_gates']], row_tile=row_tile, name=tag + "in_proj",
                             transposed=True)
    pre = _prep(zm, B, T, p, *rope_tables, rows=prep_rows, name=tag + "prep")
    fox_out = _flash_attention_t(pre['fqt'], pre['fk'], pre['fvt'], n_maps=1, tq=attn_tile, tk=attn_chunk, ck=pre['ck'],
                                 name=tag + "fox_attn")
    lp = p['diff_lambda']
    lam = jnp.exp(jnp.sum(lp[0] * lp[1])) - jnp.exp(jnp.sum(lp[2] * lp[3])) + lam_init
    diff_out = _flash_attention_t(pre['dqt'], pre['dk'], pre['dvt'], n_maps=2, tq=attn_tile, tk=attn_chunk, lam=lam,
                                  gsub=p['diff_sub_norm'], sub_scale=1.0 - lam_init, name=tag + "diff_attn")
    zm3 = zm.reshape(B, T, MAIN_COLS)
    lru_out, lru_hT = _lru(pre['xc'].reshape(B, T, LRU_WIDTH), zm3[..., _C_LRUG:_C_LRUG + LRU_WIDTH], pw['lru_wa'],
                           pw['lru_wx'], p['lru_b_a'], p['lru_b_x'], jax.nn.softplus(-p['lru_lambda']),
                           jnp.zeros((B, LRU_WIDTH), F32), rows=256, t_valid=T, name=tag + "lru")
    go, gS = _gdn(pre['gq'], pre['gk'], pre['gv'], pre['gb'].reshape(B, T, LANES),
                  jnp.zeros((B, GDN_HEADS * GDN_HD, GDN_HD), F32), chunk=64, group=math.gcd(B, 8), name=tag + "gdn")
    gdn_out = _gdn_gate(_head_major(go), zm3[..., _C_GDNZ:_C_GDNZ + BR_WIDTH], p['gdn_out_norm'])
    x2 = _merge(x2, (fox_out, lru_out, diff_out, gdn_out), gates, pw['w_branch'], pw['w_out'], row_tile=row_tile,
                name=tag + "merge")
    token_major = lambda a: jnp.swapaxes(a, 1, 2)
    state = (token_major(pre['fkT']).reshape(B, T, FOX_HEADS, FOX_HD),
             token_major(pre['fvT']).reshape(B, T, FOX_HEADS, FOX_HD),
             token_major(pre['flT'][:, :FOX_HEADS]),
             token_major(pre['dkT']).reshape(B, T, DIFF_HEADS, 2, DIFF_HD),
             token_major(pre['dvT']).reshape(B, T, DIFF_HEADS, DIFF_VD),
             lru_hT, pre['lru_tail'][:, 8 - (CONV_W - 1):], gS.reshape(B, GDN_HEADS, GDN_HD, GDN_HD),
             pre['gdn_tail'][:, 8 - (CONV_W - 1):])
    return x2.reshape(B, T, D), state


def _mixer_sublayer(x, pos, p, pw, lam_init, fox_past, diff_past, lru_h0, lru_buf, gdn_S0, gdn_buf, *, row_tile, tag):
    B, T, D = x.shape
    M = B * T
    x2 = x.reshape(M, D)
    zm, gates = _norm_matmul(x2, p['norm_mix'], [pw['w_main'], pw['w_gates']], row_tile=row_tile, name=tag + "in_proj",
                             transposed=True)
    zm = zm.reshape(B, T, MAIN_COLS)
    seg = lambda c0, w: zm[..., c0:c0 + w]

    r = seg(_C_FOX, 3 * BR_WIDTH).reshape(B, T, 3, FOX_HEADS, FOX_HD)
    fq = _rmsnorm(r[:, :, 0], p['fox_q_norm'])
    fk = _rmsnorm(r[:, :, 1], p['fox_k_norm'])
    fv = r[:, :, 2]
    flogf = jax.nn.log_sigmoid(seg(_C_SMALL, FOX_HEADS) + p['fox_b_f'])
    fo = _paged_attention(_query_rows(fq[:, :, :, None], 1), fox_past['k'], fox_past['v'],
                          _new_page(fk.reshape(B, T, BR_WIDTH)), _new_page(fv.reshape(B, T, BR_WIDTH)),
                          fox_past['page_table'], fox_past['layer'], pages_per_step=fox_past['pages_per_step'],
                          n_maps=1, scale=1.0 / math.sqrt(FOX_HD), lf_cache_t=fox_past['logf'],
                          lfnew_t=_new_page(flogf), name=tag + "fox_paged")
    fox_out = fo[:, :T].reshape(M, BR_WIDTH).astype(BF16)

    xc, lru_buf_new = _causal_conv(seg(_C_LRUX, LRU_WIDTH), lru_buf, p['lru_conv_w'], p['lru_conv_b'])
    lru_rows = min(256, T) if T % 8 == 0 else 8
    t_pad = -(-T // lru_rows) * lru_rows
    lru_out, lru_hT = _lru(_pad_time(xc, t_pad), _pad_time(seg(_C_LRUG, LRU_WIDTH), t_pad), pw['lru_wa'], pw['lru_wx'],
                           p['lru_b_a'], p['lru_b_x'], jax.nn.softplus(-p['lru_lambda']), lru_h0,
                           rows=lru_rows, t_valid=T, name=tag + "lru")
    if t_pad != T:
        lru_out = lru_out.reshape(B, t_pad, LRU_WIDTH)[:, :T].reshape(M, LRU_WIDTH)

    r = seg(_C_DIFF, 3 * BR_WIDTH).reshape(B, T, 3, DIFF_HEADS, DIFF_VD)
    dq = _rope(_rmsnorm(r[:, :, 0].reshape(B, T, DIFF_HEADS, 2, DIFF_HD), p['diff_q_norm']), pos)
    dk = _rope(_rmsnorm(r[:, :, 1].reshape(B, T, DIFF_HEADS, 2, DIFF_HD), p['diff_k_norm']), pos)
    dvv = r[:, :, 2]
    lp = p['diff_lambda']
    lam = jnp.exp(jnp.sum(lp[0] * lp[1])) - jnp.exp(jnp.sum(lp[2] * lp[3])) + lam_init
    do = _paged_attention(_query_rows(dq, 2), diff_past['k'], diff_past['v'],
                          _new_page(dk.reshape(B, T, BR_WIDTH)), _new_page(dvv.reshape(B, T, BR_WIDTH)),
                          diff_past['page_table'], diff_past['layer'], pages_per_step=diff_past['pages_per_step'],
                          n_maps=2, scale=1.0 / math.sqrt(DIFF_HD), lam=lam, gsub=p['diff_sub_norm'],
                          sub_scale=1.0 - lam_init, name=tag + "diff_paged")
    diff_out = do[:, :T].reshape(M, BR_WIDTH).astype(BF16)

    gc, gdn_buf_new = _causal_conv(seg(_C_GDN, 3 * BR_WIDTH), gdn_buf, p['gdn_conv_w'])
    gc = jax.nn.silu(gc).reshape(B, T, 3, GDN_HEADS, GDN_HD)
    gq = _l2norm(gc[:, :, 0]) * (GDN_HD ** -0.5)
    gk = _l2norm(gc[:, :, 1])
    gvv = gc[:, :, 2]
    gbeta = jax.nn.sigmoid(seg(_C_SMALL + 2 * HEADS, GDN_HEADS))
    gg = -jnp.exp(p['gdn_A_log']) * jax.nn.softplus(seg(_C_SMALL + HEADS, GDN_HEADS) + p['gdn_dt_bias'])
    chunk = 64 if T % 64 == 0 else 32
    tg = -(-T // chunk) * chunk
    gb = _pad_lanes(_pad_time(jnp.concatenate([gg, gbeta], axis=-1), tg))
    hm = lambda a: _head_major(_pad_time(a, tg))
    go, gS = _gdn(hm(gq), hm(gk), hm(gvv), gb, gdn_S0.reshape(B, GDN_HEADS * GDN_HD, GDN_HD), chunk=chunk,
                  group=math.gcd(B, 4), name=tag + "gdn")
    gdn_out = _gdn_gate(_head_major(go)[:, :T], seg(_C_GDNZ, BR_WIDTH), p['gdn_out_norm'])

    x2 = _merge(x2, (fox_out, lru_out, diff_out, gdn_out), gates, pw['w_branch'], pw['w_out'], row_tile=row_tile,
                name=tag + "merge")
    state = (fk, fv, flogf, dk, dvv, lru_hT, lru_buf_new, gS.reshape(B, GDN_HEADS, GDN_HD, GDN_HD), gdn_buf_new)
    return x2.reshape(B, T, D), state


def _memory_kv(mem, p, pw):
    B, Tm, D = mem.shape
    kv, = _norm_matmul(mem.reshape(B * Tm, D), p['norm_mem_src'], [pw['w_mem_kv']], row_tile=256, name="mem_kv")
    kv = kv.reshape(B, Tm, 2, MEM_HEADS, MEM_HD)
    return _rmsnorm(kv[:, :, 0], p['mem_k_norm']), kv[:, :, 1]


def _memory_sublayer(x, mk, mv, p, pw, *, row_tile, name):
    B, T, D = x.shape
    t_pad = -(-T // row_tile) * row_tile
    flat = lambda a: a.reshape(a.shape[0], a.shape[1], MEM_WIDTH)
    out = _mem_sublayer(_pad_time(x, t_pad), p['norm_mem'], pw['w_mem_q'], p['mem_q_norm'], flat(mk), flat(mv),
                        pw['w_mem_o'], row_tile=row_tile, name=name)
    return out[:, :T]


def kernel(x_prompt, x_sample, cache_fox_k, cache_fox_v, cache_fox_logf, cache_diff_k, cache_diff_v, cache_mem_k, cache_mem_v, state_lru_h, state_lru_conv, state_gdn_S, state_gdn_conv, page_table, mem_prompt, norm_mix, w_in, fox_b_f, fox_q_norm, fox_k_norm, lru_conv_w, lru_conv_b, lru_w_a, lru_b_a, lru_w_x, lru_b_x, lru_lambda, diff_q_norm, diff_k_norm, diff_lambda, diff_sub_norm, gdn_conv_w, gdn_A_log, gdn_dt_bias, gdn_out_norm, w_branch, w_out, norm_mem, norm_mem_src, w_mem_q, w_mem_kv, mem_q_norm, mem_k_norm, w_mem_o, norm_mlp, w_mlp_up, w_mlp_down):
    stacked = dict(norm_mix=norm_mix, w_in=w_in, fox_b_f=fox_b_f, fox_q_norm=fox_q_norm, fox_k_norm=fox_k_norm,
                   lru_conv_w=lru_conv_w, lru_conv_b=lru_conv_b, lru_w_a=lru_w_a, lru_b_a=lru_b_a,
                   lru_w_x=lru_w_x, lru_b_x=lru_b_x, lru_lambda=lru_lambda, diff_q_norm=diff_q_norm,
                   diff_k_norm=diff_k_norm, diff_lambda=diff_lambda, diff_sub_norm=diff_sub_norm,
                   gdn_conv_w=gdn_conv_w, gdn_A_log=gdn_A_log, gdn_dt_bias=gdn_dt_bias, gdn_out_norm=gdn_out_norm,
                   w_branch=w_branch, w_out=w_out, norm_mem=norm_mem, norm_mem_src=norm_mem_src, w_mem_q=w_mem_q,
                   w_mem_kv=w_mem_kv, mem_q_norm=mem_q_norm, mem_k_norm=mem_k_norm, w_mem_o=w_mem_o,
                   norm_mlp=norm_mlp, w_mlp_up=w_mlp_up, w_mlp_down=w_mlp_down)
    Bp, Tp, D = x_prompt.shape
    Bs, Ts, _ = x_sample.shape
    past_len = page_table.shape[1] * cache_fox_k.shape[2]
    rope_p = _rope_tables(jnp.arange(Tp))
    pos_s = past_len + jnp.arange(Ts)

    fox_k_t, fox_v_t, diff_k_t, diff_v_t = (_key_minor_pages(c) for c in (cache_fox_k, cache_fox_v, cache_diff_k,
                                                                      cache_diff_v))
    fox_logf_t = jnp.swapaxes(cache_fox_logf, 2, 3)

    xp, xs = x_prompt, x_sample
    pst = [[] for _ in range(11)]
    sst = [[] for _ in range(9)]
    for l in range(DEPTH):
        p = {name: arr[l] for name, arr in stacked.items()}
        pw = _pack_layer_weights(p)
        lam_init = 0.8 - 0.6 * math.exp(-0.3 * l)
        xp, st_p = _mixer_sublayer_prompt(xp, p, pw, lam_init, rope_p, row_tile=256, attn_tile=512, attn_chunk=512,
                                          prep_rows=256, tag="p_")
        mk, mv = _memory_kv(mem_prompt, p, pw)
        xp = _memory_sublayer(xp, mk, mv, p, pw, row_tile=512, name="p_mem")
        xp = _mlp_sublayer(xp.reshape(Bp * Tp, D), p['norm_mlp'], pw['w_mlp_up'], pw['w_mlp_down'], row_tile=512,
                           name="p_mlp").reshape(Bp, Tp, D)
        for lst, val in zip(pst, (st_p[0], st_p[1], st_p[2], st_p[3], st_p[4], mk, mv,
                                  st_p[5], st_p[6], st_p[7], st_p[8])):
            lst.append(val)
        fox_past = dict(k=fox_k_t, v=fox_v_t, logf=fox_logf_t, page_table=page_table, layer=l, pages_per_step=32)
        diff_past = dict(k=diff_k_t, v=diff_v_t, page_table=page_table, layer=l, pages_per_step=32)
        xs, st_s = _mixer_sublayer(xs, pos_s, p, pw, lam_init, fox_past, diff_past, state_lru_h[l], state_lru_conv[l],
                                   state_gdn_S[l], state_gdn_conv[l], row_tile=Bs * Ts, tag="s_")
        xs = _memory_sublayer(xs, cache_mem_k[l], cache_mem_v[l], p, pw, row_tile=8, name="s_mem")
        xs = _mlp_sublayer(xs.reshape(Bs * Ts, D), p['norm_mlp'], pw['w_mlp_up'], pw['w_mlp_down'], row_tile=Bs * Ts,
                           name="s_mlp").reshape(Bs, Ts, D)
        for lst, val in zip(sst, st_s):
            lst.append(val)

    return tuple([xp, xs] + [jnp.stack(v_, axis=0) for v_ in pst] + [jnp.stack(v_, axis=0) for v_ in sst])
```
